```python
import jax, jax.numpy as jnp
from jax import lax
import numpy as np

D_MODEL = 1024
BATCH = 4
SEQ = 4096
DEPTH = 1

CONV_WIDTH = 1024
CONV_KERNEL = 31
N_HEADS = 8
QK_NOPE_DIM = 128
QK_ROPE_DIM = 64
V_DIM = 128
Q_LORA_RANK = 256
KV_LORA_RANK = 256
ATTN_WIDTH = N_HEADS * V_DIM
ROPE_THETA = 10000.0
Q_BLOCK = 128
EPS = 1e-6

IN_SIZES = (
    CONV_WIDTH,
    CONV_WIDTH,
    CONV_WIDTH,
    Q_LORA_RANK,
    KV_LORA_RANK,
    QK_ROPE_DIM,
    ATTN_WIDTH,
    D_MODEL,
    D_MODEL,
)
IN_COLS = 3 * CONV_WIDTH + Q_LORA_RANK + KV_LORA_RANK + QK_ROPE_DIM + ATTN_WIDTH + 2 * D_MODEL

kernel_name = 'hybrid_conformer_mla_block'


def rms_norm(x, w):
    xf = x.astype(jnp.float32)
    y = xf * lax.rsqrt(jnp.mean(xf * xf, axis=-1, keepdims=True) + EPS)
    return (y * w.astype(jnp.float32)).astype(x.dtype)


def layer_norm(x, w, b):
    xf = x.astype(jnp.float32)
    mu = jnp.mean(xf, axis=-1, keepdims=True)
    xc = xf - mu
    var = jnp.mean(xc * xc, axis=-1, keepdims=True)
    y = xc * lax.rsqrt(var + EPS)
    return (y * w.astype(jnp.float32) + b.astype(jnp.float32)).astype(x.dtype)


def apply_rope(x, cos, sin):
    x1, x2 = jnp.split(x, 2, axis=-1)
    return jnp.concatenate([x1 * cos - x2 * sin, x1 * sin + x2 * cos], axis=-1).astype(x.dtype)


def causal_depthwise_conv(u, w, b):
    k, ch = w.shape
    out = lax.conv_general_dilated(
        u, w[:, None, :].astype(u.dtype), window_strides=(1,), padding=[(k - 1, 0)],
        dimension_numbers=('NWC', 'WIO', 'NWC'), feature_group_count=ch)
    return out + b


def mla_attention(q_nope, q_rope, k_nope, k_rope, v):
    b, s, h, _ = q_nope.shape
    nb = s // Q_BLOCK
    scale = (QK_NOPE_DIM + QK_ROPE_DIM) ** -0.5
    qn = q_nope.reshape(b, nb, Q_BLOCK, h, QK_NOPE_DIM).transpose(1, 0, 2, 3, 4)
    qr = q_rope.reshape(b, nb, Q_BLOCK, h, QK_ROPE_DIM).transpose(1, 0, 2, 3, 4)
    key_idx = jnp.arange(s)

    def block(args):
        qn_b, qr_b, i = args
        scores = (jnp.einsum('bqhd,bkhd->bhqk', qn_b, k_nope)
                  + jnp.einsum('bqhd,bkd->bhqk', qr_b, k_rope)).astype(jnp.float32) * scale
        q_idx = i * Q_BLOCK + jnp.arange(Q_BLOCK)
        mask = key_idx[None, :] <= q_idx[:, None]
        scores = jnp.where(mask[None, None], scores, -jnp.inf)
        p = jax.nn.softmax(scores, axis=-1).astype(v.dtype)
        return jnp.einsum('bhqk,bkhd->bqhd', p, v)

    out = lax.map(block, (qn, qr, jnp.arange(nb)))
    return out.transpose(1, 0, 2, 3, 4).reshape(b, s, h * V_DIM)


def setup_inputs(seed: int = 0) -> dict:
    key = jax.random.key(seed)
    ks = jax.random.split(key, 24)
    f32 = jnp.float32

    def nrm(k, shape, fan_in, mult=1.0):
        return jax.random.normal(k, shape, f32) * (mult * fan_in ** -0.5)

    def gain(k, shape):
        return 1.0 + 0.02 * jax.random.normal(k, shape, f32)

    x = jax.random.normal(ks[0], (BATCH, SEQ, D_MODEL), f32)
    c = jax.random.normal(ks[1], (BATCH, D_MODEL), f32)
    offsets = jax.random.randint(ks[2], (BATCH, 1), 0, 2048, jnp.int32)
    positions = (offsets + jnp.arange(SEQ, dtype=jnp.int32)[None, :]).astype(jnp.int32)
    return {
        'x': x,
        'c': c,
        'positions': positions,
        'w_ada': nrm(ks[3], (DEPTH, D_MODEL, 3 * D_MODEL), D_MODEL, 0.5),
        'b_ada': 0.02 * jax.random.normal(ks[4], (DEPTH, 3 * D_MODEL), f32),
        'norm_w': gain(ks[5], (DEPTH, D_MODEL)),
        'w_in': nrm(ks[6], (DEPTH, D_MODEL, IN_COLS), D_MODEL),
        'conv_w': nrm(ks[7], (DEPTH, CONV_KERNEL, CONV_WIDTH), CONV_KERNEL),
        'conv_b': 0.02 * jax.random.normal(ks[8], (DEPTH, CONV_WIDTH), f32),
        'conv_ln_w': gain(ks[9], (DEPTH, CONV_WIDTH)),
        'conv_ln_b': 0.02 * jax.random.normal(ks[10], (DEPTH, CONV_WIDTH), f32),
        'w_conv_out': nrm(ks[11], (DEPTH, CONV_WIDTH, D_MODEL), CONV_WIDTH),
        'q_norm_w': gain(ks[12], (DEPTH, Q_LORA_RANK)),
        'w_uq': nrm(ks[13], (DEPTH, Q_LORA_RANK, N_HEADS * (QK_NOPE_DIM + QK_ROPE_DIM)), Q_LORA_RANK),
        'kv_norm_w': gain(ks[14], (DEPTH, KV_LORA_RANK)),
        'w_ukv': nrm(ks[15], (DEPTH, KV_LORA_RANK, N_HEADS * (QK_NOPE_DIM + V_DIM)), KV_LORA_RANK),
        'w_attn_out': nrm(ks[16], (DEPTH, ATTN_WIDTH, D_MODEL), ATTN_WIDTH),
        'w_out': nrm(ks[17], (DEPTH, D_MODEL, D_MODEL), D_MODEL),
        'final_norm_w': gain(ks[18], (D_MODEL,)),
    }


def reference(x, c, positions, w_ada, b_ada, norm_w, w_in, conv_w, conv_b, conv_ln_w,
              conv_ln_b, w_conv_out, q_norm_w, w_uq, kv_norm_w, w_ukv, w_attn_out,
              w_out, final_norm_w):
    b, s, _ = x.shape
    split_points = []
    acc = 0
    for sz in IN_SIZES[:-1]:
        acc += sz
        split_points.append(acc)

    inv_freq = ROPE_THETA ** (-jnp.arange(0, QK_ROPE_DIM, 2, dtype=jnp.float32) / QK_ROPE_DIM)
    ang = positions.astype(jnp.float32)[..., None] * inv_freq
    cos, sin = jnp.cos(ang), jnp.sin(ang)

    c_act = jax.nn.silu(c)
    for l in range(DEPTH):
        shift, scale, gate = jnp.split(c_act @ w_ada[l] + b_ada[l], 3, axis=-1)
        h = rms_norm(x, norm_w[l]) * (1.0 + scale[:, None, :]) + shift[:, None, :]

        proj = h @ w_in[l]
        a_val, a_glu, a_gate, cq, ckv, k_pe, b_gate, g_a, g_b = jnp.split(proj, split_points, axis=-1)

        u = a_val * jax.nn.sigmoid(a_glu)
        u = causal_depthwise_conv(u, conv_w[l], conv_b[l])
        u = jax.nn.silu(layer_norm(u, conv_ln_w[l], conv_ln_b[l]))
        y_a = (u * jax.nn.silu(a_gate)) @ w_conv_out[l]

        q = (rms_norm(cq, q_norm_w[l]) @ w_uq[l]).reshape(b, s, N_HEADS, QK_NOPE_DIM + QK_ROPE_DIM)
        q_nope, q_rope = q[..., :QK_NOPE_DIM], q[..., QK_NOPE_DIM:]
        kv = (rms_norm(ckv, kv_norm_w[l]) @ w_ukv[l]).reshape(b, s, N_HEADS, QK_NOPE_DIM + V_DIM)
        k_nope, v = kv[..., :QK_NOPE_DIM], kv[..., QK_NOPE_DIM:]
        q_rope = apply_rope(q_rope, cos[:, :, None, :], sin[:, :, None, :])
        k_rope = apply_rope(k_pe, cos, sin)
        o = mla_attention(q_nope, q_rope, k_nope, k_rope, v)
        y_b = (o * jax.nn.silu(b_gate)) @ w_attn_out[l]

        merged = jax.nn.sigmoid(g_a) * y_a + jax.nn.sigmoid(g_b) * y_b
        x = x + gate[:, None, :] * (merged @ w_out[l])

    return rms_norm(x, final_norm_w)
```

```python
import functools
import math

import jax
import jax.numpy as jnp
from jax import lax
from jax.experimental import pallas as pl
from jax.experimental.pallas import tpu as pltpu

F32 = jnp.float32
BF16 = jnp.bfloat16

D_MODEL = 1024
CONV_WIDTH = 1024
CONV_KERNEL = 31
N_HEADS = 8
QK_NOPE_DIM = 128
QK_ROPE_DIM = 64
V_DIM = 128
Q_LORA_RANK = 256
KV_LORA_RANK = 256
ATTN_WIDTH = N_HEADS * V_DIM
ROPE_THETA = 10000.0
EPS = 1e-6

LANES = 128
QK_PAD = 2 * LANES
HALO = 32
VMEM_LIMIT = 56 * 1024 * 1024

Q_SCALE = (QK_NOPE_DIM + QK_ROPE_DIM) ** -0.5 * math.log2(math.e)

TM = 512
TC = 256
RC = 32
TQ = 512
TK = 512


def _sigmoid(x):
    return 1.0 / (1.0 + jnp.exp(-x))


def _silu(x):
    return x * _sigmoid(x)


def _rms(x, w):
    return x * lax.rsqrt(jnp.mean(x * x, axis=-1, keepdims=True) + EPS) * w


def _const_spec(shape):
    return pl.BlockSpec(shape, lambda *_: (0,) * len(shape), pipeline_mode=pl.Buffered(1))


def _ada_kernel(c_ref, w_ref, b_ref, o_ref):
    c = c_ref[...]
    o_ref[...] = jnp.dot(_silu(c), w_ref[...], preferred_element_type=F32) + b_ref[...]


def _ada(c_pad, w_ada, b_ada):
    rows, d = c_pad.shape
    n = w_ada.shape[1]
    tn = 512
    return pl.pallas_call(
        _ada_kernel,
        grid=(n // tn,),
        in_specs=[
            pl.BlockSpec((rows, d), lambda j: (0, 0)),
            pl.BlockSpec((d, tn), lambda j: (0, j)),
            pl.BlockSpec((1, tn), lambda j: (0, j)),
        ],
        out_specs=pl.BlockSpec((rows, tn), lambda j: (0, j)),
        out_shape=jax.ShapeDtypeStruct((rows, n), F32),
        compiler_params=pltpu.CompilerParams(dimension_semantics=("arbitrary",)),
        name="ada",
    )(c_pad, w_ada, b_ada)


def _modulated_norm(x_ref, shift_ref, scale_ref, nw_ref):
    x = x_ref[0]
    h = _rms(x, nw_ref[...]) * (1.0 + scale_ref[0]) + shift_ref[0]
    return h.astype(BF16)


def _gates_kernel(x_ref, shift_ref, scale_ref, nw_ref, w_ref,
                  u_ref, sa_ref, sb_ref, ga_ref, gb_ref):
    hb = _modulated_norm(x_ref, shift_ref, scale_ref, nw_ref)
    w = CONV_WIDTH

    def proj(k):
        return jnp.dot(hb, w_ref[:, k * w:(k + 1) * w], preferred_element_type=F32)

    u_ref[0] = (proj(0) * _sigmoid(proj(1))).astype(BF16)
    sa_ref[0] = _silu(proj(2)).astype(BF16)
    sb_ref[0] = _silu(proj(3)).astype(BF16)
    ga_ref[0] = _sigmoid(proj(4)).astype(BF16)
    gb_ref[0] = _sigmoid(proj(5)).astype(BF16)


def _gates(x, shift, scale, norm_w, w_g):
    b, s, d = x.shape
    tok = pl.BlockSpec((1, TM, d), lambda bi, i: (bi, i, 0))
    mod = pl.BlockSpec((1, 1, d), lambda bi, i: (bi, 0, 0))
    out = jax.ShapeDtypeStruct((b, s, d), BF16)
    return pl.pallas_call(
        _gates_kernel,
        grid=(b, s // TM),
        in_specs=[tok, mod, mod, _const_spec((1, d)), _const_spec(w_g.shape)],
        out_specs=[tok] * 5,
        out_shape=[out] * 5,
        compiler_params=pltpu.CompilerParams(
            dimension_semantics=("arbitrary", "arbitrary"), vmem_limit_bytes=VMEM_LIMIT),
        name="gates",
    )(x, shift, scale, norm_w, w_g)


def _mla_kernel(x_ref, shift_ref, scale_ref, nw_ref, pos_ref, invf_ref, w_ref,
                qn_ref, kvn_ref, wuq_ref, wuk_ref, wvt_ref,
                q_ref, k_ref, vt_ref):
    hb = _modulated_norm(x_ref, shift_ref, scale_ref, nw_ref)
    lat = jnp.dot(hb, w_ref[...], preferred_element_type=F32)
    cq = lat[:, :Q_LORA_RANK]
    ckv = lat[:, Q_LORA_RANK:Q_LORA_RANK + KV_LORA_RANK]
    kp = lat[:, Q_LORA_RANK + KV_LORA_RANK:]

    ang = pos_ref[0] * invf_ref[...]
    lane = lax.broadcasted_iota(jnp.int32, ang.shape, 1)
    first_half = lane < QK_ROPE_DIM
    table = jnp.where(first_half, jnp.cos(ang), jnp.sin(ang))

    def rope(t):
        return jnp.where(first_half, t + pltpu.roll(t, QK_ROPE_DIM, 1), 0.0)

    cqn = _rms(cq, qn_ref[...]).astype(BF16)
    ckvn = _rms(ckv, kvn_ref[...]).astype(BF16)

    qa = jnp.dot(cqn, wuq_ref[...], preferred_element_type=F32)
    kn = jnp.dot(ckvn, wuk_ref[...], preferred_element_type=F32)
    vt = lax.dot_general(wvt_ref[...], ckvn, (((1,), (1,)), ((), ())),
                         preferred_element_type=F32)

    k_hi = rope(kp * table).astype(BF16)
    q_table = table * Q_SCALE
    for h in range(N_HEADS):
        blk = qa[:, h * QK_PAD:(h + 1) * QK_PAD]
        q_ref[0, h, :, :LANES] = (blk[:, :LANES] * Q_SCALE).astype(BF16)
        q_ref[0, h, :, LANES:] = rope(blk[:, LANES:] * q_table).astype(BF16)
        k_ref[0, h, :, :LANES] = kn[:, h * LANES:(h + 1) * LANES].astype(BF16)
        k_ref[0, h, :, LANES:] = k_hi
    for j in range(TM // TK):
        vt_ref[0, j] = vt[:, j * TK:(j + 1) * TK].astype(BF16)


def _mla_proj(x, shift, scale, norm_w, pos, invf, w_m, qn_w, kvn_w, w_uq2, w_uk, w_vt):
    b, s, d = x.shape
    tok = pl.BlockSpec((1, TM, d), lambda bi, i: (bi, i, 0))
    mod = pl.BlockSpec((1, 1, d), lambda bi, i: (bi, 0, 0))
    qk_spec = pl.BlockSpec((1, N_HEADS, TM, QK_PAD), lambda bi, i: (bi, 0, i, 0))
    return pl.pallas_call(
        _mla_kernel,
        grid=(b, s // TM),
        in_specs=[
            tok, mod, mod, _const_spec((1, d)),
            pl.BlockSpec((1, TM, 1), lambda bi, i: (bi, i, 0)),
            _const_spec(invf.shape), _const_spec(w_m.shape),
            _const_spec(qn_w.shape), _const_spec(kvn_w.shape),
            _const_spec(w_uq2.shape), _const_spec(w_uk.shape), _const_spec(w_vt.shape),
        ],
        out_specs=[
            qk_spec, qk_spec,
            pl.BlockSpec((1, TM // TK, ATTN_WIDTH, TK), lambda bi, i: (bi, i, 0, 0)),
        ],
        out_shape=[
            jax.ShapeDtypeStruct((b, N_HEADS, s, QK_PAD), BF16),
            jax.ShapeDtypeStruct((b, N_HEADS, s, QK_PAD), BF16),
            jax.ShapeDtypeStruct((b, s // TK, ATTN_WIDTH, TK), BF16),
        ],
        compiler_params=pltpu.CompilerParams(
            dimension_semantics=("arbitrary", "arbitrary"), vmem_limit_bytes=VMEM_LIMIT),
        name="mla_proj",
    )(x, shift, scale, norm_w, pos, invf, w_m, qn_w, kvn_w, w_uq2, w_uk, w_vt)


def _conv_kernel(u_ref, uh_ref, sa_ref, cw_ref, cb_ref, lw_ref, lb_ref, o_ref, ext_ref):
    i = pl.program_id(1)
    halo = uh_ref[0].astype(F32)
    ext_ref[0:HALO, :] = jnp.where(i > 0, halo, 0.0)
    ext_ref[HALO:, :] = u_ref[0].astype(F32)
    first = HALO - (CONV_KERNEL - 1)
    for r in range(TC // RC):
        acc = jnp.broadcast_to(cb_ref[...], (RC, CONV_WIDTH))
        for j in range(CONV_KERNEL):
            lo = r * RC + first + j
            acc = acc + cw_ref[j:j + 1, :] * ext_ref[lo:lo + RC, :]
        mu = jnp.mean(acc, axis=-1, keepdims=True)
        xc = acc - mu
        var = jnp.mean(xc * xc, axis=-1, keepdims=True)
        y = xc * lax.rsqrt(var + EPS) * lw_ref[...] + lb_ref[...]
        rows = slice(r * RC, (r + 1) * RC)
        o_ref[0, rows, :] = (_silu(y) * sa_ref[0, rows, :].astype(F32)).astype(BF16)


def _conv(u, sa, conv_w, conv_b, ln_w, ln_b):
    b, s, c = u.shape
    tok = pl.BlockSpec((1, TC, c), lambda bi, i: (bi, i, 0))
    halo = pl.BlockSpec((1, HALO, c), lambda bi, i: (bi, jnp.maximum(i * (TC // HALO) - 1, 0), 0))
    vec = _const_spec((1, c))
    return pl.pallas_call(
        _conv_kernel,
        grid=(b, s // TC),
        in_specs=[tok, halo, tok, _const_spec(conv_w.shape), vec, vec, vec],
        out_specs=tok,
        out_shape=jax.ShapeDtypeStruct((b, s, c), BF16),
        scratch_shapes=[pltpu.VMEM((HALO + TC, c), F32)],
        compiler_params=pltpu.CompilerParams(
            dimension_semantics=("arbitrary", "arbitrary"), vmem_limit_bytes=VMEM_LIMIT),
        name="conv",
    )(u, u, sa, conv_w, conv_b, ln_w, ln_b)


def _attn_kernel(q_ref, k_ref, vt_ref, o_ref, acc_ref):
    qi = pl.program_id(2)
    q = q_ref[0, 0]
    acc_ref[...] = jnp.zeros_like(acc_ref)
    key_row = lax.broadcasted_iota(jnp.int32, (TK, TQ), 0)
    qry_col = lax.broadcasted_iota(jnp.int32, (TK, TQ), 1)

    def body(j, carry):
        m, l = carry
        k = k_ref[0, 0, pl.ds(pl.multiple_of(j * TK, TK), TK), :]
        s = lax.dot_general(k, q, (((1,), (1,)), ((), ())),
                            preferred_element_type=F32)
        visible = key_row + j * TK <= qry_col + qi * TQ
        s = jnp.where(visible, s, -jnp.inf)
        m_new = jnp.maximum(m, jnp.max(s, axis=0, keepdims=True))
        alpha = jnp.exp2(m - m_new)
        p = jnp.exp2(s - m_new)
        l = alpha * l + jnp.sum(p, axis=0, keepdims=True)
        pv = jnp.dot(vt_ref[0, j], p.astype(BF16), preferred_element_type=F32)
        acc_ref[...] = alpha * acc_ref[...] + pv
        return m_new, l

    n_kv = (qi + 1) * (TQ // TK)
    m0 = jnp.full((1, TQ), -jnp.inf, F32)
    l0 = jnp.zeros((1, TQ), F32)
    _, l = lax.fori_loop(0, n_kv, body, (m0, l0))
    o_ref[0] = jnp.transpose(acc_ref[...] / l).astype(BF16)


def _attention(q, k, vt):
    b, h, s, _ = q.shape
    return pl.pallas_call(
        _attn_kernel,
        grid=(b, h, s // TQ),
        in_specs=[
            pl.BlockSpec((1, 1, TQ, QK_PAD), lambda bi, hi, qi: (bi, hi, qi, 0)),
            pl.BlockSpec((1, 1, s, QK_PAD), lambda bi, hi, qi: (bi, hi, 0, 0)),
            pl.BlockSpec((1, s // TK, V_DIM, TK), lambda bi, hi, qi: (bi, 0, hi, 0)),
        ],
        out_specs=pl.BlockSpec((1, TQ, V_DIM), lambda bi, hi, qi: (bi, qi, hi)),
        out_shape=jax.ShapeDtypeStruct((b, s, h * V_DIM), BF16),
        scratch_shapes=[pltpu.VMEM((V_DIM, TQ), F32)],
        compiler_params=pltpu.CompilerParams(
            dimension_semantics=("arbitrary", "arbitrary", "arbitrary"),
            vmem_limit_bytes=VMEM_LIMIT),
        name="attn",
    )(q, k, vt)


def _out_kernel(ug_ref, o_ref, sb_ref, ga_ref, gb_ref, x_ref, gate_ref,
                wco_ref, wao_ref, wo_ref, fw_ref, y_ref):
    ya = jnp.dot(ug_ref[0], wco_ref[...], preferred_element_type=F32)
    og = (o_ref[0].astype(F32) * sb_ref[0].astype(F32)).astype(BF16)
    yb = jnp.dot(og, wao_ref[...], preferred_element_type=F32)
    merged = ga_ref[0].astype(F32) * ya + gb_ref[0].astype(F32) * yb
    delta = jnp.dot(merged.astype(BF16), wo_ref[...], preferred_element_type=F32)
    y = x_ref[0] + gate_ref[0] * delta
    y_ref[0] = _rms(y, fw_ref[...])


def _out(ug, o, sb, ga, gb, x, gate, w_co, w_ao, w_o, fw):
    b, s, d = x.shape
    tok = pl.BlockSpec((1, TM, d), lambda bi, i: (bi, i, 0))
    mod = pl.BlockSpec((1, 1, d), lambda bi, i: (bi, 0, 0))
    wspec = _const_spec((d, d))
    return pl.pallas_call(
        _out_kernel,
        grid=(b, s // TM),
        in_specs=[tok, tok, tok, tok, tok, tok, mod, wspec, wspec, wspec, _const_spec((1, d))],
        out_specs=tok,
        out_shape=jax.ShapeDtypeStruct((b, s, d), F32),
        compiler_params=pltpu.CompilerParams(
            dimension_semantics=("arbitrary", "arbitrary"), vmem_limit_bytes=VMEM_LIMIT),
        name="out",
    )(ug, o, sb, ga, gb, x, gate, w_co, w_ao, w_o, fw)


def _rot_cols(w):
    half = w.shape[-1] // 2
    return jnp.concatenate([-w[..., half:], w[..., :half]], axis=-1)


def _layer(x, c_act_pad, pos, invf, w_ada, b_ada, norm_w, w_in, conv_w, conv_b, conv_ln_w,
           conv_ln_b, w_conv_out, q_norm_w, w_uq, kv_norm_w, w_ukv, w_attn_out, w_out, final_w):
    b, s, d = x.shape
    mod = _ada(c_act_pad, w_ada, b_ada[None, :])[:b]
    shift, scale, gate = (m[:, None, :] for m in jnp.split(mod, 3, axis=-1))

    o_cq = 3 * CONV_WIDTH
    o_kpe = o_cq + Q_LORA_RANK + KV_LORA_RANK
    o_bg = o_kpe + QK_ROPE_DIM
    w_kpe = w_in[:, o_kpe:o_bg]
    w_g = jnp.concatenate([w_in[:, :o_cq], w_in[:, o_bg:]], axis=1).astype(BF16)
    w_m = jnp.concatenate([w_in[:, o_cq:o_kpe], w_kpe, _rot_cols(w_kpe)], axis=1).astype(BF16)

    wq = w_uq.reshape(Q_LORA_RANK, N_HEADS, QK_NOPE_DIM + QK_ROPE_DIM)
    wq_rope = wq[..., QK_NOPE_DIM:]
    w_uq2 = jnp.concatenate([wq, _rot_cols(wq_rope)], axis=-1)
    w_uq2 = w_uq2.reshape(Q_LORA_RANK, N_HEADS * QK_PAD).astype(BF16)
    wkv = w_ukv.reshape(KV_LORA_RANK, N_HEADS, QK_NOPE_DIM + V_DIM)
    w_uk = wkv[..., :QK_NOPE_DIM].reshape(KV_LORA_RANK, N_HEADS * QK_NOPE_DIM).astype(BF16)
    w_vt = wkv[..., QK_NOPE_DIM:].reshape(KV_LORA_RANK, ATTN_WIDTH).T.astype(BF16)

    nw = norm_w[None, :]
    u, sa, sb, ga, gb = _gates(x, shift, scale, nw, w_g)
    q, k, vt = _mla_proj(x, shift, scale, nw, pos, invf, w_m, q_norm_w[None, :],
                         kv_norm_w[None, :], w_uq2, w_uk, w_vt)
    cw = jnp.pad(conv_w, ((0, HALO - CONV_KERNEL), (0, 0)))
    ug = _conv(u, sa, cw, conv_b[None, :], conv_ln_w[None, :], conv_ln_b[None, :])
    o = _attention(q, k, vt)
    return _out(ug, o, sb, ga, gb, x, gate, w_conv_out.astype(BF16), w_attn_out.astype(BF16),
                w_out.astype(BF16), final_w[None, :])


def kernel(x, c, positions, w_ada, b_ada, norm_w, w_in, conv_w, conv_b, conv_ln_w, conv_ln_b,
           w_conv_out, q_norm_w, w_uq, kv_norm_w, w_ukv, w_attn_out, w_out, final_norm_w):
    b, s, d = x.shape
    depth = w_ada.shape[0]
    assert depth == 1, "final rmsnorm is fused into the single layer's output kernel"
    inv_freq = ROPE_THETA ** (-jnp.arange(0, QK_ROPE_DIM, 2, dtype=F32) / QK_ROPE_DIM)
    invf = jnp.tile(inv_freq, LANES // inv_freq.shape[0])[None, :]
    pos = positions.astype(F32)[..., None]
    c_pad = jnp.pad(c, ((0, 8 - b), (0, 0)))
    return _layer(x, c_pad, pos, invf, w_ada[0], b_ada[0], norm_w[0], w_in[0], conv_w[0],
                  conv_b[0], conv_ln_w[0], conv_ln_b[0], w_conv_out[0], q_norm_w[0], w_uq[0],
                  kv_norm_w[0], w_ukv[0], w_attn_out[0], w_out[0], final_norm_w)
```

```python
import functools
import math

import jax
import jax.numpy as jnp
from jax import lax
from jax.experimental import pallas as pl
from jax.experimental.pallas import tpu as pltpu

F32 = jnp.float32
BF16 = jnp.bfloat16

D_MODEL = 1024
CONV_WIDTH = 1024
CONV_KERNEL = 31
N_HEADS = 8
QK_NOPE_DIM = 128
QK_ROPE_DIM = 64
V_DIM = 128
Q_LORA_RANK = 256
KV_LORA_RANK = 256
ATTN_WIDTH = N_HEADS * V_DIM
ROPE_THETA = 10000.0
EPS = 1e-6

LANES = 128
QK_PAD = 2 * LANES
HALO = 32
VMEM_LIMIT = 56 * 1024 * 1024

Q_SCALE = (QK_NOPE_DIM + QK_ROPE_DIM) ** -0.5 * math.log2(math.e)

TM = 512
TC = 256
RC = 64
CONV_LANES = 256
SUBLANES = 8
SHIFT_ROWS = TC + HALO - SUBLANES
SHIFT_BLOCK = 40
assert SHIFT_ROWS % SHIFT_BLOCK == 0 and SHIFT_BLOCK % SUBLANES == 0
TQ = 512
TK = 512


def _sigmoid(x):
    return 1.0 / (1.0 + jnp.exp(-x))


def _silu(x):
    return x * _sigmoid(x)


def _rms(x, w):
    return x * lax.rsqrt(jnp.mean(x * x, axis=-1, keepdims=True) + EPS) * w


def _const_spec(shape):
    return pl.BlockSpec(shape, lambda *_: (0,) * len(shape), pipeline_mode=pl.Buffered(1))


def _ada_kernel(c_ref, w_ref, b_ref, o_ref):
    c = c_ref[...]
    o_ref[...] = jnp.dot(_silu(c), w_ref[...], preferred_element_type=F32) + b_ref[...]


def _ada(c_pad, w_ada, b_ada):
    rows, d = c_pad.shape
    n = w_ada.shape[1]
    tn = 512
    return pl.pallas_call(
        _ada_kernel,
        grid=(n // tn,),
        in_specs=[
            pl.BlockSpec((rows, d), lambda j: (0, 0)),
            pl.BlockSpec((d, tn), lambda j: (0, j)),
            pl.BlockSpec((1, tn), lambda j: (0, j)),
        ],
        out_specs=pl.BlockSpec((rows, tn), lambda j: (0, j)),
        out_shape=jax.ShapeDtypeStruct((rows, n), F32),
        compiler_params=pltpu.CompilerParams(dimension_semantics=("arbitrary",)),
        name="ada",
    )(c_pad, w_ada, b_ada)


def _modulated_norm(x_ref, shift_ref, scale_ref, nw_ref):
    x = x_ref[0]
    h = _rms(x, nw_ref[...]) * (1.0 + scale_ref[0]) + shift_ref[0]
    return h.astype(BF16)


def _gates_kernel(x_ref, shift_ref, scale_ref, nw_ref, w_ref,
                  u_ref, sa_ref, sb_ref, ga_ref, gb_ref):
    hb = _modulated_norm(x_ref, shift_ref, scale_ref, nw_ref)
    w = CONV_WIDTH

    def proj(k):
        return jnp.dot(hb, w_ref[:, k * w:(k + 1) * w], preferred_element_type=F32)

    u_ref[0] = (proj(0) * _sigmoid(proj(1))).astype(BF16)
    sa_ref[0] = _silu(proj(2)).astype(BF16)
    sb_ref[0] = _silu(proj(3)).astype(BF16)
    ga_ref[0] = _sigmoid(proj(4)).astype(BF16)
    gb_ref[0] = _sigmoid(proj(5)).astype(BF16)


def _gates(x, shift, scale, norm_w, w_g):
    b, s, d = x.shape
    tok = pl.BlockSpec((1, TM, d), lambda bi, i: (bi, i, 0))
    mod = pl.BlockSpec((1, 1, d), lambda bi, i: (bi, 0, 0))
    out = jax.ShapeDtypeStruct((b, s, d), BF16)
    return pl.pallas_call(
        _gates_kernel,
        grid=(b, s // TM),
        in_specs=[tok, mod, mod, _const_spec((1, d)), _const_spec(w_g.shape)],
        out_specs=[tok] * 5,
        out_shape=[out] * 5,
        compiler_params=pltpu.CompilerParams(
            dimension_semantics=("arbitrary", "arbitrary"), vmem_limit_bytes=VMEM_LIMIT),
        name="gates",
    )(x, shift, scale, norm_w, w_g)


def _mla_kernel(x_ref, shift_ref, scale_ref, nw_ref, pos_ref, invf_ref, w_ref,
                qn_ref, kvn_ref, wuq_ref, wuk_ref, wvt_ref,
                q_ref, k_ref, vt_ref):
    hb = _modulated_norm(x_ref, shift_ref, scale_ref, nw_ref)
    lat = jnp.dot(hb, w_ref[...], preferred_element_type=F32)
    cq = lat[:, :Q_LORA_RANK]
    ckv = lat[:, Q_LORA_RANK:Q_LORA_RANK + KV_LORA_RANK]
    kp = lat[:, Q_LORA_RANK + KV_LORA_RANK:]

    ang = pos_ref[0] * invf_ref[...]
    lane = lax.broadcasted_iota(jnp.int32, ang.shape, 1)
    first_half = lane < QK_ROPE_DIM
    table = jnp.where(first_half, jnp.cos(ang), jnp.sin(ang))

    def rope(t):
        return jnp.where(first_half, t + pltpu.roll(t, QK_ROPE_DIM, 1), 0.0)

    cqn = _rms(cq, qn_ref[...]).astype(BF16)
    ckvn = _rms(ckv, kvn_ref[...]).astype(BF16)

    qa = jnp.dot(cqn, wuq_ref[...], preferred_element_type=F32)
    kn = jnp.dot(ckvn, wuk_ref[...], preferred_element_type=F32)
    vt = lax.dot_general(wvt_ref[...], ckvn, (((1,), (1,)), ((), ())),
                         preferred_element_type=F32)

    k_hi = rope(kp * table).astype(BF16)
    q_table = table * Q_SCALE
    for h in range(N_HEADS):
        blk = qa[:, h * QK_PAD:(h + 1) * QK_PAD]
        q_ref[0, h, :, :LANES] = (blk[:, :LANES] * Q_SCALE).astype(BF16)
        q_ref[0, h, :, LANES:] = rope(blk[:, LANES:] * q_table).astype(BF16)
        k_ref[0, h, :, :LANES] = kn[:, h * LANES:(h + 1) * LANES].astype(BF16)
        k_ref[0, h, :, LANES:] = k_hi
    for j in range(TM // TK):
        vt_ref[0, j] = vt[:, j * TK:(j + 1) * TK].astype(BF16)


def _mla_proj(x, shift, scale, norm_w, pos, invf, w_m, qn_w, kvn_w, w_uq2, w_uk, w_vt):
    b, s, d = x.shape
    tok = pl.BlockSpec((1, TM, d), lambda bi, i: (bi, i, 0))
    mod = pl.BlockSpec((1, 1, d), lambda bi, i: (bi, 0, 0))
    qk_spec = pl.BlockSpec((1, N_HEADS, TM, QK_PAD), lambda bi, i: (bi, 0, i, 0))
    return pl.pallas_call(
        _mla_kernel,
        grid=(b, s // TM),
        in_specs=[
            tok, mod, mod, _const_spec((1, d)),
            pl.BlockSpec((1, TM, 1), lambda bi, i: (bi, i, 0)),
            _const_spec(invf.shape), _const_spec(w_m.shape),
            _const_spec(qn_w.shape), _const_spec(kvn_w.shape),
            _const_spec(w_uq2.shape), _const_spec(w_uk.shape), _const_spec(w_vt.shape),
        ],
        out_specs=[
            qk_spec, qk_spec,
            pl.BlockSpec((1, TM // TK, ATTN_WIDTH, TK), lambda bi, i: (bi, i, 0, 0)),
        ],
        out_shape=[
            jax.ShapeDtypeStruct((b, N_HEADS, s, QK_PAD), BF16),
            jax.ShapeDtypeStruct((b, N_HEADS, s, QK_PAD), BF16),
            jax.ShapeDtypeStruct((b, s // TK, ATTN_WIDTH, TK), BF16),
        ],
        compiler_params=pltpu.CompilerParams(
            dimension_semantics=("arbitrary", "arbitrary"), vmem_limit_bytes=VMEM_LIMIT),
        name="mla_proj",
    )(x, shift, scale, norm_w, pos, invf, w_m, qn_w, kvn_w, w_uq2, w_uk, w_vt)


def _conv_kernel(u_ref, uh_ref, sa_ref, cw_ref, cb_ref, lw_ref, lb_ref, o_ref, sh_ref, cv_ref):
    i = pl.program_id(1)
    n_lc = CONV_WIDTH // CONV_LANES
    first = HALO - (CONV_KERNEL - 1)
    for lc in range(n_lc):
        lanes = slice(lc * CONV_LANES, (lc + 1) * CONV_LANES)
        halo = uh_ref[0, :, lanes].astype(F32)
        sh_ref[lc, 0, 0:HALO, :] = jnp.where(i > 0, halo, 0.0)
        sh_ref[lc, 0, HALO:, :] = u_ref[0, :, lanes].astype(F32)
        for r in range(1, SUBLANES):
            for rb in range(0, SHIFT_ROWS, SHIFT_BLOCK):
                sh_ref[lc, r, rb:rb + SHIFT_BLOCK, :] = sh_ref[lc, 0, rb + r:rb + r + SHIFT_BLOCK, :]

    groups = RC // SUBLANES

    def row_chunk(c, carry):
        row0 = pl.multiple_of(c * RC, RC)

        def lane_chunk(lc, carry2):
            acc = jnp.zeros((groups, SUBLANES, CONV_LANES), F32)
            for r in range(SUBLANES):
                offs = [o for o in range(first, first + CONV_KERNEL) if o % SUBLANES == r]
                g0, g1 = offs[0] // SUBLANES, offs[-1] // SUBLANES
                span = groups + g1 - g0
                start = pl.multiple_of(row0 + g0 * SUBLANES, SUBLANES)
                xs = sh_ref[lc, r, pl.ds(start, span * SUBLANES), :]
                xs = xs.reshape(span, SUBLANES, CONV_LANES)
                for o in offs:
                    g = o // SUBLANES - g0
                    acc = acc + cw_ref[lc, o - first][None] * xs[g:g + groups]
            cv_ref[lc] = acc.reshape(RC, CONV_LANES)
            return carry2

        lax.fori_loop(0, n_lc, lane_chunk, 0)
        acc = jnp.concatenate([cv_ref[lc] for lc in range(n_lc)], axis=1) + cb_ref[...]
        mu = jnp.mean(acc, axis=-1, keepdims=True)
        xc = acc - mu
        var = jnp.mean(xc * xc, axis=-1, keepdims=True)
        y = xc * lax.rsqrt(var + EPS) * lw_ref[...] + lb_ref[...]
        gate = sa_ref[0, pl.ds(row0, RC), :].astype(F32)
        o_ref[0, pl.ds(row0, RC), :] = (_silu(y) * gate).astype(BF16)
        return carry

    lax.fori_loop(0, TC // RC, row_chunk, 0)


def _conv(u, sa, conv_w, conv_b, ln_w, ln_b):
    b, s, c = u.shape
    n_lc = c // CONV_LANES
    tok = pl.BlockSpec((1, TC, c), lambda bi, i: (bi, i, 0))
    halo = pl.BlockSpec((1, HALO, c), lambda bi, i: (bi, jnp.maximum(i * (TC // HALO) - 1, 0), 0))
    vec = _const_spec((1, c))
    return pl.pallas_call(
        _conv_kernel,
        grid=(b, s // TC),
        in_specs=[tok, halo, tok, _const_spec(conv_w.shape), vec, vec, vec],
        out_specs=tok,
        out_shape=jax.ShapeDtypeStruct((b, s, c), BF16),
        scratch_shapes=[pltpu.VMEM((n_lc, SUBLANES, HALO + TC, CONV_LANES), F32),
                        pltpu.VMEM((n_lc, RC, CONV_LANES), F32)],
        compiler_params=pltpu.CompilerParams(
            dimension_semantics=("arbitrary", "arbitrary"), vmem_limit_bytes=VMEM_LIMIT),
        name="conv",
    )(u, u, sa, conv_w, conv_b, ln_w, ln_b)


def _attn_kernel(q_ref, k_ref, vt_ref, o_ref, acc_ref):
    qi = pl.program_id(2)
    q = q_ref[0, 0]
    acc_ref[...] = jnp.zeros_like(acc_ref)
    key_row = lax.broadcasted_iota(jnp.int32, (TK, TQ), 0)
    qry_col = lax.broadcasted_iota(jnp.int32, (TK, TQ), 1)

    def body(j, carry):
        m, l = carry
        k = k_ref[0, 0, pl.ds(pl.multiple_of(j * TK, TK), TK), :]
        s = lax.dot_general(k, q, (((1,), (1,)), ((), ())),
                            preferred_element_type=F32)
        visible = key_row + j * TK <= qry_col + qi * TQ
        s = jnp.where(visible, s, -jnp.inf)
        m_new = jnp.maximum(m, jnp.max(s, axis=0, keepdims=True))
        alpha = jnp.exp2(m - m_new)
        p = jnp.exp2(s - m_new)
        l = alpha * l + jnp.sum(p, axis=0, keepdims=True)
        pv = jnp.dot(vt_ref[0, j], p.astype(BF16), preferred_element_type=F32)
        acc_ref[...] = alpha * acc_ref[...] + pv
        return m_new, l

    n_kv = (qi + 1) * (TQ // TK)
    m0 = jnp.full((1, TQ), -jnp.inf, F32)
    l0 = jnp.zeros((1, TQ), F32)
    _, l = lax.fori_loop(0, n_kv, body, (m0, l0))
    o_ref[0] = jnp.transpose(acc_ref[...] / l).astype(BF16)


def _attention(q, k, vt):
    b, h, s, _ = q.shape
    return pl.pallas_call(
        _attn_kernel,
        grid=(b, h, s // TQ),
        in_specs=[
            pl.BlockSpec((1, 1, TQ, QK_PAD), lambda bi, hi, qi: (bi, hi, qi, 0)),
            pl.BlockSpec((1, 1, s, QK_PAD), lambda bi, hi, qi: (bi, hi, 0, 0)),
            pl.BlockSpec((1, s // TK, V_DIM, TK), lambda bi, hi, qi: (bi, 0, hi, 0)),
        ],
        out_specs=pl.BlockSpec((1, TQ, V_DIM), lambda bi, hi, qi: (bi, qi, hi)),
        out_shape=jax.ShapeDtypeStruct((b, s, h * V_DIM), BF16),
        scratch_shapes=[pltpu.VMEM((V_DIM, TQ), F32)],
        compiler_params=pltpu.CompilerParams(
            dimension_semantics=("arbitrary", "arbitrary", "arbitrary"),
            vmem_limit_bytes=VMEM_LIMIT),
        name="attn",
    )(q, k, vt)


def _out_kernel(ug_ref, o_ref, sb_ref, ga_ref, gb_ref, x_ref, gate_ref,
                wco_ref, wao_ref, wo_ref, fw_ref, y_ref):
    ya = jnp.dot(ug_ref[0], wco_ref[...], preferred_element_type=F32)
    og = (o_ref[0].astype(F32) * sb_ref[0].astype(F32)).astype(BF16)
    yb = jnp.dot(og, wao_ref[...], preferred_element_type=F32)
    merged = ga_ref[0].astype(F32) * ya + gb_ref[0].astype(F32) * yb
    delta = jnp.dot(merged.astype(BF16), wo_ref[...], preferred_element_type=F32)
    y = x_ref[0] + gate_ref[0] * delta
    y_ref[0] = _rms(y, fw_ref[...])


def _out(ug, o, sb, ga, gb, x, gate, w_co, w_ao, w_o, fw):
    b, s, d = x.shape
    tok = pl.BlockSpec((1, TM, d), lambda bi, i: (bi, i, 0))
    mod = pl.BlockSpec((1, 1, d), lambda bi, i: (bi, 0, 0))
    wspec = _const_spec((d, d))
    return pl.pallas_call(
        _out_kernel,
        grid=(b, s // TM),
        in_specs=[tok, tok, tok, tok, tok, tok, mod, wspec, wspec, wspec, _const_spec((1, d))],
        out_specs=tok,
        out_shape=jax.ShapeDtypeStruct((b, s, d), F32),
        compiler_params=pltpu.CompilerParams(
            dimension_semantics=("arbitrary", "arbitrary"), vmem_limit_bytes=VMEM_LIMIT),
        name="out",
    )(ug, o, sb, ga, gb, x, gate, w_co, w_ao, w_o, fw)


def _rot_cols(w):
    half = w.shape[-1] // 2
    return jnp.concatenate([-w[..., half:], w[..., :half]], axis=-1)


def _layer(x, c_act_pad, pos, invf, w_ada, b_ada, norm_w, w_in, conv_w, conv_b, conv_ln_w,
           conv_ln_b, w_conv_out, q_norm_w, w_uq, kv_norm_w, w_ukv, w_attn_out, w_out, final_w):
    b, s, d = x.shape
    mod = _ada(c_act_pad, w_ada, b_ada[None, :])[:b]
    shift, scale, gate = (m[:, None, :] for m in jnp.split(mod, 3, axis=-1))

    o_cq = 3 * CONV_WIDTH
    o_kpe = o_cq + Q_LORA_RANK + KV_LORA_RANK
    o_bg = o_kpe + QK_ROPE_DIM
    w_kpe = w_in[:, o_kpe:o_bg]
    w_g = jnp.concatenate([w_in[:, :o_cq], w_in[:, o_bg:]], axis=1).astype(BF16)
    w_m = jnp.concatenate([w_in[:, o_cq:o_kpe], w_kpe, _rot_cols(w_kpe)], axis=1).astype(BF16)

    wq = w_uq.reshape(Q_LORA_RANK, N_HEADS, QK_NOPE_DIM + QK_ROPE_DIM)
    wq_rope = wq[..., QK_NOPE_DIM:]
    w_uq2 = jnp.concatenate([wq, _rot_cols(wq_rope)], axis=-1)
    w_uq2 = w_uq2.reshape(Q_LORA_RANK, N_HEADS * QK_PAD).astype(BF16)
    wkv = w_ukv.reshape(KV_LORA_RANK, N_HEADS, QK_NOPE_DIM + V_DIM)
    w_uk = wkv[..., :QK_NOPE_DIM].reshape(KV_LORA_RANK, N_HEADS * QK_NOPE_DIM).astype(BF16)
    w_vt = wkv[..., QK_NOPE_DIM:].reshape(KV_LORA_RANK, ATTN_WIDTH).T.astype(BF16)

    nw = norm_w[None, :]
    u, sa, sb, ga, gb = _gates(x, shift, scale, nw, w_g)
    q, k, vt = _mla_proj(x, shift, scale, nw, pos, invf, w_m, q_norm_w[None, :],
                         kv_norm_w[None, :], w_uq2, w_uk, w_vt)
    cw = conv_w.reshape(CONV_KERNEL, CONV_WIDTH // CONV_LANES, 1, CONV_LANES).transpose(1, 0, 2, 3)
    cw = jnp.broadcast_to(cw, (CONV_WIDTH // CONV_LANES, CONV_KERNEL, SUBLANES, CONV_LANES))
    ug = _conv(u, sa, cw, conv_b[None, :], conv_ln_w[None, :], conv_ln_b[None, :])
    o = _attention(q, k, vt)
    return _out(ug, o, sb, ga, gb, x, gate, w_conv_out.astype(BF16), w_attn_out.astype(BF16),
                w_out.astype(BF16), final_w[None, :])


def kernel(x, c, positions, w_ada, b_ada, norm_w, w_in, conv_w, conv_b, conv_ln_w, conv_ln_b,
           w_conv_out, q_norm_w, w_uq, kv_norm_w, w_ukv, w_attn_out, w_out, final_norm_w):
    b, s, d = x.shape
    depth = w_ada.shape[0]
    assert depth == 1, "final rmsnorm is fused into the single layer's output kernel"
    inv_freq = ROPE_THETA ** (-jnp.arange(0, QK_ROPE_DIM, 2, dtype=F32) / QK_ROPE_DIM)
    invf = jnp.tile(inv_freq, LANES // inv_freq.shape[0])[None, :]
    pos = positions.astype(F32)[..., None]
    c_pad = jnp.pad(c, ((0, 8 - b), (0, 0)))
    return _layer(x, c_pad, pos, invf, w_ada[0], b_ada[0], norm_w[0], w_in[0], conv_w[0],
                  conv_b[0], conv_ln_w[0], conv_ln_b[0], w_conv_out[0], q_norm_w[0], w_uq[0],
                  kv_norm_w[0], w_ukv[0], w_attn_out[0], w_out[0], final_norm_w)
```

```python
import functools
import math

import jax
import jax.numpy as jnp
from jax import lax
from jax.experimental import pallas as pl
from jax.experimental.pallas import tpu as pltpu

F32 = jnp.float32
BF16 = jnp.bfloat16

D_MODEL = 1024
CONV_WIDTH = 1024
CONV_KERNEL = 31
N_HEADS = 8
QK_NOPE_DIM = 128
QK_ROPE_DIM = 64
V_DIM = 128
Q_LORA_RANK = 256
KV_LORA_RANK = 256
ATTN_WIDTH = N_HEADS * V_DIM
ROPE_THETA = 10000.0
EPS = 1e-6

LANES = 128
QK_PAD = 2 * LANES
BF16_SUBLANES = 16
V_PAD = V_DIM + BF16_SUBLANES
HALO = 32
VMEM_LIMIT = 56 * 1024 * 1024

Q_SCALE = (QK_NOPE_DIM + QK_ROPE_DIM) ** -0.5 * math.log2(math.e)

TM = 512
TC = 256
RC = 64
CONV_LANES = 256
SUBLANES = 8
SHIFT_ROWS = TC + HALO - SUBLANES
SHIFT_BLOCK = 40
assert SHIFT_ROWS % SHIFT_BLOCK == 0 and SHIFT_BLOCK % SUBLANES == 0
TQ = 512
TK = 512


def _sigmoid(x):
    return 1.0 / (1.0 + jnp.exp(-x))


def _silu(x):
    return x * _sigmoid(x)


def _rms(x, w):
    return x * lax.rsqrt(jnp.mean(x * x, axis=-1, keepdims=True) + EPS) * w


def _const_spec(shape):
    return pl.BlockSpec(shape, lambda *_: (0,) * len(shape), pipeline_mode=pl.Buffered(1))


def _ada_kernel(c_ref, w_ref, b_ref, o_ref):
    c = c_ref[...]
    o_ref[...] = jnp.dot(_silu(c), w_ref[...], preferred_element_type=F32) + b_ref[...]


def _ada(c_pad, w_ada, b_ada):
    rows, d = c_pad.shape
    n = w_ada.shape[1]
    tn = 512
    return pl.pallas_call(
        _ada_kernel,
        grid=(n // tn,),
        in_specs=[
            pl.BlockSpec((rows, d), lambda j: (0, 0)),
            pl.BlockSpec((d, tn), lambda j: (0, j)),
            pl.BlockSpec((1, tn), lambda j: (0, j)),
        ],
        out_specs=pl.BlockSpec((rows, tn), lambda j: (0, j)),
        out_shape=jax.ShapeDtypeStruct((rows, n), F32),
        compiler_params=pltpu.CompilerParams(dimension_semantics=("arbitrary",)),
        name="ada",
    )(c_pad, w_ada, b_ada)


def _modulated_norm(x_ref, shift_ref, scale_ref, nw_ref):
    x = x_ref[0]
    h = _rms(x, nw_ref[...]) * (1.0 + scale_ref[0]) + shift_ref[0]
    return h.astype(BF16)


def _gates_kernel(x_ref, shift_ref, scale_ref, nw_ref, w_ref,
                  u_ref, sa_ref, sb_ref, ga_ref, gb_ref):
    hb = _modulated_norm(x_ref, shift_ref, scale_ref, nw_ref)
    w = CONV_WIDTH

    def proj(k):
        return jnp.dot(hb, w_ref[:, k * w:(k + 1) * w], preferred_element_type=F32)

    u_ref[0] = (proj(0) * _sigmoid(proj(1))).astype(BF16)
    sa_ref[0] = _silu(proj(2)).astype(BF16)
    sb_ref[0] = _silu(proj(3)).astype(BF16)
    ga_ref[0] = _sigmoid(proj(4)).astype(BF16)
    gb_ref[0] = _sigmoid(proj(5)).astype(BF16)


def _gates(x, shift, scale, norm_w, w_g):
    b, s, d = x.shape
    tok = pl.BlockSpec((1, TM, d), lambda bi, i: (bi, i, 0))
    mod = pl.BlockSpec((1, 1, d), lambda bi, i: (bi, 0, 0))
    out = jax.ShapeDtypeStruct((b, s, d), BF16)
    return pl.pallas_call(
        _gates_kernel,
        grid=(b, s // TM),
        in_specs=[tok, mod, mod, _const_spec((1, d)), _const_spec(w_g.shape)],
        out_specs=[tok] * 5,
        out_shape=[out] * 5,
        compiler_params=pltpu.CompilerParams(
            dimension_semantics=("arbitrary", "arbitrary"), vmem_limit_bytes=VMEM_LIMIT),
        name="gates",
    )(x, shift, scale, norm_w, w_g)


def _mla_kernel(x_ref, shift_ref, scale_ref, nw_ref, pos_ref, invf_ref, w_ref,
                qn_ref, kvn_ref, wuq_ref, wuk_ref, wvt_ref,
                q_ref, k_ref, vt_ref):
    hb = _modulated_norm(x_ref, shift_ref, scale_ref, nw_ref)
    lat = jnp.dot(hb, w_ref[...], preferred_element_type=F32)
    cq = lat[:, :Q_LORA_RANK]
    ckv = lat[:, Q_LORA_RANK:Q_LORA_RANK + KV_LORA_RANK]
    kp = lat[:, Q_LORA_RANK + KV_LORA_RANK:]

    ang = pos_ref[0] * invf_ref[...]
    lane = lax.broadcasted_iota(jnp.int32, ang.shape, 1)
    first_half = lane < QK_ROPE_DIM
    table = jnp.where(first_half, jnp.cos(ang), jnp.sin(ang))

    def rope(t):
        return jnp.where(first_half, t + pltpu.roll(t, QK_ROPE_DIM, 1), 0.0)

    cqn = _rms(cq, qn_ref[...]).astype(BF16)
    ckvn = _rms(ckv, kvn_ref[...]).astype(BF16)

    qa = jnp.dot(cqn, wuq_ref[...], preferred_element_type=F32)
    kn = jnp.dot(ckvn, wuk_ref[...], preferred_element_type=F32)
    vt = lax.dot_general(wvt_ref[...], ckvn, (((1,), (1,)), ((), ())),
                         preferred_element_type=F32)

    k_hi = rope(kp * table).astype(BF16)
    q_table = table * Q_SCALE
    for h in range(N_HEADS):
        blk = qa[:, h * QK_PAD:(h + 1) * QK_PAD]
        q_ref[0, h, :, :LANES] = (blk[:, :LANES] * Q_SCALE).astype(BF16)
        q_ref[0, h, :, LANES:] = rope(blk[:, LANES:] * q_table).astype(BF16)
        k_ref[0, h, :, :LANES] = kn[:, h * LANES:(h + 1) * LANES].astype(BF16)
        k_ref[0, h, :, LANES:] = k_hi
    ones = jnp.ones((V_PAD - V_DIM, TK), BF16)
    for j in range(TM // TK):
        for h in range(N_HEADS):
            vt_ref[0, j, h, :V_DIM, :] = vt[h * V_DIM:(h + 1) * V_DIM, j * TK:(j + 1) * TK].astype(BF16)
            vt_ref[0, j, h, V_DIM:, :] = ones


def _mla_proj(x, shift, scale, norm_w, pos, invf, w_m, qn_w, kvn_w, w_uq2, w_uk, w_vt):
    b, s, d = x.shape
    tok = pl.BlockSpec((1, TM, d), lambda bi, i: (bi, i, 0))
    mod = pl.BlockSpec((1, 1, d), lambda bi, i: (bi, 0, 0))
    qk_spec = pl.BlockSpec((1, N_HEADS, TM, QK_PAD), lambda bi, i: (bi, 0, i, 0))
    return pl.pallas_call(
        _mla_kernel,
        grid=(b, s // TM),
        in_specs=[
            tok, mod, mod, _const_spec((1, d)),
            pl.BlockSpec((1, TM, 1), lambda bi, i: (bi, i, 0)),
            _const_spec(invf.shape), _const_spec(w_m.shape),
            _const_spec(qn_w.shape), _const_spec(kvn_w.shape),
            _const_spec(w_uq2.shape), _const_spec(w_uk.shape), _const_spec(w_vt.shape),
        ],
        out_specs=[
            qk_spec, qk_spec,
            pl.BlockSpec((1, TM // TK, N_HEADS, V_PAD, TK), lambda bi, i: (bi, i, 0, 0, 0)),
        ],
        out_shape=[
            jax.ShapeDtypeStruct((b, N_HEADS, s, QK_PAD), BF16),
            jax.ShapeDtypeStruct((b, N_HEADS, s, QK_PAD), BF16),
            jax.ShapeDtypeStruct((b, s // TK, N_HEADS, V_PAD, TK), BF16),
        ],
        compiler_params=pltpu.CompilerParams(
            dimension_semantics=("arbitrary", "arbitrary"), vmem_limit_bytes=VMEM_LIMIT),
        name="mla_proj",
    )(x, shift, scale, norm_w, pos, invf, w_m, qn_w, kvn_w, w_uq2, w_uk, w_vt)


def _conv_kernel(u_ref, uh_ref, sa_ref, cw_ref, cb_ref, lw_ref, lb_ref, o_ref, sh_ref, cv_ref):
    i = pl.program_id(1)
    n_lc = CONV_WIDTH // CONV_LANES
    first = HALO - (CONV_KERNEL - 1)
    for lc in range(n_lc):
        lanes = slice(lc * CONV_LANES, (lc + 1) * CONV_LANES)
        halo = uh_ref[0, :, lanes].astype(F32)
        sh_ref[lc, 0, 0:HALO, :] = jnp.where(i > 0, halo, 0.0)
        sh_ref[lc, 0, HALO:, :] = u_ref[0, :, lanes].astype(F32)
        for r in range(1, SUBLANES):
            for rb in range(0, SHIFT_ROWS, SHIFT_BLOCK):
                sh_ref[lc, r, rb:rb + SHIFT_BLOCK, :] = sh_ref[lc, 0, rb + r:rb + r + SHIFT_BLOCK, :]

    groups = RC // SUBLANES

    def row_chunk(c, carry):
        row0 = pl.multiple_of(c * RC, RC)

        def lane_chunk(lc, carry2):
            acc = jnp.zeros((groups, SUBLANES, CONV_LANES), F32)
            for r in range(SUBLANES):
                offs = [o for o in range(first, first + CONV_KERNEL) if o % SUBLANES == r]
                g0, g1 = offs[0] // SUBLANES, offs[-1] // SUBLANES
                span = groups + g1 - g0
                start = pl.multiple_of(row0 + g0 * SUBLANES, SUBLANES)
                xs = sh_ref[lc, r, pl.ds(start, span * SUBLANES), :]
                xs = xs.reshape(span, SUBLANES, CONV_LANES)
                for o in offs:
                    g = o // SUBLANES - g0
                    acc = acc + cw_ref[lc, o - first][None] * xs[g:g + groups]
            cv_ref[lc] = acc.reshape(RC, CONV_LANES)
            return carry2

        lax.fori_loop(0, n_lc, lane_chunk, 0)
        acc = jnp.concatenate([cv_ref[lc] for lc in range(n_lc)], axis=1) + cb_ref[...]
        mu = jnp.mean(acc, axis=-1, keepdims=True)
        xc = acc - mu
        var = jnp.mean(xc * xc, axis=-1, keepdims=True)
        y = xc * lax.rsqrt(var + EPS) * lw_ref[...] + lb_ref[...]
        gate = sa_ref[0, pl.ds(row0, RC), :].astype(F32)
        o_ref[0, pl.ds(row0, RC), :] = (_silu(y) * gate).astype(BF16)
        return carry

    lax.fori_loop(0, TC // RC, row_chunk, 0)


def _conv(u, sa, conv_w, conv_b, ln_w, ln_b):
    b, s, c = u.shape
    n_lc = c // CONV_LANES
    tok = pl.BlockSpec((1, TC, c), lambda bi, i: (bi, i, 0))
    halo = pl.BlockSpec((1, HALO, c), lambda bi, i: (bi, jnp.maximum(i * (TC // HALO) - 1, 0), 0))
    vec = _const_spec((1, c))
    return pl.pallas_call(
        _conv_kernel,
        grid=(b, s // TC),
        in_specs=[tok, halo, tok, _const_spec(conv_w.shape), vec, vec, vec],
        out_specs=tok,
        out_shape=jax.ShapeDtypeStruct((b, s, c), BF16),
        scratch_shapes=[pltpu.VMEM((n_lc, SUBLANES, HALO + TC, CONV_LANES), F32),
                        pltpu.VMEM((n_lc, RC, CONV_LANES), F32)],
        compiler_params=pltpu.CompilerParams(
            dimension_semantics=("arbitrary", "arbitrary"), vmem_limit_bytes=VMEM_LIMIT),
        name="conv",
    )(u, u, sa, conv_w, conv_b, ln_w, ln_b)


ATTN_BUFFERS = 3
ATTN_AHEAD = 2
ATTN_GROUP = 3


def _attn_scores(q_ref, k_ref, bias_ref, item, s_out, bm_out, diagonal):
    qn, jn = item
    k = k_ref[0, 0, pl.ds(pl.multiple_of(jn * TK, TK), TK), :]
    q = q_ref[0, 0, pl.ds(pl.multiple_of(qn * TQ, TQ), TQ), :]
    s = lax.dot_general(k, q, (((1,), (1,)), ((), ())), preferred_element_type=F32)
    if diagonal:
        s = s + bias_ref[...]
    s_out[...] = s
    bm_out[...] = jnp.max(s, axis=0, keepdims=True)


def _attn_accumulate(vt_ref, o_ref, acc_ref, item, p_in, al_in, diagonal):
    qp, jp = item
    pv = jnp.dot(vt_ref[0, jp, 0], p_in[...], preferred_element_type=F32)
    acc = al_in[...] * acc_ref[qp] + pv
    acc_ref[qp] = acc
    if diagonal:
        rows = pl.ds(pl.multiple_of(qp * TQ, TQ), TQ)
        out = acc[:V_DIM] / acc[V_DIM:V_DIM + 1]
        o_ref[0, rows, :] = jnp.transpose(out).astype(BF16)


def _attn_step(q_ref, k_ref, vt_ref, bias_ref, o_ref, acc_ref, m_ref, cur, nxt, prv, items, diag):
    s_cur, p_cur, bm_cur, al_cur = cur
    s_nxt, _, bm_nxt, _ = nxt
    _, p_prv, _, al_prv = prv
    item_next, (qc, jc), item_prev = items
    diag_next, diag_prev = diag
    _attn_accumulate(vt_ref, o_ref, acc_ref, item_prev, p_prv, al_prv, diag_prev)
    m_old = jnp.where(jc == 0, -jnp.inf, m_ref[qc])
    m_new = jnp.maximum(m_old, bm_cur[...])
    p_cur[...] = jnp.exp2(s_cur[...] - m_new).astype(BF16)
    al_cur[...] = jnp.exp2(m_old - m_new)
    m_ref[qc] = m_new
    _attn_scores(q_ref, k_ref, bias_ref, item_next, s_nxt, bm_nxt, diag_next)


def _attn_item(t, n_lower):
    def lower(t):
        q = jnp.int32(1)
        for c in range(2, 64):
            if c * (c - 1) // 2 >= n_lower:
                break
            q = q + (t >= c * (c - 1) // 2).astype(jnp.int32)
        return q, t - lax.shift_right_logical(q * (q - 1), 1)
    ql, jl = lower(jnp.minimum(t, n_lower - 1))
    d = t - n_lower
    is_diag = t >= n_lower
    return jnp.where(is_diag, d, ql), jnp.where(is_diag, d, jl)


def _attn_kernel(q_ref, k_ref, vt_ref, bias_ref, o_ref, acc_ref, m_ref, *bufs):
    n_q = q_ref.shape[2] // TQ
    n_lower = n_q * (n_q - 1) // 2
    n_items = n_lower + n_q
    nbuf = ATTN_BUFFERS
    assert TQ == TK and ATTN_GROUP % nbuf == 0 and n_items % nbuf == 0
    ring = tuple(bufs[4 * r:4 * r + 4] for r in range(nbuf))

    def item(t):
        return _attn_item(jnp.clip(t, 0, n_items - 1), n_lower)

    def steps(t0, count, t0_static=None):
        for u in range(count):
            t = t0 + u
            diag = (False, False) if t0_static is None else (
                t0_static + u + ATTN_AHEAD >= n_lower, t0_static + u - 1 >= n_lower)
            _attn_step(q_ref, k_ref, vt_ref, bias_ref, o_ref, acc_ref, m_ref,
                       ring[u % nbuf], ring[(u + ATTN_AHEAD) % nbuf], ring[(u - 1) % nbuf],
                       (item(t + ATTN_AHEAD), item(t), item(t - 1)), diag)

    acc_ref[...] = jnp.zeros_like(acc_ref)
    m_ref[...] = jnp.zeros_like(m_ref)
    _, p_last, _, al_last = ring[nbuf - 1]
    p_last[...] = jnp.zeros_like(p_last)
    al_last[...] = jnp.ones_like(al_last)
    for a in range(ATTN_AHEAD):
        s_buf, _, bm_buf, _ = ring[a]
        _attn_scores(q_ref, k_ref, bias_ref, item(jnp.int32(a)), s_buf, bm_buf, False)

    n_plain = (n_lower - ATTN_AHEAD) // ATTN_GROUP
    lax.fori_loop(0, n_plain, lambda g, c: (steps(g * ATTN_GROUP, ATTN_GROUP), c)[1], 0)
    t_tail = n_plain * ATTN_GROUP
    steps(jnp.int32(t_tail), n_items - t_tail, t_tail)
    _, p_last, _, al_last = ring[(n_items - 1) % nbuf]
    _attn_accumulate(vt_ref, o_ref, acc_ref, item(jnp.int32(n_items - 1)), p_last, al_last, True)


def _attention(q, k, vt):
    b, h, s, _ = q.shape
    n_q = s // TQ
    row = lax.broadcasted_iota(jnp.int32, (TK, TQ), 0)
    col = lax.broadcasted_iota(jnp.int32, (TK, TQ), 1)
    bias = jnp.where(row <= col, 0.0, -jnp.inf).astype(F32)
    stat = pltpu.VMEM((1, TQ), F32)
    parity = [pltpu.VMEM((TK, TQ), F32), pltpu.VMEM((TK, TQ), BF16), stat, stat]
    return pl.pallas_call(
        _attn_kernel,
        grid=(b, h),
        in_specs=[
            pl.BlockSpec((1, 1, s, QK_PAD), lambda bi, hi: (bi, hi, 0, 0)),
            pl.BlockSpec((1, 1, s, QK_PAD), lambda bi, hi: (bi, hi, 0, 0)),
            pl.BlockSpec((1, s // TK, 1, V_PAD, TK), lambda bi, hi: (bi, 0, hi, 0, 0)),
            _const_spec(bias.shape),
        ],
        out_specs=pl.BlockSpec((1, s, V_DIM), lambda bi, hi: (bi, 0, hi)),
        out_shape=jax.ShapeDtypeStruct((b, s, h * V_DIM), BF16),
        scratch_shapes=[pltpu.VMEM((n_q, V_PAD, TQ), F32), pltpu.VMEM((n_q, 1, TQ), F32)]
        + parity * ATTN_BUFFERS,
        compiler_params=pltpu.CompilerParams(
            dimension_semantics=("arbitrary", "arbitrary"),
            vmem_limit_bytes=VMEM_LIMIT),
        name="attn",
    )(q, k, vt, bias)


def _out_kernel(ug_ref, o_ref, sb_ref, ga_ref, gb_ref, x_ref, gate_ref,
                wco_ref, wao_ref, wo_ref, fw_ref, y_ref):
    ya = jnp.dot(ug_ref[0], wco_ref[...], preferred_element_type=F32)
    og = (o_ref[0].astype(F32) * sb_ref[0].astype(F32)).astype(BF16)
    yb = jnp.dot(og, wao_ref[...], preferred_element_type=F32)
    merged = ga_ref[0].astype(F32) * ya + gb_ref[0].astype(F32) * yb
    delta = jnp.dot(merged.astype(BF16), wo_ref[...], preferred_element_type=F32)
    y = x_ref[0] + gate_ref[0] * delta
    y_ref[0] = _rms(y, fw_ref[...])


def _out(ug, o, sb, ga, gb, x, gate, w_co, w_ao, w_o, fw):
    b, s, d = x.shape
    tok = pl.BlockSpec((1, TM, d), lambda bi, i: (bi, i, 0))
    mod = pl.BlockSpec((1, 1, d), lambda bi, i: (bi, 0, 0))
    wspec = _const_spec((d, d))
    return pl.pallas_call(
        _out_kernel,
        grid=(b, s // TM),
        in_specs=[tok, tok, tok, tok, tok, tok, mod, wspec, wspec, wspec, _const_spec((1, d))],
        out_specs=tok,
        out_shape=jax.ShapeDtypeStruct((b, s, d), F32),
        compiler_params=pltpu.CompilerParams(
            dimension_semantics=("arbitrary", "arbitrary"), vmem_limit_bytes=VMEM_LIMIT),
        name="out",
    )(ug, o, sb, ga, gb, x, gate, w_co, w_ao, w_o, fw)


def _rot_cols(w):
    half = w.shape[-1] // 2
    return jnp.concatenate([-w[..., half:], w[..., :half]], axis=-1)


def _layer(x, c_act_pad, pos, invf, w_ada, b_ada, norm_w, w_in, conv_w, conv_b, conv_ln_w,
           conv_ln_b, w_conv_out, q_norm_w, w_uq, kv_norm_w, w_ukv, w_attn_out, w_out, final_w):
    b, s, d = x.shape
    mod = _ada(c_act_pad, w_ada, b_ada[None, :])[:b]
    shift, scale, gate = (m[:, None, :] for m in jnp.split(mod, 3, axis=-1))

    o_cq = 3 * CONV_WIDTH
    o_kpe = o_cq + Q_LORA_RANK + KV_LORA_RANK
    o_bg = o_kpe + QK_ROPE_DIM
    w_kpe = w_in[:, o_kpe:o_bg]
    w_g = jnp.concatenate([w_in[:, :o_cq], w_in[:, o_bg:]], axis=1).astype(BF16)
    w_m = jnp.concatenate([w_in[:, o_cq:o_kpe], w_kpe, _rot_cols(w_kpe)], axis=1).astype(BF16)

    wq = w_uq.reshape(Q_LORA_RANK, N_HEADS, QK_NOPE_DIM + QK_ROPE_DIM)
    wq_rope = wq[..., QK_NOPE_DIM:]
    w_uq2 = jnp.concatenate([wq, _rot_cols(wq_rope)], axis=-1)
    w_uq2 = w_uq2.reshape(Q_LORA_RANK, N_HEADS * QK_PAD).astype(BF16)
    wkv = w_ukv.reshape(KV_LORA_RANK, N_HEADS, QK_NOPE_DIM + V_DIM)
    w_uk = wkv[..., :QK_NOPE_DIM].reshape(KV_LORA_RANK, N_HEADS * QK_NOPE_DIM).astype(BF16)
    w_vt = wkv[..., QK_NOPE_DIM:].reshape(KV_LORA_RANK, ATTN_WIDTH).T.astype(BF16)

    nw = norm_w[None, :]
    u, sa, sb, ga, gb = _gates(x, shift, scale, nw, w_g)
    q, k, vt = _mla_proj(x, shift, scale, nw, pos, invf, w_m, q_norm_w[None, :],
                         kv_norm_w[None, :], w_uq2, w_uk, w_vt)
    cw = conv_w.reshape(CONV_KERNEL, CONV_WIDTH // CONV_LANES, 1, CONV_LANES).transpose(1, 0, 2, 3)
    cw = jnp.broadcast_to(cw, (CONV_WIDTH // CONV_LANES, CONV_KERNEL, SUBLANES, CONV_LANES))
    ug = _conv(u, sa, cw, conv_b[None, :], conv_ln_w[None, :], conv_ln_b[None, :])
    o = _attention(q, k, vt)
    return _out(ug, o, sb, ga, gb, x, gate, w_conv_out.astype(BF16), w_attn_out.astype(BF16),
                w_out.astype(BF16), final_w[None, :])


def kernel(x, c, positions, w_ada, b_ada, norm_w, w_in, conv_w, conv_b, conv_ln_w, conv_ln_b,
           w_conv_out, q_norm_w, w_uq, kv_norm_w, w_ukv, w_attn_out, w_out, final_norm_w):
    b, s, d = x.shape
    depth = w_ada.shape[0]
    assert depth == 1, "final rmsnorm is fused into the single layer's output kernel"
    inv_freq = ROPE_THETA ** (-jnp.arange(0, QK_ROPE_DIM, 2, dtype=F32) / QK_ROPE_DIM)
    invf = jnp.tile(inv_freq, LANES // inv_freq.shape[0])[None, :]
    pos = positions.astype(F32)[..., None]
    c_pad = jnp.pad(c, ((0, 8 - b), (0, 0)))
    return _layer(x, c_pad, pos, invf, w_ada[0], b_ada[0], norm_w[0], w_in[0], conv_w[0],
                  conv_b[0], conv_ln_w[0], conv_ln_b[0], w_conv_out[0], q_norm_w[0], w_uq[0],
                  kv_norm_w[0], w_ukv[0], w_attn_out[0], w_out[0], final_norm_w)
```

```python
import functools
import math

import jax
import jax.numpy as jnp
from jax import lax
from jax.experimental import pallas as pl
from jax.experimental.pallas import tpu as pltpu

F32 = jnp.float32
BF16 = jnp.bfloat16

D_MODEL = 1024
CONV_WIDTH = 1024
CONV_KERNEL = 31
N_HEADS = 8
QK_NOPE_DIM = 128
QK_ROPE_DIM = 64
V_DIM = 128
Q_LORA_RANK = 256
KV_LORA_RANK = 256
ATTN_WIDTH = N_HEADS * V_DIM
ROPE_THETA = 10000.0
EPS = 1e-6

LANES = 128
QK_PAD = 2 * LANES
BF16_SUBLANES = 16
V_PAD = V_DIM + BF16_SUBLANES
HALO = 32
VMEM_LIMIT = 56 * 1024 * 1024

Q_SCALE = (QK_NOPE_DIM + QK_ROPE_DIM) ** -0.5 * math.log2(math.e)

W_COL_VAL = 0
W_COL_GLU = W_COL_VAL + CONV_WIDTH
W_COL_AGATE = W_COL_GLU + CONV_WIDTH
W_COL_LATENT = W_COL_AGATE + CONV_WIDTH
W_LATENT_COLS = Q_LORA_RANK + KV_LORA_RANK + 2 * QK_ROPE_DIM
W_LATENT_BLOCK = 768
W_COL_BGATE = W_COL_LATENT + W_LATENT_COLS
W_COL_GA = W_COL_BGATE + ATTN_WIDTH
W_COL_GB = W_COL_GA + D_MODEL
W_COLS = W_COL_GB + D_MODEL

TM = 512
TC = 256
RC = 64
CONV_LANES = 256
SUBLANES = 8
SHIFT_ROWS = TC + HALO - SUBLANES
SHIFT_BLOCK = 40
assert SHIFT_ROWS % SHIFT_BLOCK == 0 and SHIFT_BLOCK % SUBLANES == 0
TQ = 512
TK = 512


def _sigmoid(x):
    return 1.0 / (1.0 + jnp.exp(-x))


def _silu(x):
    return x * _sigmoid(x)


def _rms(x, w):
    return x * lax.rsqrt(jnp.mean(x * x, axis=-1, keepdims=True) + EPS) * w


def _const_spec(shape):
    return pl.BlockSpec(shape, lambda *_: (0,) * len(shape), pipeline_mode=pl.Buffered(1))


def _ada_kernel(c_ref, w_ref, b_ref, o_ref):
    c = c_ref[...]
    o_ref[...] = jnp.dot(_silu(c), w_ref[...], preferred_element_type=F32) + b_ref[...]


def _ada(c_pad, w_ada, b_ada):
    rows, d = c_pad.shape
    n = w_ada.shape[1]
    tn = 512
    return pl.pallas_call(
        _ada_kernel,
        grid=(n // tn,),
        in_specs=[
            pl.BlockSpec((rows, d), lambda j: (0, 0)),
            pl.BlockSpec((d, tn), lambda j: (0, j)),
            pl.BlockSpec((1, tn), lambda j: (0, j)),
        ],
        out_specs=pl.BlockSpec((rows, tn), lambda j: (0, j)),
        out_shape=jax.ShapeDtypeStruct((rows, n), F32),
        compiler_params=pltpu.CompilerParams(dimension_semantics=("arbitrary",)),
        name="ada",
    )(c_pad, w_ada, b_ada)


def _modulated_norm(x_ref, shift_ref, scale_ref, nw_ref):
    x = x_ref[0]
    h = _rms(x, nw_ref[...]) * (1.0 + scale_ref[0]) + shift_ref[0]
    return h.astype(BF16)


def _gateconv_kernel(x_ref, shift_ref, scale_ref, nw_ref, w_ref, cw_ref, cb_ref,
                     cv_ref, sa_ref, sb_ref, ga_ref, gb_ref, sh_ref, hb_ref):
    i = pl.program_id(1)
    n_lc = CONV_WIDTH // CONV_LANES
    first = HALO - (CONV_KERNEL - 1)
    groups = RC // SUBLANES

    @pl.when(i == 0)
    def _():
        for lc in range(n_lc):
            sh_ref[lc, 0, 0:HALO, :] = jnp.zeros((HALO, CONV_LANES), F32)

    @pl.when(i > 0)
    def _():
        for lc in range(n_lc):
            sh_ref[lc, 0, 0:HALO, :] = sh_ref[lc, 0, TC:TC + HALO, :]

    hb_ref[...] = _modulated_norm(x_ref, shift_ref, scale_ref, nw_ref)

    def proj(col):
        return jnp.dot(hb_ref[...], w_ref[:, col:col + D_MODEL], preferred_element_type=F32)

    u = proj(W_COL_VAL) * _sigmoid(proj(W_COL_GLU))
    for lc in range(n_lc):
        sh_ref[lc, 0, HALO:, :] = u[:, lc * CONV_LANES:(lc + 1) * CONV_LANES]

    always = i >= 0

    @pl.when(always)
    def _():
        for lc in range(n_lc):
            for r in range(1, SUBLANES):
                for rb in range(0, SHIFT_ROWS, SHIFT_BLOCK):
                    sh_ref[lc, r, rb:rb + SHIFT_BLOCK, :] = (
                        sh_ref[lc, 0, rb + r:rb + r + SHIFT_BLOCK, :])

    gate_groups = ((W_COL_AGATE, sa_ref, _silu), (W_COL_BGATE, sb_ref, _silu),
                   (W_COL_GA, ga_ref, _sigmoid), (W_COL_GB, gb_ref, _sigmoid))
    assert TC // RC == len(gate_groups)

    def gate_and_conv_chunk(c, col, out_ref, act):
        out_ref[0] = act(proj(col)).astype(BF16)
        row0 = c * RC
        for lc in range(n_lc):
            lanes = slice(lc * CONV_LANES, (lc + 1) * CONV_LANES)
            acc = jnp.zeros((groups, SUBLANES, CONV_LANES), F32)
            for r in range(SUBLANES):
                offs = [o for o in range(first, first + CONV_KERNEL) if o % SUBLANES == r]
                g0, g1 = offs[0] // SUBLANES, offs[-1] // SUBLANES
                span = groups + g1 - g0
                start = row0 + g0 * SUBLANES
                xs = sh_ref[lc, r, start:start + span * SUBLANES, :]
                xs = xs.reshape(span, SUBLANES, CONV_LANES)
                for o in offs:
                    g = o // SUBLANES - g0
                    acc = acc + cw_ref[lc, o - first][None] * xs[g:g + groups]
            cv = acc.reshape(RC, CONV_LANES) + cb_ref[:, lanes]
            cv_ref[0, row0:row0 + RC, lanes] = cv.astype(BF16)

    for c, (col, out_ref, act) in enumerate(gate_groups):
        pl.when(always)(functools.partial(gate_and_conv_chunk, c, col, out_ref, act))


def _gateconv(x, shift, scale, norm_w, w_full, conv_w, conv_b):
    b, s, d = x.shape
    n_lc = d // CONV_LANES
    tok = pl.BlockSpec((1, TC, d), lambda bi, i: (bi, i, 0))
    mod = pl.BlockSpec((1, 1, d), lambda bi, i: (bi, 0, 0))
    out = jax.ShapeDtypeStruct((b, s, d), BF16)
    return pl.pallas_call(
        _gateconv_kernel,
        grid=(b, s // TC),
        in_specs=[tok, mod, mod, _const_spec((1, d)), _const_spec(w_full.shape),
                  _const_spec(conv_w.shape), _const_spec((1, d))],
        out_specs=[tok] * 5,
        out_shape=[out] * 5,
        scratch_shapes=[pltpu.VMEM((n_lc, SUBLANES, HALO + TC, CONV_LANES), F32),
                        pltpu.VMEM((TC, d), BF16)],
        compiler_params=pltpu.CompilerParams(
            dimension_semantics=("arbitrary", "arbitrary"), vmem_limit_bytes=VMEM_LIMIT),
        name="gateconv",
    )(x, shift, scale, norm_w, w_full, conv_w, conv_b)


def _mla_kernel(x_ref, shift_ref, scale_ref, nw_ref, pos_ref, invf_ref, w_ref,
                qn_ref, kvn_ref, wuq_ref, wuk_ref, wvt_ref,
                q_ref, k_ref, vt_ref):
    hb = _modulated_norm(x_ref, shift_ref, scale_ref, nw_ref)
    lat = jnp.dot(hb, w_ref[:, :W_LATENT_COLS], preferred_element_type=F32)
    cq = lat[:, :Q_LORA_RANK]
    ckv = lat[:, Q_LORA_RANK:Q_LORA_RANK + KV_LORA_RANK]
    kp = lat[:, Q_LORA_RANK + KV_LORA_RANK:]

    ang = pos_ref[0] * invf_ref[...]
    lane = lax.broadcasted_iota(jnp.int32, ang.shape, 1)
    first_half = lane < QK_ROPE_DIM
    table = jnp.where(first_half, jnp.cos(ang), jnp.sin(ang))

    def rope(t):
        return jnp.where(first_half, t + pltpu.roll(t, QK_ROPE_DIM, 1), 0.0)

    cqn = _rms(cq, qn_ref[...]).astype(BF16)
    ckvn = _rms(ckv, kvn_ref[...]).astype(BF16)

    qa = jnp.dot(cqn, wuq_ref[...], preferred_element_type=F32)
    kn = jnp.dot(ckvn, wuk_ref[...], preferred_element_type=F32)
    vt = lax.dot_general(wvt_ref[...], ckvn, (((1,), (1,)), ((), ())),
                         preferred_element_type=F32)

    k_hi = rope(kp * table).astype(BF16)
    q_table = table * Q_SCALE
    for h in range(N_HEADS):
        blk = qa[:, h * QK_PAD:(h + 1) * QK_PAD]
        q_ref[0, h, :, :LANES] = (blk[:, :LANES] * Q_SCALE).astype(BF16)
        q_ref[0, h, :, LANES:] = rope(blk[:, LANES:] * q_table).astype(BF16)
        k_ref[0, h, :, :LANES] = kn[:, h * LANES:(h + 1) * LANES].astype(BF16)
        k_ref[0, h, :, LANES:] = k_hi
    ones = jnp.ones((V_PAD - V_DIM, TK), BF16)
    for j in range(TM // TK):
        for h in range(N_HEADS):
            vt_ref[0, j, h, :V_DIM, :] = vt[h * V_DIM:(h + 1) * V_DIM, j * TK:(j + 1) * TK].astype(BF16)
            vt_ref[0, j, h, V_DIM:, :] = ones


def _mla_proj(x, shift, scale, norm_w, pos, invf, w_full, qn_w, kvn_w, w_uq2, w_uk, w_vt):
    b, s, d = x.shape
    tok = pl.BlockSpec((1, TM, d), lambda bi, i: (bi, i, 0))
    mod = pl.BlockSpec((1, 1, d), lambda bi, i: (bi, 0, 0))
    qk_spec = pl.BlockSpec((1, N_HEADS, TM, QK_PAD), lambda bi, i: (bi, 0, i, 0))
    assert W_COL_LATENT % W_LATENT_BLOCK == 0 and W_LATENT_BLOCK >= W_LATENT_COLS
    w_lat = pl.BlockSpec((d, W_LATENT_BLOCK), lambda bi, i: (0, W_COL_LATENT // W_LATENT_BLOCK),
                         pipeline_mode=pl.Buffered(1))
    return pl.pallas_call(
        _mla_kernel,
        grid=(b, s // TM),
        in_specs=[
            tok, mod, mod, _const_spec((1, d)),
            pl.BlockSpec((1, TM, 1), lambda bi, i: (bi, i, 0)),
            _const_spec(invf.shape), w_lat,
            _const_spec(qn_w.shape), _const_spec(kvn_w.shape),
            _const_spec(w_uq2.shape), _const_spec(w_uk.shape), _const_spec(w_vt.shape),
        ],
        out_specs=[
            qk_spec, qk_spec,
            pl.BlockSpec((1, TM // TK, N_HEADS, V_PAD, TK), lambda bi, i: (bi, i, 0, 0, 0)),
        ],
        out_shape=[
            jax.ShapeDtypeStruct((b, N_HEADS, s, QK_PAD), BF16),
            jax.ShapeDtypeStruct((b, N_HEADS, s, QK_PAD), BF16),
            jax.ShapeDtypeStruct((b, s // TK, N_HEADS, V_PAD, TK), BF16),
        ],
        compiler_params=pltpu.CompilerParams(
            dimension_semantics=("arbitrary", "arbitrary"), vmem_limit_bytes=VMEM_LIMIT),
        name="mla_proj",
    )(x, shift, scale, norm_w, pos, invf, w_full, qn_w, kvn_w, w_uq2, w_uk, w_vt)


ATTN_BUFFERS = 3
ATTN_AHEAD = 2
ATTN_GROUP = 3


def _attn_scores(q_ref, k_ref, bias_ref, item, s_out, bm_out, diagonal):
    qn, jn = item
    k = k_ref[0, 0, pl.ds(pl.multiple_of(jn * TK, TK), TK), :]
    q = q_ref[0, 0, pl.ds(pl.multiple_of(qn * TQ, TQ), TQ), :]
    s = lax.dot_general(k, q, (((1,), (1,)), ((), ())), preferred_element_type=F32)
    if diagonal:
        s = s + bias_ref[...]
    s_out[...] = s
    bm_out[...] = jnp.max(s, axis=0, keepdims=True)


def _attn_accumulate(vt_ref, o_ref, acc_ref, item, p_in, al_in, diagonal):
    qp, jp = item
    pv = jnp.dot(vt_ref[0, jp, 0], p_in[...], preferred_element_type=F32)
    acc = al_in[...] * acc_ref[qp] + pv
    acc_ref[qp] = acc
    if diagonal:
        rows = pl.ds(pl.multiple_of(qp * TQ, TQ), TQ)
        out = acc[:V_DIM] / acc[V_DIM:V_DIM + 1]
        o_ref[0, rows, :] = jnp.transpose(out).astype(BF16)


def _attn_step(q_ref, k_ref, vt_ref, bias_ref, o_ref, acc_ref, m_ref, cur, nxt, prv, items, diag):
    s_cur, p_cur, bm_cur, al_cur = cur
    s_nxt, _, bm_nxt, _ = nxt
    _, p_prv, _, al_prv = prv
    item_next, (qc, jc), item_prev = items
    diag_next, diag_prev = diag
    _attn_accumulate(vt_ref, o_ref, acc_ref, item_prev, p_prv, al_prv, diag_prev)
    m_old = jnp.where(jc == 0, -jnp.inf, m_ref[qc])
    m_new = jnp.maximum(m_old, bm_cur[...])
    p_cur[...] = jnp.exp2(s_cur[...] - m_new).astype(BF16)
    al_cur[...] = jnp.exp2(m_old - m_new)
    m_ref[qc] = m_new
    _attn_scores(q_ref, k_ref, bias_ref, item_next, s_nxt, bm_nxt, diag_next)


def _attn_item(t, n_lower):
    def lower(t):
        q = jnp.int32(1)
        for c in range(2, 64):
            if c * (c - 1) // 2 >= n_lower:
                break
            q = q + (t >= c * (c - 1) // 2).astype(jnp.int32)
        return q, t - lax.shift_right_logical(q * (q - 1), 1)
    ql, jl = lower(jnp.minimum(t, n_lower - 1))
    d = t - n_lower
    is_diag = t >= n_lower
    return jnp.where(is_diag, d, ql), jnp.where(is_diag, d, jl)


def _attn_kernel(q_ref, k_ref, vt_ref, bias_ref, o_ref, acc_ref, m_ref, *bufs):
    n_q = q_ref.shape[2] // TQ
    n_lower = n_q * (n_q - 1) // 2
    n_items = n_lower + n_q
    nbuf = ATTN_BUFFERS
    assert TQ == TK and ATTN_GROUP % nbuf == 0 and n_items % nbuf == 0
    ring = tuple(bufs[4 * r:4 * r + 4] for r in range(nbuf))

    def item(t):
        return _attn_item(jnp.clip(t, 0, n_items - 1), n_lower)

    def steps(t0, count, t0_static=None):
        for u in range(count):
            t = t0 + u
            diag = (False, False) if t0_static is None else (
                t0_static + u + ATTN_AHEAD >= n_lower, t0_static + u - 1 >= n_lower)
            _attn_step(q_ref, k_ref, vt_ref, bias_ref, o_ref, acc_ref, m_ref,
                       ring[u % nbuf], ring[(u + ATTN_AHEAD) % nbuf], ring[(u - 1) % nbuf],
                       (item(t + ATTN_AHEAD), item(t), item(t - 1)), diag)

    acc_ref[...] = jnp.zeros_like(acc_ref)
    m_ref[...] = jnp.zeros_like(m_ref)
    _, p_last, _, al_last = ring[nbuf - 1]
    p_last[...] = jnp.zeros_like(p_last)
    al_last[...] = jnp.ones_like(al_last)
    for a in range(ATTN_AHEAD):
        s_buf, _, bm_buf, _ = ring[a]
        _attn_scores(q_ref, k_ref, bias_ref, item(jnp.int32(a)), s_buf, bm_buf, False)

    n_plain = (n_lower - ATTN_AHEAD) // ATTN_GROUP
    lax.fori_loop(0, n_plain, lambda g, c: (steps(g * ATTN_GROUP, ATTN_GROUP), c)[1], 0)
    t_tail = n_plain * ATTN_GROUP
    steps(jnp.int32(t_tail), n_items - t_tail, t_tail)
    _, p_last, _, al_last = ring[(n_items - 1) % nbuf]
    _attn_accumulate(vt_ref, o_ref, acc_ref, item(jnp.int32(n_items - 1)), p_last, al_last, True)


def _attention(q, k, vt):
    b, h, s, _ = q.shape
    n_q = s // TQ
    row = lax.broadcasted_iota(jnp.int32, (TK, TQ), 0)
    col = lax.broadcasted_iota(jnp.int32, (TK, TQ), 1)
    bias = jnp.where(row <= col, 0.0, -jnp.inf).astype(F32)
    stat = pltpu.VMEM((1, TQ), F32)
    parity = [pltpu.VMEM((TK, TQ), F32), pltpu.VMEM((TK, TQ), BF16), stat, stat]
    return pl.pallas_call(
        _attn_kernel,
        grid=(b, h),
        in_specs=[
            pl.BlockSpec((1, 1, s, QK_PAD), lambda bi, hi: (bi, hi, 0, 0)),
            pl.BlockSpec((1, 1, s, QK_PAD), lambda bi, hi: (bi, hi, 0, 0)),
            pl.BlockSpec((1, s // TK, 1, V_PAD, TK), lambda bi, hi: (bi, 0, hi, 0, 0)),
            _const_spec(bias.shape),
        ],
        out_specs=pl.BlockSpec((1, s, V_DIM), lambda bi, hi: (bi, 0, hi)),
        out_shape=jax.ShapeDtypeStruct((b, s, h * V_DIM), BF16),
        scratch_shapes=[pltpu.VMEM((n_q, V_PAD, TQ), F32), pltpu.VMEM((n_q, 1, TQ), F32)]
        + parity * ATTN_BUFFERS,
        compiler_params=pltpu.CompilerParams(
            dimension_semantics=("arbitrary", "arbitrary"),
            vmem_limit_bytes=VMEM_LIMIT),
        name="attn",
    )(q, k, vt, bias)


def _out_kernel(cv_ref, sa_ref, o_ref, sb_ref, ga_ref, gb_ref, x_ref, gate_ref, lw_ref, lb_ref,
                wco_ref, wao_ref, wo_ref, fw_ref, y_ref):
    cv = cv_ref[0].astype(F32)
    mu = jnp.mean(cv, axis=-1, keepdims=True)
    xc = cv - mu
    var = jnp.mean(xc * xc, axis=-1, keepdims=True)
    ln = xc * lax.rsqrt(var + EPS) * lw_ref[...] + lb_ref[...]
    ug = (_silu(ln) * sa_ref[0].astype(F32)).astype(BF16)
    ya = jnp.dot(ug, wco_ref[...], preferred_element_type=F32)
    og = (o_ref[0].astype(F32) * sb_ref[0].astype(F32)).astype(BF16)
    yb = jnp.dot(og, wao_ref[...], preferred_element_type=F32)
    merged = ga_ref[0].astype(F32) * ya + gb_ref[0].astype(F32) * yb
    delta = jnp.dot(merged.astype(BF16), wo_ref[...], preferred_element_type=F32)
    y = x_ref[0] + gate_ref[0] * delta
    y_ref[0] = _rms(y, fw_ref[...])


def _out(cv, sa, o, sb, ga, gb, x, gate, ln_w, ln_b, w_co, w_ao, w_o, fw):
    b, s, d = x.shape
    tok = pl.BlockSpec((1, TM, d), lambda bi, i: (bi, i, 0))
    mod = pl.BlockSpec((1, 1, d), lambda bi, i: (bi, 0, 0))
    wspec = _const_spec((d, d))
    vec = _const_spec((1, d))
    return pl.pallas_call(
        _out_kernel,
        grid=(b, s // TM),
        in_specs=[tok, tok, tok, tok, tok, tok, tok, mod, vec, vec, wspec, wspec, wspec, vec],
        out_specs=tok,
        out_shape=jax.ShapeDtypeStruct((b, s, d), F32),
        compiler_params=pltpu.CompilerParams(
            dimension_semantics=("arbitrary", "arbitrary"), vmem_limit_bytes=VMEM_LIMIT),
        name="out",
    )(cv, sa, o, sb, ga, gb, x, gate, ln_w, ln_b, w_co, w_ao, w_o, fw)


def _rot_cols(w):
    half = w.shape[-1] // 2
    return jnp.concatenate([-w[..., half:], w[..., :half]], axis=-1)


def _layer(x, c_act_pad, pos, invf, w_ada, b_ada, norm_w, w_in, conv_w, conv_b, conv_ln_w,
           conv_ln_b, w_conv_out, q_norm_w, w_uq, kv_norm_w, w_ukv, w_attn_out, w_out, final_w):
    b, s, d = x.shape
    mod = _ada(c_act_pad, w_ada, b_ada[None, :])[:b]
    shift, scale, gate = (m[:, None, :] for m in jnp.split(mod, 3, axis=-1))

    o_kpe = W_COL_LATENT + Q_LORA_RANK + KV_LORA_RANK
    o_bg = o_kpe + QK_ROPE_DIM
    w_full = jnp.concatenate(
        [w_in[:, :o_bg], _rot_cols(w_in[:, o_kpe:o_bg]), w_in[:, o_bg:]], axis=1).astype(BF16)
    assert w_full.shape[1] == W_COLS

    wq = w_uq.reshape(Q_LORA_RANK, N_HEADS, QK_NOPE_DIM + QK_ROPE_DIM)
    wq_rope = wq[..., QK_NOPE_DIM:]
    w_uq2 = jnp.concatenate([wq, _rot_cols(wq_rope)], axis=-1)
    w_uq2 = w_uq2.reshape(Q_LORA_RANK, N_HEADS * QK_PAD).astype(BF16)
    wkv = w_ukv.reshape(KV_LORA_RANK, N_HEADS, QK_NOPE_DIM + V_DIM)
    w_uk = wkv[..., :QK_NOPE_DIM].reshape(KV_LORA_RANK, N_HEADS * QK_NOPE_DIM).astype(BF16)
    w_vt = wkv[..., QK_NOPE_DIM:].reshape(KV_LORA_RANK, ATTN_WIDTH).T.astype(BF16)

    nw = norm_w[None, :]
    cw = conv_w.reshape(CONV_KERNEL, CONV_WIDTH // CONV_LANES, 1, CONV_LANES).transpose(1, 0, 2, 3)
    cw = jnp.broadcast_to(cw, (CONV_WIDTH // CONV_LANES, CONV_KERNEL, SUBLANES, CONV_LANES))
    cv, sa, sb, ga, gb = _gateconv(x, shift, scale, nw, w_full, cw, conv_b[None, :])
    q, k, vt = _mla_proj(x, shift, scale, nw, pos, invf, w_full, q_norm_w[None, :],
                         kv_norm_w[None, :], w_uq2, w_uk, w_vt)
    o = _attention(q, k, vt)
    return _out(cv, sa, o, sb, ga, gb, x, gate, conv_ln_w[None, :], conv_ln_b[None, :],
                w_conv_out.astype(BF16), w_attn_out.astype(BF16), w_out.astype(BF16),
                final_w[None, :])


def kernel(x, c, positions, w_ada, b_ada, norm_w, w_in, conv_w, conv_b, conv_ln_w, conv_ln_b,
           w_conv_out, q_norm_w, w_uq, kv_norm_w, w_ukv, w_attn_out, w_out, final_norm_w):
    b, s, d = x.shape
    depth = w_ada.shape[0]
    assert depth == 1, "final rmsnorm is fused into the single layer's output kernel"
    inv_freq = ROPE_THETA ** (-jnp.arange(0, QK_ROPE_DIM, 2, dtype=F32) / QK_ROPE_DIM)
    invf = jnp.tile(inv_freq, LANES // inv_freq.shape[0])[None, :]
    pos = positions.astype(F32)[..., None]
    c_pad = jnp.pad(c, ((0, 8 - b), (0, 0)))
    return _layer(x, c_pad, pos, invf, w_ada[0], b_ada[0], norm_w[0], w_in[0], conv_w[0],
                  conv_b[0], conv_ln_w[0], conv_ln_b[0], w_conv_out[0], q_norm_w[0], w_uq[0],
                  kv_norm_w[0], w_ukv[0], w_attn_out[0], w_out[0], final_norm_w)
```

```python
import functools
import math

import jax
import jax.numpy as jnp
from jax import lax
from jax.experimental import pallas as pl
from jax.experimental.pallas import tpu as pltpu

F32 = jnp.float32
BF16 = jnp.bfloat16

D_MODEL = 1024
CONV_WIDTH = 1024
CONV_KERNEL = 31
N_HEADS = 8
QK_NOPE_DIM = 128
QK_ROPE_DIM = 64
V_DIM = 128
Q_LORA_RANK = 256
KV_LORA_RANK = 256
ATTN_WIDTH = N_HEADS * V_DIM
ROPE_THETA = 10000.0
EPS = 1e-6

LANES = 128
QK_PAD = 2 * LANES
BF16_SUBLANES = 16
V_PAD = V_DIM + BF16_SUBLANES
HALO = 32
VMEM_LIMIT = 56 * 1024 * 1024

Q_SCALE = (QK_NOPE_DIM + QK_ROPE_DIM) ** -0.5 * math.log2(math.e)

IN_COL_LATENT = 3 * CONV_WIDTH
IN_COL_BGATE = IN_COL_LATENT + Q_LORA_RANK + KV_LORA_RANK + QK_ROPE_DIM
IN_COLS = IN_COL_BGATE + ATTN_WIDTH + 2 * D_MODEL
G_VAL, G_GLU, G_AGATE, G_BGATE, G_GA, G_GB = range(6)
N_WIDE_GROUPS = 6
assert CONV_WIDTH == ATTN_WIDTH == D_MODEL

TM = 512
TC = 256
RC = 64
CONV_LANES = 128
PROJ_COLS = 512
SUBLANES = 8
SHIFT_ROWS = TC + HALO - SUBLANES
SHIFT_BLOCK = 40
assert SHIFT_ROWS % SHIFT_BLOCK == 0 and SHIFT_BLOCK % SUBLANES == 0
TQ = 512
TK = 512


def _sigmoid(x):
    return 1.0 / (1.0 + jnp.exp(-x))


def _silu(x):
    return x * _sigmoid(x)


def _rms(x, w):
    return x * lax.rsqrt(jnp.mean(x * x, axis=-1, keepdims=True) + EPS) * w


def _const_spec(shape):
    return pl.BlockSpec(shape, lambda *_: (0,) * len(shape), pipeline_mode=pl.Buffered(1))


def _ada_kernel(c_ref, w_ref, b_ref, o_ref):
    c = c_ref[...]
    o_ref[...] = jnp.dot(_silu(c), w_ref[...], preferred_element_type=F32) + b_ref[...]


def _ada(c_pad, w_ada, b_ada):
    rows, d = c_pad.shape
    n = w_ada.shape[1]
    tn = 512
    return pl.pallas_call(
        _ada_kernel,
        grid=(n // tn,),
        in_specs=[
            pl.BlockSpec((rows, d), lambda j: (0, 0)),
            pl.BlockSpec((d, tn), lambda j: (0, j)),
            pl.BlockSpec((1, tn), lambda j: (0, j)),
        ],
        out_specs=pl.BlockSpec((rows, tn), lambda j: (0, j)),
        out_shape=jax.ShapeDtypeStruct((rows, n), F32),
        compiler_params=pltpu.CompilerParams(dimension_semantics=("arbitrary",)),
        name="ada",
    )(c_pad, w_ada, b_ada)


def _modulated_norm(x_ref, shift_ref, scale_ref, nw_ref):
    x = x_ref[0]
    h = _rms(x, nw_ref[...]) * (1.0 + scale_ref[0]) + shift_ref[0]
    return h.astype(BF16)


def _wprep_kernel(wt_ref, out_ref):
    blk = wt_ref[...].astype(BF16)
    n = blk.shape[1]
    eye = (lax.broadcasted_iota(jnp.int32, (n, n), 0)
           == lax.broadcasted_iota(jnp.int32, (n, n), 1)).astype(BF16)
    out_ref[...] = lax.dot_general(eye, blk, (((1,), (1,)), ((), ())),
                                   preferred_element_type=F32).astype(BF16)


def _wprep(w_in_t):
    d = w_in_t.shape[1]

    def first_row(g):
        row = jnp.where(g < G_BGATE, g * d, IN_COL_BGATE + (g - G_BGATE) * d)
        return pl.multiple_of(row, math.gcd(d, IN_COL_BGATE))

    return pl.pallas_call(
        _wprep_kernel,
        grid=(N_WIDE_GROUPS,),
        in_specs=[pl.BlockSpec((pl.Element(d), pl.Element(d)), lambda g: (first_row(g), 0))],
        out_specs=pl.BlockSpec((d, d), lambda g: (0, g)),
        out_shape=jax.ShapeDtypeStruct((d, N_WIDE_GROUPS * d), BF16),
        compiler_params=pltpu.CompilerParams(dimension_semantics=("arbitrary",)),
        name="wprep",
    )(w_in_t)


def _gateconv_kernel(x_ref, shift_ref, scale_ref, nw_ref, w_ref, cw_ref, cb_ref,
                     cv_ref, sa_ref, sb_ref, ga_ref, gb_ref, sh_ref, hb_ref):
    k = pl.program_id(1)
    n_lc = CONV_WIDTH // CONV_LANES
    first = HALO - (CONV_KERNEL - 1)
    groups = RC // SUBLANES

    @pl.when(k == 0)
    def _():
        for lc in range(n_lc):
            sh_ref[lc, 0, 0:HALO, :] = jnp.zeros((HALO, CONV_LANES), F32)

    @pl.when(k > 0)
    def _():
        for lc in range(n_lc):
            sh_ref[lc, 0, 0:HALO, :] = sh_ref[lc, 0, TC:TC + HALO, :]

    hb_ref[...] = _modulated_norm(x_ref, shift_ref, scale_ref, nw_ref)

    def proj(group, piece):
        col = group * D_MODEL + piece * PROJ_COLS
        return jnp.dot(hb_ref[...], w_ref[:, col:col + PROJ_COLS], preferred_element_type=F32)

    for piece in range(D_MODEL // PROJ_COLS):
        u = proj(G_VAL, piece) * _sigmoid(proj(G_GLU, piece))
        for j in range(PROJ_COLS // CONV_LANES):
            lc = piece * (PROJ_COLS // CONV_LANES) + j
            sh_ref[lc, 0, HALO:, :] = u[:, j * CONV_LANES:(j + 1) * CONV_LANES]

    def gate_piece(group, out_ref, act, piece):
        cols = slice(piece * PROJ_COLS, (piece + 1) * PROJ_COLS)
        out_ref[0, :, cols] = act(proj(group, piece)).astype(BF16)

    mxu_work = []
    for group, out_ref, act in ((G_AGATE, sa_ref, _silu), (G_BGATE, sb_ref, _silu),
                                (G_GA, ga_ref, _sigmoid), (G_GB, gb_ref, _sigmoid)):
        mxu_work += [(1, functools.partial(gate_piece, group, out_ref, act, p))
                     for p in range(D_MODEL // PROJ_COLS)]

    def shift_chunk(lc):
        for r in range(1, SUBLANES):
            for rb in range(0, SHIFT_ROWS, SHIFT_BLOCK):
                sh_ref[lc, r, rb:rb + SHIFT_BLOCK, :] = sh_ref[lc, 0, rb + r:rb + r + SHIFT_BLOCK, :]

    def conv_chunk(lc, c):
        row0 = c * RC
        lanes = slice(lc * CONV_LANES, (lc + 1) * CONV_LANES)
        acc = jnp.zeros((groups, SUBLANES, CONV_LANES), F32)
        for r in range(SUBLANES):
            offs = [o for o in range(first, first + CONV_KERNEL) if o % SUBLANES == r]
            g0, g1 = offs[0] // SUBLANES, offs[-1] // SUBLANES
            span = groups + g1 - g0
            start = row0 + g0 * SUBLANES
            xs = sh_ref[lc, r, start:start + span * SUBLANES, :]
            xs = xs.reshape(span, SUBLANES, CONV_LANES)
            for o in offs:
                g = o // SUBLANES - g0
                acc = acc + cw_ref[lc, o - first][None] * xs[g:g + groups]
        cv = acc.reshape(RC, CONV_LANES) + cb_ref[:, lanes]
        cv_ref[0, row0:row0 + RC, lanes] = cv.astype(BF16)

    vpu_work = []
    for lc in range(n_lc):
        vpu_work.append(functools.partial(shift_chunk, lc))
        vpu_work += [functools.partial(conv_chunk, lc, c) for c in range(TC // RC)]

    @pl.when(k >= 0)
    def _():
        total = sum(weight for weight, _ in mxu_work)
        done, issued = 0, 0
        for weight, mxu_fn in mxu_work:
            mxu_fn()
            done += weight
            target = (len(vpu_work) * done) // total
            for vpu_fn in vpu_work[issued:target]:
                vpu_fn()
            issued = target
        assert issued == len(vpu_work)


def _gateconv(x, shift, scale, norm_w, w_wide, conv_w, conv_b):
    b, s, d = x.shape
    n_lc = d // CONV_LANES
    tok = pl.BlockSpec((1, TC, d), lambda bi, k: (bi, k, 0))
    mod = pl.BlockSpec((1, 1, d), lambda bi, k: (bi, 0, 0))
    out = jax.ShapeDtypeStruct((b, s, d), BF16)
    return pl.pallas_call(
        _gateconv_kernel,
        grid=(b, s // TC),
        in_specs=[tok, mod, mod, _const_spec((1, d)), _const_spec(w_wide.shape),
                  _const_spec(conv_w.shape), _const_spec((1, d))],
        out_specs=[tok] * 5,
        out_shape=[out] * 5,
        scratch_shapes=[pltpu.VMEM((n_lc, SUBLANES, HALO + TC, CONV_LANES), F32),
                        pltpu.VMEM((TC, d), BF16)],
        compiler_params=pltpu.CompilerParams(
            dimension_semantics=("arbitrary", "arbitrary"), vmem_limit_bytes=VMEM_LIMIT),
        name="gateconv",
    )(x, shift, scale, norm_w, w_wide, conv_w, conv_b)


def _rope_kernel(pos_ref, invf_ref, cos_ref, sin_ref):
    ang = pos_ref[...] * invf_ref[...]
    cos_ref[...] = jnp.cos(ang)
    sin_ref[...] = jnp.sin(ang)


def _rope_table(pos_dense, invf):
    rows = pos_dense.shape[0]
    tr = 512
    blk = pl.BlockSpec((tr, LANES), lambda i: (i, 0))
    out = jax.ShapeDtypeStruct((rows, LANES), F32)
    return pl.pallas_call(
        _rope_kernel,
        grid=(rows // tr,),
        in_specs=[blk, _const_spec(invf.shape)],
        out_specs=[blk, blk],
        out_shape=[out, out],
        compiler_params=pltpu.CompilerParams(dimension_semantics=("arbitrary",)),
        name="rope",
    )(pos_dense, invf)


def _mla_kernel(x_ref, shift_ref, scale_ref, nw_ref, table_ref, w_ref,
                qn_ref, kvn_ref, wuq_ref, wuk_ref, wvt_ref,
                q_ref, k_ref, vt_ref):
    hb = _modulated_norm(x_ref, shift_ref, scale_ref, nw_ref)
    lat = lax.dot_general(hb, w_ref[...], (((1,), (1,)), ((), ())),
                          preferred_element_type=F32)
    cq = lat[:, :Q_LORA_RANK]
    ckv = lat[:, Q_LORA_RANK:Q_LORA_RANK + KV_LORA_RANK]
    kp = lat[:, Q_LORA_RANK + KV_LORA_RANK:]

    table = table_ref[0]
    lane = lax.broadcasted_iota(jnp.int32, table.shape, 1)
    first_half = lane < QK_ROPE_DIM

    def rope(t):
        return jnp.where(first_half, t + pltpu.roll(t, QK_ROPE_DIM, 1), 0.0)

    cqn = _rms(cq, qn_ref[...]).astype(BF16)
    ckvn = _rms(ckv, kvn_ref[...]).astype(BF16)

    qa = jnp.dot(cqn, wuq_ref[...], preferred_element_type=F32)
    kn = jnp.dot(ckvn, wuk_ref[...], preferred_element_type=F32)
    vt = lax.dot_general(wvt_ref[...], ckvn, (((1,), (1,)), ((), ())),
                         preferred_element_type=F32)

    k_hi = rope(kp * table).astype(BF16)
    q_table = table * Q_SCALE
    for h in range(N_HEADS):
        blk = qa[:, h * QK_PAD:(h + 1) * QK_PAD]
        q_ref[0, h, :, :LANES] = (blk[:, :LANES] * Q_SCALE).astype(BF16)
        q_ref[0, h, :, LANES:] = rope(blk[:, LANES:] * q_table).astype(BF16)
        k_ref[0, h, :, :LANES] = kn[:, h * LANES:(h + 1) * LANES].astype(BF16)
        k_ref[0, h, :, LANES:] = k_hi
    ones = jnp.ones((V_PAD - V_DIM, TK), BF16)
    for j in range(TM // TK):
        for h in range(N_HEADS):
            vt_ref[0, j, h, :V_DIM, :] = vt[h * V_DIM:(h + 1) * V_DIM, j * TK:(j + 1) * TK].astype(BF16)
            vt_ref[0, j, h, V_DIM:, :] = ones


def _mla_proj(x, shift, scale, norm_w, table, w_lat_t, qn_w, kvn_w, w_uq2, w_uk, w_vt):
    b, s, d = x.shape
    tok = pl.BlockSpec((1, TM, d), lambda bi, i: (bi, i, 0))
    mod = pl.BlockSpec((1, 1, d), lambda bi, i: (bi, 0, 0))
    qk_spec = pl.BlockSpec((1, N_HEADS, TM, QK_PAD), lambda bi, i: (bi, 0, i, 0))
    return pl.pallas_call(
        _mla_kernel,
        grid=(b, s // TM),
        in_specs=[
            tok, mod, mod, _const_spec((1, d)),
            pl.BlockSpec((1, TM, LANES), lambda bi, i: (bi, i, 0)),
            _const_spec(w_lat_t.shape),
            _const_spec(qn_w.shape), _const_spec(kvn_w.shape),
            _const_spec(w_uq2.shape), _const_spec(w_uk.shape), _const_spec(w_vt.shape),
        ],
        out_specs=[
            qk_spec, qk_spec,
            pl.BlockSpec((1, TM // TK, N_HEADS, V_PAD, TK), lambda bi, i: (bi, i, 0, 0, 0)),
        ],
        out_shape=[
            jax.ShapeDtypeStruct((b, N_HEADS, s, QK_PAD), BF16),
            jax.ShapeDtypeStruct((b, N_HEADS, s, QK_PAD), BF16),
            jax.ShapeDtypeStruct((b, s // TK, N_HEADS, V_PAD, TK), BF16),
        ],
        compiler_params=pltpu.CompilerParams(
            dimension_semantics=("arbitrary", "arbitrary"), vmem_limit_bytes=VMEM_LIMIT),
        name="mla_proj",
    )(x, shift, scale, norm_w, table, w_lat_t, qn_w, kvn_w, w_uq2, w_uk, w_vt)


ATTN_BUFFERS = 3
ATTN_AHEAD = 2
ATTN_GROUP = 3


def _attn_scores(q_ref, k_ref, bias_ref, item, s_out, bm_out, diagonal):
    qn, jn = item
    k = k_ref[0, 0, pl.ds(pl.multiple_of(jn * TK, TK), TK), :]
    q = q_ref[0, 0, pl.ds(pl.multiple_of(qn * TQ, TQ), TQ), :]
    s = lax.dot_general(k, q, (((1,), (1,)), ((), ())), preferred_element_type=F32)
    if diagonal:
        s = s + bias_ref[...]
    s_out[...] = s
    bm_out[...] = jnp.max(s, axis=0, keepdims=True)


def _attn_accumulate(vt_ref, o_ref, acc_ref, item, p_in, al_in, diagonal):
    qp, jp = item
    pv = jnp.dot(vt_ref[0, jp, 0], p_in[...], preferred_element_type=F32)
    acc = al_in[...] * acc_ref[qp] + pv
    acc_ref[qp] = acc
    if diagonal:
        rows = pl.ds(pl.multiple_of(qp * TQ, TQ), TQ)
        out = acc[:V_DIM] / acc[V_DIM:V_DIM + 1]
        o_ref[0, rows, :] = jnp.transpose(out).astype(BF16)


def _attn_step(q_ref, k_ref, vt_ref, bias_ref, o_ref, acc_ref, m_ref, cur, nxt, prv, items, diag):
    s_cur, p_cur, bm_cur, al_cur = cur
    s_nxt, _, bm_nxt, _ = nxt
    _, p_prv, _, al_prv = prv
    item_next, (qc, jc), item_prev = items
    diag_next, diag_prev = diag
    _attn_accumulate(vt_ref, o_ref, acc_ref, item_prev, p_prv, al_prv, diag_prev)
    m_old = jnp.where(jc == 0, -jnp.inf, m_ref[qc])
    m_new = jnp.maximum(m_old, bm_cur[...])
    p_cur[...] = jnp.exp2(s_cur[...] - m_new).astype(BF16)
    al_cur[...] = jnp.exp2(m_old - m_new)
    m_ref[qc] = m_new
    _attn_scores(q_ref, k_ref, bias_ref, item_next, s_nxt, bm_nxt, diag_next)


def _attn_item(t, n_lower):
    def lower(t):
        q = jnp.int32(1)
        for c in range(2, 64):
            if c * (c - 1) // 2 >= n_lower:
                break
            q = q + (t >= c * (c - 1) // 2).astype(jnp.int32)
        return q, t - lax.shift_right_logical(q * (q - 1), 1)
    ql, jl = lower(jnp.minimum(t, n_lower - 1))
    d = t - n_lower
    is_diag = t >= n_lower
    return jnp.where(is_diag, d, ql), jnp.where(is_diag, d, jl)


def _attn_kernel(q_ref, k_ref, vt_ref, bias_ref, o_ref, acc_ref, m_ref, *bufs):
    n_q = q_ref.shape[2] // TQ
    n_lower = n_q * (n_q - 1) // 2
    n_items = n_lower + n_q
    nbuf = ATTN_BUFFERS
    assert TQ == TK and ATTN_GROUP % nbuf == 0 and n_items % nbuf == 0
    ring = tuple(bufs[4 * r:4 * r + 4] for r in range(nbuf))

    def item(t):
        return _attn_item(jnp.clip(t, 0, n_items - 1), n_lower)

    def steps(t0, count, t0_static=None):
        for u in range(count):
            t = t0 + u
            diag = (False, False) if t0_static is None else (
                t0_static + u + ATTN_AHEAD >= n_lower, t0_static + u - 1 >= n_lower)
            _attn_step(q_ref, k_ref, vt_ref, bias_ref, o_ref, acc_ref, m_ref,
                       ring[u % nbuf], ring[(u + ATTN_AHEAD) % nbuf], ring[(u - 1) % nbuf],
                       (item(t + ATTN_AHEAD), item(t), item(t - 1)), diag)

    acc_ref[...] = jnp.zeros_like(acc_ref)
    m_ref[...] = jnp.zeros_like(m_ref)
    _, p_last, _, al_last = ring[nbuf - 1]
    p_last[...] = jnp.zeros_like(p_last)
    al_last[...] = jnp.ones_like(al_last)
    for a in range(ATTN_AHEAD):
        s_buf, _, bm_buf, _ = ring[a]
        _attn_scores(q_ref, k_ref, bias_ref, item(jnp.int32(a)), s_buf, bm_buf, False)

    n_plain = (n_lower - ATTN_AHEAD) // ATTN_GROUP
    lax.fori_loop(0, n_plain, lambda g, c: (steps(g * ATTN_GROUP, ATTN_GROUP), c)[1], 0)
    t_tail = n_plain * ATTN_GROUP
    steps(jnp.int32(t_tail), n_items - t_tail, t_tail)
    _, p_last, _, al_last = ring[(n_items - 1) % nbuf]
    _attn_accumulate(vt_ref, o_ref, acc_ref, item(jnp.int32(n_items - 1)), p_last, al_last, True)


def _attention(q, k, vt):
    b, h, s, _ = q.shape
    n_q = s // TQ
    row = lax.broadcasted_iota(jnp.int32, (TK, TQ), 0)
    col = lax.broadcasted_iota(jnp.int32, (TK, TQ), 1)
    bias = jnp.where(row <= col, 0.0, -jnp.inf).astype(F32)
    stat = pltpu.VMEM((1, TQ), F32)
    parity = [pltpu.VMEM((TK, TQ), F32), pltpu.VMEM((TK, TQ), BF16), stat, stat]
    return pl.pallas_call(
        _attn_kernel,
        grid=(b, h),
        in_specs=[
            pl.BlockSpec((1, 1, s, QK_PAD), lambda bi, hi: (bi, hi, 0, 0)),
            pl.BlockSpec((1, 1, s, QK_PAD), lambda bi, hi: (bi, hi, 0, 0)),
            pl.BlockSpec((1, s // TK, 1, V_PAD, TK), lambda bi, hi: (bi, 0, hi, 0, 0)),
            _const_spec(bias.shape),
        ],
        out_specs=pl.BlockSpec((1, s, V_DIM), lambda bi, hi: (bi, 0, hi)),
        out_shape=jax.ShapeDtypeStruct((b, s, h * V_DIM), BF16),
        scratch_shapes=[pltpu.VMEM((n_q, V_PAD, TQ), F32), pltpu.VMEM((n_q, 1, TQ), F32)]
        + parity * ATTN_BUFFERS,
        compiler_params=pltpu.CompilerParams(
            dimension_semantics=("arbitrary", "arbitrary"),
            vmem_limit_bytes=VMEM_LIMIT),
        name="attn",
    )(q, k, vt, bias)


def _out_kernel(cv_ref, sa_ref, o_ref, sb_ref, ga_ref, gb_ref, x_ref, gate_ref, lw_ref, lb_ref,
                wco_ref, wao_ref, wo_ref, fw_ref, y_ref):
    cv = cv_ref[0].astype(F32)
    mu = jnp.mean(cv, axis=-1, keepdims=True)
    xc = cv - mu
    var = jnp.mean(xc * xc, axis=-1, keepdims=True)
    ln = xc * lax.rsqrt(var + EPS) * lw_ref[...] + lb_ref[...]
    ug = (_silu(ln) * sa_ref[0].astype(F32)).astype(BF16)
    ya = jnp.dot(ug, wco_ref[...], preferred_element_type=F32)
    og = (o_ref[0].astype(F32) * sb_ref[0].astype(F32)).astype(BF16)
    yb = jnp.dot(og, wao_ref[...], preferred_element_type=F32)
    merged = ga_ref[0].astype(F32) * ya + gb_ref[0].astype(F32) * yb
    delta = jnp.dot(merged.astype(BF16), wo_ref[...], preferred_element_type=F32)
    y = x_ref[0] + gate_ref[0] * delta
    y_ref[0] = _rms(y, fw_ref[...])


def _out(cv, sa, o, sb, ga, gb, x, gate, ln_w, ln_b, w_co, w_ao, w_o, fw):
    b, s, d = x.shape
    tok = pl.BlockSpec((1, TM, d), lambda bi, i: (bi, i, 0))
    mod = pl.BlockSpec((1, 1, d), lambda bi, i: (bi, 0, 0))
    wspec = _const_spec((d, d))
    vec = _const_spec((1, d))
    return pl.pallas_call(
        _out_kernel,
        grid=(b, s // TM),
        in_specs=[tok, tok, tok, tok, tok, tok, tok, mod, vec, vec, wspec, wspec, wspec, vec],
        out_specs=tok,
        out_shape=jax.ShapeDtypeStruct((b, s, d), F32),
        compiler_params=pltpu.CompilerParams(
            dimension_semantics=("arbitrary", "arbitrary"), vmem_limit_bytes=VMEM_LIMIT),
        name="out",
    )(cv, sa, o, sb, ga, gb, x, gate, ln_w, ln_b, w_co, w_ao, w_o, fw)


def _rot_cols(w):
    half = w.shape[-1] // 2
    return jnp.concatenate([-w[..., half:], w[..., :half]], axis=-1)


def _layer(x, c_act_pad, table, w_ada, b_ada, norm_w, w_in, conv_w, conv_b, conv_ln_w,
           conv_ln_b, w_conv_out, q_norm_w, w_uq, kv_norm_w, w_ukv, w_attn_out, w_out, final_w):
    b, s, d = x.shape
    mod = _ada(c_act_pad, w_ada, b_ada[None, :])[:b]
    shift, scale, gate = (m[:, None, :] for m in jnp.split(mod, 3, axis=-1))

    assert w_in.shape[1] == IN_COLS
    w_wide = _wprep(w_in.T)
    w_kpe = w_in[:, IN_COL_BGATE - QK_ROPE_DIM:IN_COL_BGATE]
    w_lat_t = jnp.concatenate([w_in[:, IN_COL_LATENT:IN_COL_BGATE], _rot_cols(w_kpe)],
                              axis=1).T.astype(BF16)

    wq = w_uq.reshape(Q_LORA_RANK, N_HEADS, QK_NOPE_DIM + QK_ROPE_DIM)
    wq_rope = wq[..., QK_NOPE_DIM:]
    w_uq2 = jnp.concatenate([wq, _rot_cols(wq_rope)], axis=-1)
    w_uq2 = w_uq2.reshape(Q_LORA_RANK, N_HEADS * QK_PAD).astype(BF16)
    wkv = w_ukv.reshape(KV_LORA_RANK, N_HEADS, QK_NOPE_DIM + V_DIM)
    w_uk = wkv[..., :QK_NOPE_DIM].reshape(KV_LORA_RANK, N_HEADS * QK_NOPE_DIM).astype(BF16)
    w_vt = wkv[..., QK_NOPE_DIM:].reshape(KV_LORA_RANK, ATTN_WIDTH).T.astype(BF16)

    nw = norm_w[None, :]
    cw = conv_w.reshape(CONV_KERNEL, CONV_WIDTH // CONV_LANES, 1, CONV_LANES).transpose(1, 0, 2, 3)
    cw = jnp.broadcast_to(cw, (CONV_WIDTH // CONV_LANES, CONV_KERNEL, SUBLANES, CONV_LANES))
    cv, sa, sb, ga, gb = _gateconv(x, shift, scale, nw, w_wide, cw, conv_b[None, :])
    q, k, vt = _mla_proj(x, shift, scale, nw, table, w_lat_t, q_norm_w[None, :],
                         kv_norm_w[None, :], w_uq2, w_uk, w_vt)
    o = _attention(q, k, vt)
    return _out(cv, sa, o, sb, ga, gb, x, gate, conv_ln_w[None, :], conv_ln_b[None, :],
                w_conv_out.astype(BF16), w_attn_out.astype(BF16), w_out.astype(BF16),
                final_w[None, :])


def kernel(x, c, positions, w_ada, b_ada, norm_w, w_in, conv_w, conv_b, conv_ln_w, conv_ln_b,
           w_conv_out, q_norm_w, w_uq, kv_norm_w, w_ukv, w_attn_out, w_out, final_norm_w):
    b, s, d = x.shape
    depth = w_ada.shape[0]
    assert depth == 1, "final rmsnorm is fused into the single layer's output kernel"
    inv_freq = ROPE_THETA ** (-jnp.arange(0, QK_ROPE_DIM, 2, dtype=F32) / QK_ROPE_DIM)
    n_freq = inv_freq.shape[0]
    per_row = LANES // n_freq
    invf = jnp.tile(inv_freq, per_row)[None, :]
    pos_dense = jnp.repeat(positions.astype(F32).reshape(b * s // per_row, per_row), n_freq, axis=1)
    cos_d, sin_d = _rope_table(pos_dense, invf)
    cos_t = cos_d.reshape(b, s, n_freq)
    sin_t = sin_d.reshape(b, s, n_freq)
    table = jnp.concatenate([cos_t, cos_t, sin_t, sin_t], axis=-1)
    c_pad = jnp.pad(c, ((0, 8 - b), (0, 0)))
    return _layer(x, c_pad, table, w_ada[0], b_ada[0], norm_w[0], w_in[0], conv_w[0],
                  conv_b[0], conv_ln_w[0], conv_ln_b[0], w_conv_out[0], q_norm_w[0], w_uq[0],
                  kv_norm_w[0], w_ukv[0], w_attn_out[0], w_out[0], final_norm_w)
```

```python
import functools
import math

import jax
import jax.numpy as jnp
from jax import lax
from jax.experimental import pallas as pl
from jax.experimental.pallas import tpu as pltpu

F32 = jnp.float32
BF16 = jnp.bfloat16

D_MODEL = 1024
CONV_WIDTH = 1024
CONV_KERNEL = 31
N_HEADS = 8
QK_NOPE_DIM = 128
QK_ROPE_DIM = 64
V_DIM = 128
Q_LORA_RANK = 256
KV_LORA_RANK = 256
ATTN_WIDTH = N_HEADS * V_DIM
ROPE_THETA = 10000.0
EPS = 1e-6

LANES = 128
QK_PAD = 2 * LANES
BF16_SUBLANES = 16
V_PAD = V_DIM + BF16_SUBLANES
ROPE_PACK = LANES // (QK_ROPE_DIM // 2)
HALO = 32
VMEM_LIMIT = 56 * 1024 * 1024

Q_SCALE = (QK_NOPE_DIM + QK_ROPE_DIM) ** -0.5 * math.log2(math.e)

IN_COL_LATENT = 3 * CONV_WIDTH
IN_COL_BGATE = IN_COL_LATENT + Q_LORA_RANK + KV_LORA_RANK + QK_ROPE_DIM
IN_COLS = IN_COL_BGATE + ATTN_WIDTH + 2 * D_MODEL
G_VAL, G_GLU, G_AGATE, G_BGATE, G_GA, G_GB = range(6)
N_WIDE_GROUPS = 6
assert CONV_WIDTH == ATTN_WIDTH == D_MODEL

TM = 512
TC = 256
RC = 64
CONV_LANES = 128
PROJ_COLS = 512
SUBLANES = 8
SHIFT_ROWS = TC + HALO - SUBLANES
SHIFT_BLOCK = 40
assert SHIFT_ROWS % SHIFT_BLOCK == 0 and SHIFT_BLOCK % SUBLANES == 0
TQ = 512
TK = 512


def _sigmoid(x):
    return 1.0 / (1.0 + jnp.exp(-x))


def _silu(x):
    return x * _sigmoid(x)


def _rms(x, w):
    return x * lax.rsqrt(jnp.mean(x * x, axis=-1, keepdims=True) + EPS) * w


def _const_spec(shape):
    return pl.BlockSpec(shape, lambda *_: (0,) * len(shape), pipeline_mode=pl.Buffered(1))


def _ada_kernel(c_ref, w_ref, b_ref, o_ref):
    c = c_ref[...]
    o_ref[...] = jnp.dot(_silu(c), w_ref[...], preferred_element_type=F32) + b_ref[...]


def _ada(c_pad, w_ada, b_ada):
    rows, d = c_pad.shape
    n = w_ada.shape[1]
    tn = 512
    return pl.pallas_call(
        _ada_kernel,
        grid=(n // tn,),
        in_specs=[
            pl.BlockSpec((rows, d), lambda j: (0, 0)),
            pl.BlockSpec((d, tn), lambda j: (0, j)),
            pl.BlockSpec((1, tn), lambda j: (0, j)),
        ],
        out_specs=pl.BlockSpec((rows, tn), lambda j: (0, j)),
        out_shape=jax.ShapeDtypeStruct((rows, n), F32),
        compiler_params=pltpu.CompilerParams(dimension_semantics=("arbitrary",)),
        name="ada",
    )(c_pad, w_ada, b_ada)


def _modulated_norm(x_ref, shift_ref, scale_ref, nw_ref):
    x = x_ref[0]
    h = _rms(x, nw_ref[...]) * (1.0 + scale_ref[0]) + shift_ref[0]
    return h.astype(BF16)


def _wprep_kernel(wt_ref, out_ref):
    blk = wt_ref[...].astype(BF16)
    n = blk.shape[1]
    eye = (lax.broadcasted_iota(jnp.int32, (n, n), 0)
           == lax.broadcasted_iota(jnp.int32, (n, n), 1)).astype(BF16)
    out_ref[...] = lax.dot_general(eye, blk, (((1,), (1,)), ((), ())),
                                   preferred_element_type=F32).astype(BF16)


def _wprep(w_in_t):
    d = w_in_t.shape[1]

    def first_row(g):
        row = jnp.where(g < G_BGATE, g * d, IN_COL_BGATE + (g - G_BGATE) * d)
        return pl.multiple_of(row, math.gcd(d, IN_COL_BGATE))

    return pl.pallas_call(
        _wprep_kernel,
        grid=(N_WIDE_GROUPS,),
        in_specs=[pl.BlockSpec((pl.Element(d), pl.Element(d)), lambda g: (first_row(g), 0))],
        out_specs=pl.BlockSpec((d, d), lambda g: (0, g)),
        out_shape=jax.ShapeDtypeStruct((d, N_WIDE_GROUPS * d), BF16),
        compiler_params=pltpu.CompilerParams(dimension_semantics=("arbitrary",)),
        name="wprep",
    )(w_in_t)


def _gateconv_kernel(x_ref, shift_ref, scale_ref, nw_ref, w_ref, cw_ref, cb_ref,
                     cv_ref, sa_ref, sb_ref, ga_ref, gb_ref, sh_ref, hb_ref):
    k = pl.program_id(1)
    n_lc = CONV_WIDTH // CONV_LANES
    first = HALO - (CONV_KERNEL - 1)
    groups = RC // SUBLANES

    @pl.when(k == 0)
    def _():
        for lc in range(n_lc):
            sh_ref[lc, 0, 0:HALO, :] = jnp.zeros((HALO, CONV_LANES), F32)

    @pl.when(k > 0)
    def _():
        for lc in range(n_lc):
            sh_ref[lc, 0, 0:HALO, :] = sh_ref[lc, 0, TC:TC + HALO, :]

    hb_ref[...] = _modulated_norm(x_ref, shift_ref, scale_ref, nw_ref)

    def proj(group, piece):
        col = group * D_MODEL + piece * PROJ_COLS
        return jnp.dot(hb_ref[...], w_ref[:, col:col + PROJ_COLS], preferred_element_type=F32)

    for piece in range(D_MODEL // PROJ_COLS):
        u = proj(G_VAL, piece) * _sigmoid(proj(G_GLU, piece))
        for j in range(PROJ_COLS // CONV_LANES):
            lc = piece * (PROJ_COLS // CONV_LANES) + j
            sh_ref[lc, 0, HALO:, :] = u[:, j * CONV_LANES:(j + 1) * CONV_LANES]

    def gate_piece(group, out_ref, act, piece):
        cols = slice(piece * PROJ_COLS, (piece + 1) * PROJ_COLS)
        out_ref[0, :, cols] = act(proj(group, piece)).astype(BF16)

    mxu_work = []
    for group, out_ref, act in ((G_AGATE, sa_ref, _silu), (G_BGATE, sb_ref, _silu),
                                (G_GA, ga_ref, _sigmoid), (G_GB, gb_ref, _sigmoid)):
        mxu_work += [(1, functools.partial(gate_piece, group, out_ref, act, p))
                     for p in range(D_MODEL // PROJ_COLS)]

    def shift_chunk(lc):
        for r in range(1, SUBLANES):
            for rb in range(0, SHIFT_ROWS, SHIFT_BLOCK):
                sh_ref[lc, r, rb:rb + SHIFT_BLOCK, :] = sh_ref[lc, 0, rb + r:rb + r + SHIFT_BLOCK, :]

    def conv_chunk(lc, c):
        row0 = c * RC
        lanes = slice(lc * CONV_LANES, (lc + 1) * CONV_LANES)
        acc = jnp.zeros((groups, SUBLANES, CONV_LANES), F32)
        for r in range(SUBLANES):
            offs = [o for o in range(first, first + CONV_KERNEL) if o % SUBLANES == r]
            g0, g1 = offs[0] // SUBLANES, offs[-1] // SUBLANES
            span = groups + g1 - g0
            start = row0 + g0 * SUBLANES
            xs = sh_ref[lc, r, start:start + span * SUBLANES, :]
            xs = xs.reshape(span, SUBLANES, CONV_LANES)
            for o in offs:
                g = o // SUBLANES - g0
                acc = acc + cw_ref[lc, o - first][None] * xs[g:g + groups]
        cv = acc.reshape(RC, CONV_LANES) + cb_ref[:, lanes]
        cv_ref[0, row0:row0 + RC, lanes] = cv.astype(BF16)

    vpu_work = []
    for lc in range(n_lc):
        vpu_work.append(functools.partial(shift_chunk, lc))
        vpu_work += [functools.partial(conv_chunk, lc, c) for c in range(TC // RC)]

    @pl.when(k >= 0)
    def _():
        total = sum(weight for weight, _ in mxu_work)
        done, issued = 0, 0
        for weight, mxu_fn in mxu_work:
            mxu_fn()
            done += weight
            target = (len(vpu_work) * done) // total
            for vpu_fn in vpu_work[issued:target]:
                vpu_fn()
            issued = target
        assert issued == len(vpu_work)


def _gateconv(x, shift, scale, norm_w, w_wide, conv_w, conv_b):
    b, s, d = x.shape
    n_lc = d // CONV_LANES
    tok = pl.BlockSpec((1, TC, d), lambda bi, k: (bi, k, 0))
    mod = pl.BlockSpec((1, 1, d), lambda bi, k: (bi, 0, 0))
    out = jax.ShapeDtypeStruct((b, s, d), BF16)
    return pl.pallas_call(
        _gateconv_kernel,
        grid=(b, s // TC),
        in_specs=[tok, mod, mod, _const_spec((1, d)), _const_spec(w_wide.shape),
                  _const_spec(conv_w.shape), _const_spec((1, d))],
        out_specs=[tok] * 5,
        out_shape=[out] * 5,
        scratch_shapes=[pltpu.VMEM((n_lc, SUBLANES, HALO + TC, CONV_LANES), F32),
                        pltpu.VMEM((TC, d), BF16)],
        compiler_params=pltpu.CompilerParams(
            dimension_semantics=("arbitrary", "arbitrary"), vmem_limit_bytes=VMEM_LIMIT),
        name="gateconv",
    )(x, shift, scale, norm_w, w_wide, conv_w, conv_b)


def _rope_kernel(pos_ref, invf_ref, table_ref):
    ang = pos_ref[...] * invf_ref[...]
    cos_d, sin_d = jnp.cos(ang), jnp.sin(ang)
    n_freq = LANES // ROPE_PACK
    quarter = lax.broadcasted_iota(jnp.int32, ang.shape, 1) // n_freq
    for p in range(ROPE_PACK):
        def placed(src, q):
            return pltpu.roll(src, ((q - p) * n_freq) % LANES, 1)
        row = jnp.where(quarter == 0, placed(cos_d, 0),
                        jnp.where(quarter == 1, placed(cos_d, 1),
                                  jnp.where(quarter == 2, placed(sin_d, 2), placed(sin_d, 3))))
        table_ref[pl.ds(p, ang.shape[0], stride=ROPE_PACK), :] = row


def _rope_table(pos_dense, invf):
    rows = pos_dense.shape[0]
    tr = 512
    return pl.pallas_call(
        _rope_kernel,
        grid=(rows // tr,),
        in_specs=[pl.BlockSpec((tr, LANES), lambda i: (i, 0)), _const_spec(invf.shape)],
        out_specs=pl.BlockSpec((ROPE_PACK * tr, LANES), lambda i: (i, 0)),
        out_shape=jax.ShapeDtypeStruct((ROPE_PACK * rows, LANES), F32),
        compiler_params=pltpu.CompilerParams(dimension_semantics=("arbitrary",)),
        name="rope",
    )(pos_dense, invf)


def _mla_kernel(x_ref, shift_ref, scale_ref, nw_ref, table_ref, w_ref,
                qn_ref, kvn_ref, wuq_ref, wuk_ref, wvt_ref,
                q_ref, k_ref, vt_ref):
    hb = _modulated_norm(x_ref, shift_ref, scale_ref, nw_ref)
    lat = lax.dot_general(hb, w_ref[...], (((1,), (1,)), ((), ())),
                          preferred_element_type=F32)
    cq = lat[:, :Q_LORA_RANK]
    ckv = lat[:, Q_LORA_RANK:Q_LORA_RANK + KV_LORA_RANK]
    kp = lat[:, Q_LORA_RANK + KV_LORA_RANK:]

    table = table_ref[0]
    lane = lax.broadcasted_iota(jnp.int32, table.shape, 1)
    first_half = lane < QK_ROPE_DIM

    def rope(t):
        return jnp.where(first_half, t + pltpu.roll(t, QK_ROPE_DIM, 1), 0.0)

    cqn = _rms(cq, qn_ref[...]).astype(BF16)
    ckvn = _rms(ckv, kvn_ref[...]).astype(BF16)

    qa = jnp.dot(cqn, wuq_ref[...], preferred_element_type=F32)
    kn = jnp.dot(ckvn, wuk_ref[...], preferred_element_type=F32)
    vt = lax.dot_general(wvt_ref[...], ckvn, (((1,), (1,)), ((), ())),
                         preferred_element_type=F32)

    k_hi = rope(kp * table).astype(BF16)
    q_table = table * Q_SCALE
    for h in range(N_HEADS):
        blk = qa[:, h * QK_PAD:(h + 1) * QK_PAD]
        q_ref[0, h, :, :LANES] = (blk[:, :LANES] * Q_SCALE).astype(BF16)
        q_ref[0, h, :, LANES:] = rope(blk[:, LANES:] * q_table).astype(BF16)
        k_ref[0, h, :, :LANES] = kn[:, h * LANES:(h + 1) * LANES].astype(BF16)
        k_ref[0, h, :, LANES:] = k_hi
    ones = jnp.ones((V_PAD - V_DIM, TK), BF16)
    for j in range(TM // TK):
        for h in range(N_HEADS):
            vt_ref[0, j, h, :V_DIM, :] = vt[h * V_DIM:(h + 1) * V_DIM, j * TK:(j + 1) * TK].astype(BF16)
            vt_ref[0, j, h, V_DIM:, :] = ones


def _mla_proj(x, shift, scale, norm_w, table, w_lat_t, qn_w, kvn_w, w_uq2, w_uk, w_vt):
    b, s, d = x.shape
    tok = pl.BlockSpec((1, TM, d), lambda bi, i: (bi, i, 0))
    mod = pl.BlockSpec((1, 1, d), lambda bi, i: (bi, 0, 0))
    qk_spec = pl.BlockSpec((1, N_HEADS, TM, QK_PAD), lambda bi, i: (bi, 0, i, 0))
    return pl.pallas_call(
        _mla_kernel,
        grid=(b, s // TM),
        in_specs=[
            tok, mod, mod, _const_spec((1, d)),
            pl.BlockSpec((1, TM, LANES), lambda bi, i: (bi, i, 0)),
            _const_spec(w_lat_t.shape),
            _const_spec(qn_w.shape), _const_spec(kvn_w.shape),
            _const_spec(w_uq2.shape), _const_spec(w_uk.shape), _const_spec(w_vt.shape),
        ],
        out_specs=[
            qk_spec, qk_spec,
            pl.BlockSpec((1, TM // TK, N_HEADS, V_PAD, TK), lambda bi, i: (bi, i, 0, 0, 0)),
        ],
        out_shape=[
            jax.ShapeDtypeStruct((b, N_HEADS, s, QK_PAD), BF16),
            jax.ShapeDtypeStruct((b, N_HEADS, s, QK_PAD), BF16),
            jax.ShapeDtypeStruct((b, s // TK, N_HEADS, V_PAD, TK), BF16),
        ],
        compiler_params=pltpu.CompilerParams(
            dimension_semantics=("arbitrary", "arbitrary"), vmem_limit_bytes=VMEM_LIMIT),
        name="mla_proj",
    )(x, shift, scale, norm_w, table, w_lat_t, qn_w, kvn_w, w_uq2, w_uk, w_vt)


ATTN_BUFFERS = 3
ATTN_AHEAD = 2
ATTN_GROUP = 3


def _attn_scores(q_ref, k_ref, bias_ref, item, s_out, bm_out, diagonal):
    qn, jn = item
    k = k_ref[0, 0, pl.ds(pl.multiple_of(jn * TK, TK), TK), :]
    q = q_ref[0, 0, pl.ds(pl.multiple_of(qn * TQ, TQ), TQ), :]
    s = lax.dot_general(k, q, (((1,), (1,)), ((), ())), preferred_element_type=F32)
    if diagonal:
        s = s + bias_ref[...]
    s_out[...] = s
    bm_out[...] = jnp.max(s, axis=0, keepdims=True)


def _attn_accumulate(vt_ref, o_ref, acc_ref, item, p_in, al_in, diagonal):
    qp, jp = item
    pv = jnp.dot(vt_ref[0, jp, 0], p_in[...], preferred_element_type=F32)
    acc = al_in[...] * acc_ref[qp] + pv
    acc_ref[qp] = acc
    if diagonal:
        rows = pl.ds(pl.multiple_of(qp * TQ, TQ), TQ)
        out = acc[:V_DIM] / acc[V_DIM:V_DIM + 1]
        o_ref[0, rows, :] = jnp.transpose(out).astype(BF16)


def _attn_step(q_ref, k_ref, vt_ref, bias_ref, o_ref, acc_ref, m_ref, cur, nxt, prv, items, diag):
    s_cur, p_cur, bm_cur, al_cur = cur
    s_nxt, _, bm_nxt, _ = nxt
    _, p_prv, _, al_prv = prv
    item_next, (qc, jc), item_prev = items
    diag_next, diag_prev = diag
    _attn_accumulate(vt_ref, o_ref, acc_ref, item_prev, p_prv, al_prv, diag_prev)
    m_old = jnp.where(jc == 0, -jnp.inf, m_ref[qc])
    m_new = jnp.maximum(m_old, bm_cur[...])
    p_cur[...] = jnp.exp2(s_cur[...] - m_new).astype(BF16)
    al_cur[...] = jnp.exp2(m_old - m_new)
    m_ref[qc] = m_new
    _attn_scores(q_ref, k_ref, bias_ref, item_next, s_nxt, bm_nxt, diag_next)


def _attn_item(t, n_lower):
    def lower(t):
        q = jnp.int32(1)
        for c in range(2, 64):
            if c * (c - 1) // 2 >= n_lower:
                break
            q = q + (t >= c * (c - 1) // 2).astype(jnp.int32)
        return q, t - lax.shift_right_logical(q * (q - 1), 1)
    ql, jl = lower(jnp.minimum(t, n_lower - 1))
    d = t - n_lower
    is_diag = t >= n_lower
    return jnp.where(is_diag, d, ql), jnp.where(is_diag, d, jl)


def _attn_kernel(q_ref, k_ref, vt_ref, bias_ref, o_ref, acc_ref, m_ref, *bufs):
    n_q = q_ref.shape[2] // TQ
    n_lower = n_q * (n_q - 1) // 2
    n_items = n_lower + n_q
    nbuf = ATTN_BUFFERS
    assert TQ == TK and ATTN_GROUP % nbuf == 0 and n_items % nbuf == 0
    ring = tuple(bufs[4 * r:4 * r + 4] for r in range(nbuf))

    def item(t):
        return _attn_item(jnp.clip(t, 0, n_items - 1), n_lower)

    def steps(t0, count, t0_static=None):
        for u in range(count):
            t = t0 + u
            diag = (False, False) if t0_static is None else (
                t0_static + u + ATTN_AHEAD >= n_lower, t0_static + u - 1 >= n_lower)
            _attn_step(q_ref, k_ref, vt_ref, bias_ref, o_ref, acc_ref, m_ref,
                       ring[u % nbuf], ring[(u + ATTN_AHEAD) % nbuf], ring[(u - 1) % nbuf],
                       (item(t + ATTN_AHEAD), item(t), item(t - 1)), diag)

    acc_ref[...] = jnp.zeros_like(acc_ref)
    m_ref[...] = jnp.zeros_like(m_ref)
    _, p_last, _, al_last = ring[nbuf - 1]
    p_last[...] = jnp.zeros_like(p_last)
    al_last[...] = jnp.ones_like(al_last)
    for a in range(ATTN_AHEAD):
        s_buf, _, bm_buf, _ = ring[a]
        _attn_scores(q_ref, k_ref, bias_ref, item(jnp.int32(a)), s_buf, bm_buf, False)

    n_plain = (n_lower - ATTN_AHEAD) // ATTN_GROUP
    lax.fori_loop(0, n_plain, lambda g, c: (steps(g * ATTN_GROUP, ATTN_GROUP), c)[1], 0)
    t_tail = n_plain * ATTN_GROUP
    steps(jnp.int32(t_tail), n_items - t_tail, t_tail)
    _, p_last, _, al_last = ring[(n_items - 1) % nbuf]
    _attn_accumulate(vt_ref, o_ref, acc_ref, item(jnp.int32(n_items - 1)), p_last, al_last, True)


def _attention(q, k, vt):
    b, h, s, _ = q.shape
    n_q = s // TQ
    row = lax.broadcasted_iota(jnp.int32, (TK, TQ), 0)
    col = lax.broadcasted_iota(jnp.int32, (TK, TQ), 1)
    bias = jnp.where(row <= col, 0.0, -jnp.inf).astype(F32)
    stat = pltpu.VMEM((1, TQ), F32)
    parity = [pltpu.VMEM((TK, TQ), F32), pltpu.VMEM((TK, TQ), BF16), stat, stat]
    return pl.pallas_call(
        _attn_kernel,
        grid=(b, h),
        in_specs=[
            pl.BlockSpec((1, 1, s, QK_PAD), lambda bi, hi: (bi, hi, 0, 0)),
            pl.BlockSpec((1, 1, s, QK_PAD), lambda bi, hi: (bi, hi, 0, 0)),
            pl.BlockSpec((1, s // TK, 1, V_PAD, TK), lambda bi, hi: (bi, 0, hi, 0, 0)),
            _const_spec(bias.shape),
        ],
        out_specs=pl.BlockSpec((1, s, V_DIM), lambda bi, hi: (bi, 0, hi)),
        out_shape=jax.ShapeDtypeStruct((b, s, h * V_DIM), BF16),
        scratch_shapes=[pltpu.VMEM((n_q, V_PAD, TQ), F32), pltpu.VMEM((n_q, 1, TQ), F32)]
        + parity * ATTN_BUFFERS,
        compiler_params=pltpu.CompilerParams(
            dimension_semantics=("arbitrary", "arbitrary"),
            vmem_limit_bytes=VMEM_LIMIT),
        name="attn",
    )(q, k, vt, bias)


def _out_kernel(cv_ref, sa_ref, o_ref, sb_ref, ga_ref, gb_ref, x_ref, gate_ref, lw_ref, lb_ref,
                wco_ref, wao_ref, wo_ref, fw_ref, y_ref):
    cv = cv_ref[0].astype(F32)
    mu = jnp.mean(cv, axis=-1, keepdims=True)
    xc = cv - mu
    var = jnp.mean(xc * xc, axis=-1, keepdims=True)
    ln = xc * lax.rsqrt(var + EPS) * lw_ref[...] + lb_ref[...]
    ug = (_silu(ln) * sa_ref[0].astype(F32)).astype(BF16)
    ya = jnp.dot(ug, wco_ref[...], preferred_element_type=F32)
    og = (o_ref[0].astype(F32) * sb_ref[0].astype(F32)).astype(BF16)
    yb = jnp.dot(og, wao_ref[...], preferred_element_type=F32)
    merged = ga_ref[0].astype(F32) * ya + gb_ref[0].astype(F32) * yb
    delta = jnp.dot(merged.astype(BF16), wo_ref[...], preferred_element_type=F32)
    y = x_ref[0] + gate_ref[0] * delta
    y_ref[0] = _rms(y, fw_ref[...])


def _out(cv, sa, o, sb, ga, gb, x, gate, ln_w, ln_b, w_co, w_ao, w_o, fw):
    b, s, d = x.shape
    tok = pl.BlockSpec((1, TM, d), lambda bi, i: (bi, i, 0))
    mod = pl.BlockSpec((1, 1, d), lambda bi, i: (bi, 0, 0))
    wspec = _const_spec((d, d))
    vec = _const_spec((1, d))
    return pl.pallas_call(
        _out_kernel,
        grid=(b, s // TM),
        in_specs=[tok, tok, tok, tok, tok, tok, tok, mod, vec, vec, wspec, wspec, wspec, vec],
        out_specs=tok,
        out_shape=jax.ShapeDtypeStruct((b, s, d), F32),
        compiler_params=pltpu.CompilerParams(
            dimension_semantics=("arbitrary", "arbitrary"), vmem_limit_bytes=VMEM_LIMIT),
        name="out",
    )(cv, sa, o, sb, ga, gb, x, gate, ln_w, ln_b, w_co, w_ao, w_o, fw)


def _rot_cols(w):
    half = w.shape[-1] // 2
    return jnp.concatenate([-w[..., half:], w[..., :half]], axis=-1)


def _layer(x, c_act_pad, table, w_ada, b_ada, norm_w, w_in, conv_w, conv_b, conv_ln_w,
           conv_ln_b, w_conv_out, q_norm_w, w_uq, kv_norm_w, w_ukv, w_attn_out, w_out, final_w):
    b, s, d = x.shape
    mod = _ada(c_act_pad, w_ada, b_ada[None, :])[:b]
    shift, scale, gate = (m[:, None, :] for m in jnp.split(mod, 3, axis=-1))

    assert w_in.shape[1] == IN_COLS
    w_in_t = w_in.T
    w_wide = _wprep(w_in_t)
    w_kpe_t = w_in_t[IN_COL_BGATE - QK_ROPE_DIM:IN_COL_BGATE]
    half = QK_ROPE_DIM // 2
    w_lat_t = jnp.concatenate([w_in_t[IN_COL_LATENT:IN_COL_BGATE], -w_kpe_t[half:], w_kpe_t[:half]],
                              axis=0).astype(BF16)

    wq = w_uq.reshape(Q_LORA_RANK, N_HEADS, QK_NOPE_DIM + QK_ROPE_DIM)
    wq_rope = wq[..., QK_NOPE_DIM:]
    w_uq2 = jnp.concatenate([wq, _rot_cols(wq_rope)], axis=-1)
    w_uq2 = w_uq2.reshape(Q_LORA_RANK, N_HEADS * QK_PAD).astype(BF16)
    wkv = w_ukv.reshape(KV_LORA_RANK, N_HEADS, QK_NOPE_DIM + V_DIM)
    w_uk = wkv[..., :QK_NOPE_DIM].reshape(KV_LORA_RANK, N_HEADS * QK_NOPE_DIM).astype(BF16)
    w_vt = wkv[..., QK_NOPE_DIM:].reshape(KV_LORA_RANK, ATTN_WIDTH).T.astype(BF16)

    nw = norm_w[None, :]
    cw = conv_w.reshape(CONV_KERNEL, CONV_WIDTH // CONV_LANES, 1, CONV_LANES).transpose(1, 0, 2, 3)
    cw = jnp.broadcast_to(cw, (CONV_WIDTH // CONV_LANES, CONV_KERNEL, SUBLANES, CONV_LANES))
    cv, sa, sb, ga, gb = _gateconv(x, shift, scale, nw, w_wide, cw, conv_b[None, :])
    q, k, vt = _mla_proj(x, shift, scale, nw, table, w_lat_t, q_norm_w[None, :],
                         kv_norm_w[None, :], w_uq2, w_uk, w_vt)
    o = _attention(q, k, vt)
    return _out(cv, sa, o, sb, ga, gb, x, gate, conv_ln_w[None, :], conv_ln_b[None, :],
                w_conv_out.astype(BF16), w_attn_out.astype(BF16), w_out.astype(BF16),
                final_w[None, :])


def kernel(x, c, positions, w_ada, b_ada, norm_w, w_in, conv_w, conv_b, conv_ln_w, conv_ln_b,
           w_conv_out, q_norm_w, w_uq, kv_norm_w, w_ukv, w_attn_out, w_out, final_norm_w):
    b, s, d = x.shape
    depth = w_ada.shape[0]
    assert depth == 1, "final rmsnorm is fused into the single layer's output kernel"
    inv_freq = ROPE_THETA ** (-jnp.arange(0, QK_ROPE_DIM, 2, dtype=F32) / QK_ROPE_DIM)
    n_freq = inv_freq.shape[0]
    assert ROPE_PACK * n_freq == LANES
    invf = jnp.tile(inv_freq, ROPE_PACK)[None, :]
    pos_dense = jnp.repeat(positions.astype(F32).reshape(b * s // ROPE_PACK, ROPE_PACK), n_freq, axis=1)
    table = _rope_table(pos_dense, invf).reshape(b, s, LANES)
    c_pad = jnp.pad(c, ((0, 8 - b), (0, 0)))
    return _layer(x, c_pad, table, w_ada[0], b_ada[0], norm_w[0], w_in[0], conv_w[0],
                  conv_b[0], conv_ln_w[0], conv_ln_b[0], w_conv_out[0], q_norm_w[0], w_uq[0],
                  kv_norm_w[0], w_ukv[0], w_attn_out[0], w_out[0], final_norm_w)
```

```python
import functools
import math

import jax
import jax.numpy as jnp
from jax import lax
from jax.experimental import pallas as pl
from jax.experimental.pallas import tpu as pltpu

F32 = jnp.float32
BF16 = jnp.bfloat16

D_MODEL = 1024
CONV_WIDTH = 1024
CONV_KERNEL = 31
N_HEADS = 8
QK_NOPE_DIM = 128
QK_ROPE_DIM = 64
V_DIM = 128
Q_LORA_RANK = 256
KV_LORA_RANK = 256
ATTN_WIDTH = N_HEADS * V_DIM
ROPE_THETA = 10000.0
EPS = 1e-6

LANES = 128
QK_PAD = 2 * LANES
BF16_SUBLANES = 16
V_PAD = V_DIM + BF16_SUBLANES
ROPE_PACK = LANES // (QK_ROPE_DIM // 2)
HALO = 32
VMEM_LIMIT = 56 * 1024 * 1024

Q_SCALE = (QK_NOPE_DIM + QK_ROPE_DIM) ** -0.5 * math.log2(math.e)

IN_COL_LATENT = 3 * CONV_WIDTH
IN_COL_BGATE = IN_COL_LATENT + Q_LORA_RANK + KV_LORA_RANK + QK_ROPE_DIM
IN_COLS = IN_COL_BGATE + ATTN_WIDTH + 2 * D_MODEL
G_VAL, G_GLU, G_AGATE, G_BGATE, G_GA, G_GB = range(6)
N_WIDE_GROUPS = 6
assert CONV_WIDTH == ATTN_WIDTH == D_MODEL

TM = 512
TC = 256
RC = 64
CONV_LANES = 256
PROJ_COLS = 1024
SUBLANES = 8
SHIFT_ROWS = TC + HALO - SUBLANES
SHIFT_BLOCK = 40
assert SHIFT_ROWS % SHIFT_BLOCK == 0 and SHIFT_BLOCK % SUBLANES == 0
TQ = 512
TK = 512


def _sigmoid(x):
    return 1.0 / (1.0 + jnp.exp(-x))


def _silu(x):
    return x * _sigmoid(x)


def _rms(x, w):
    return x * lax.rsqrt(jnp.mean(x * x, axis=-1, keepdims=True) + EPS) * w


def _const_spec(shape):
    return pl.BlockSpec(shape, lambda *_: (0,) * len(shape), pipeline_mode=pl.Buffered(1))


def _ada_kernel(c_ref, w_ref, b_ref, o_ref):
    c = c_ref[...]
    o_ref[...] = jnp.dot(_silu(c), w_ref[...], preferred_element_type=F32) + b_ref[...]


def _ada(c_pad, w_ada, b_ada):
    rows, d = c_pad.shape
    n = w_ada.shape[1]
    tn = 512
    return pl.pallas_call(
        _ada_kernel,
        grid=(n // tn,),
        in_specs=[
            pl.BlockSpec((rows, d), lambda j: (0, 0)),
            pl.BlockSpec((d, tn), lambda j: (0, j)),
            pl.BlockSpec((1, tn), lambda j: (0, j)),
        ],
        out_specs=pl.BlockSpec((rows, tn), lambda j: (0, j)),
        out_shape=jax.ShapeDtypeStruct((rows, n), F32),
        compiler_params=pltpu.CompilerParams(dimension_semantics=("arbitrary",)),
        name="ada",
    )(c_pad, w_ada, b_ada)


def _modulated_norm(x_ref, shift_ref, scale_ref, nw_ref):
    x = x_ref[0]
    h = _rms(x, nw_ref[...]) * (1.0 + scale_ref[0]) + shift_ref[0]
    return h.astype(BF16)


def _wprep_kernel(wt_ref, out_ref):
    blk = wt_ref[...].astype(BF16)
    n = blk.shape[1]
    eye = (lax.broadcasted_iota(jnp.int32, (n, n), 0)
           == lax.broadcasted_iota(jnp.int32, (n, n), 1)).astype(BF16)
    out_ref[...] = lax.dot_general(eye, blk, (((1,), (1,)), ((), ())),
                                   preferred_element_type=F32).astype(BF16)


def _wprep(w_in_t):
    d = w_in_t.shape[1]

    def first_row(g):
        row = jnp.where(g < G_BGATE, g * d, IN_COL_BGATE + (g - G_BGATE) * d)
        return pl.multiple_of(row, math.gcd(d, IN_COL_BGATE))

    return pl.pallas_call(
        _wprep_kernel,
        grid=(N_WIDE_GROUPS,),
        in_specs=[pl.BlockSpec((pl.Element(d), pl.Element(d)), lambda g: (first_row(g), 0))],
        out_specs=pl.BlockSpec((d, d), lambda g: (0, g)),
        out_shape=jax.ShapeDtypeStruct((d, N_WIDE_GROUPS * d), BF16),
        compiler_params=pltpu.CompilerParams(dimension_semantics=("arbitrary",)),
        name="wprep",
    )(w_in_t)


def _gateconv_kernel(x_ref, shift_ref, scale_ref, nw_ref, w_ref, cw_ref, cb_ref,
                     cv_ref, sa_ref, sb_ref, ga_ref, gb_ref, sh_ref, hb_ref):
    k = pl.program_id(1)
    n_lc = CONV_WIDTH // CONV_LANES
    first = HALO - (CONV_KERNEL - 1)
    groups = RC // SUBLANES

    @pl.when(k == 0)
    def _():
        for lc in range(n_lc):
            sh_ref[lc, 0, 0:HALO, :] = jnp.zeros((HALO, CONV_LANES), F32)

    @pl.when(k > 0)
    def _():
        for lc in range(n_lc):
            sh_ref[lc, 0, 0:HALO, :] = sh_ref[lc, 0, TC:TC + HALO, :]

    hb_ref[...] = _modulated_norm(x_ref, shift_ref, scale_ref, nw_ref)

    def proj(group, piece):
        col = group * D_MODEL + piece * PROJ_COLS
        return jnp.dot(hb_ref[...], w_ref[:, col:col + PROJ_COLS], preferred_element_type=F32)

    for piece in range(D_MODEL // PROJ_COLS):
        u = proj(G_VAL, piece) * _sigmoid(proj(G_GLU, piece))
        for j in range(PROJ_COLS // CONV_LANES):
            lc = piece * (PROJ_COLS // CONV_LANES) + j
            sh_ref[lc, 0, HALO:, :] = u[:, j * CONV_LANES:(j + 1) * CONV_LANES]

    def gate_piece(group, out_ref, act, piece):
        cols = slice(piece * PROJ_COLS, (piece + 1) * PROJ_COLS)
        out_ref[0, :, cols] = act(proj(group, piece)).astype(BF16)

    mxu_work = []
    for group, out_ref, act in ((G_AGATE, sa_ref, _silu), (G_BGATE, sb_ref, _silu),
                                (G_GA, ga_ref, _sigmoid), (G_GB, gb_ref, _sigmoid)):
        mxu_work += [(1, functools.partial(gate_piece, group, out_ref, act, p))
                     for p in range(D_MODEL // PROJ_COLS)]

    def shift_chunk(lc):
        for r in range(1, SUBLANES):
            for rb in range(0, SHIFT_ROWS, SHIFT_BLOCK):
                sh_ref[lc, r, rb:rb + SHIFT_BLOCK, :] = sh_ref[lc, 0, rb + r:rb + r + SHIFT_BLOCK, :]

    def conv_chunk(lc, c):
        row0 = c * RC
        lanes = slice(lc * CONV_LANES, (lc + 1) * CONV_LANES)
        acc = jnp.zeros((groups, SUBLANES, CONV_LANES), F32)
        for r in range(SUBLANES):
            offs = [o for o in range(first, first + CONV_KERNEL) if o % SUBLANES == r]
            g0, g1 = offs[0] // SUBLANES, offs[-1] // SUBLANES
            span = groups + g1 - g0
            start = row0 + g0 * SUBLANES
            xs = sh_ref[lc, r, start:start + span * SUBLANES, :]
            xs = xs.reshape(span, SUBLANES, CONV_LANES)
            for o in offs:
                g = o // SUBLANES - g0
                acc = acc + cw_ref[lc, o - first][None] * xs[g:g + groups]
        cv = acc.reshape(RC, CONV_LANES) + cb_ref[:, lanes]
        cv_ref[0, row0:row0 + RC, lanes] = cv.astype(BF16)

    vpu_work = []
    for lc in range(n_lc):
        vpu_work.append(functools.partial(shift_chunk, lc))
        vpu_work += [functools.partial(conv_chunk, lc, c) for c in range(TC // RC)]

    @pl.when(k >= 0)
    def _():
        total = sum(weight for weight, _ in mxu_work)
        done, issued = 0, 0
        for weight, mxu_fn in mxu_work:
            mxu_fn()
            done += weight
            target = (len(vpu_work) * done) // total
            for vpu_fn in vpu_work[issued:target]:
                vpu_fn()
            issued = target
        assert issued == len(vpu_work)


def _gateconv(x, shift, scale, norm_w, w_wide, conv_w, conv_b):
    b, s, d = x.shape
    n_lc = d // CONV_LANES
    tok = pl.BlockSpec((1, TC, d), lambda bi, k: (bi, k, 0))
    mod = pl.BlockSpec((1, 1, d), lambda bi, k: (bi, 0, 0))
    out = jax.ShapeDtypeStruct((b, s, d), BF16)
    return pl.pallas_call(
        _gateconv_kernel,
        grid=(b, s // TC),
        in_specs=[tok, mod, mod, _const_spec((1, d)), _const_spec(w_wide.shape),
                  _const_spec(conv_w.shape), _const_spec((1, d))],
        out_specs=[tok] * 5,
        out_shape=[out] * 5,
        scratch_shapes=[pltpu.VMEM((n_lc, SUBLANES, HALO + TC, CONV_LANES), F32),
                        pltpu.VMEM((TC, d), BF16)],
        compiler_params=pltpu.CompilerParams(
            dimension_semantics=("arbitrary", "arbitrary"), vmem_limit_bytes=VMEM_LIMIT),
        name="gateconv",
    )(x, shift, scale, norm_w, w_wide, conv_w, conv_b)


def _rope_kernel(pos_ref, invf_ref, table_ref):
    ang = pos_ref[...] * invf_ref[...]
    cos_d, sin_d = jnp.cos(ang), jnp.sin(ang)
    n_freq = LANES // ROPE_PACK
    quarter = lax.broadcasted_iota(jnp.int32, ang.shape, 1) // n_freq
    for p in range(ROPE_PACK):
        def placed(src, q):
            return pltpu.roll(src, ((q - p) * n_freq) % LANES, 1)
        row = jnp.where(quarter == 0, placed(cos_d, 0),
                        jnp.where(quarter == 1, placed(cos_d, 1),
                                  jnp.where(quarter == 2, placed(sin_d, 2), placed(sin_d, 3))))
        table_ref[pl.ds(p, ang.shape[0], stride=ROPE_PACK), :] = row


def _rope_table(pos_dense, invf):
    rows = pos_dense.shape[0]
    tr = 512
    return pl.pallas_call(
        _rope_kernel,
        grid=(rows // tr,),
        in_specs=[pl.BlockSpec((tr, LANES), lambda i: (i, 0)), _const_spec(invf.shape)],
        out_specs=pl.BlockSpec((ROPE_PACK * tr, LANES), lambda i: (i, 0)),
        out_shape=jax.ShapeDtypeStruct((ROPE_PACK * rows, LANES), F32),
        compiler_params=pltpu.CompilerParams(dimension_semantics=("arbitrary",)),
        name="rope",
    )(pos_dense, invf)


def _mla_kernel(x_ref, shift_ref, scale_ref, nw_ref, table_ref, w_ref,
                qn_ref, kvn_ref, wuq_ref, wuk_ref, wvt_ref,
                q_ref, k_ref, vt_ref):
    hb = _modulated_norm(x_ref, shift_ref, scale_ref, nw_ref)
    lat = lax.dot_general(hb, w_ref[...], (((1,), (1,)), ((), ())),
                          preferred_element_type=F32)
    cq = lat[:, :Q_LORA_RANK]
    ckv = lat[:, Q_LORA_RANK:Q_LORA_RANK + KV_LORA_RANK]
    kp = lat[:, Q_LORA_RANK + KV_LORA_RANK:]

    table = table_ref[0]
    lane = lax.broadcasted_iota(jnp.int32, table.shape, 1)
    first_half = lane < QK_ROPE_DIM

    def rope(t):
        return jnp.where(first_half, t + pltpu.roll(t, QK_ROPE_DIM, 1), 0.0)

    cqn = _rms(cq, qn_ref[...]).astype(BF16)
    ckvn = _rms(ckv, kvn_ref[...]).astype(BF16)

    qa = jnp.dot(cqn, wuq_ref[...], preferred_element_type=F32)
    kn = jnp.dot(ckvn, wuk_ref[...], preferred_element_type=F32)
    vt = lax.dot_general(wvt_ref[...], ckvn, (((1,), (1,)), ((), ())),
                         preferred_element_type=F32)

    k_hi = rope(kp * table).astype(BF16)
    q_table = table * Q_SCALE
    for h in range(N_HEADS):
        blk = qa[:, h * QK_PAD:(h + 1) * QK_PAD]
        q_ref[0, h, :, :LANES] = (blk[:, :LANES] * Q_SCALE).astype(BF16)
        q_ref[0, h, :, LANES:] = rope(blk[:, LANES:] * q_table).astype(BF16)
        k_ref[0, h, :, :LANES] = kn[:, h * LANES:(h + 1) * LANES].astype(BF16)
        k_ref[0, h, :, LANES:] = k_hi
    ones = jnp.ones((V_PAD - V_DIM, TK), BF16)
    for j in range(TM // TK):
        for h in range(N_HEADS):
            vt_ref[0, j, h, :V_DIM, :] = vt[h * V_DIM:(h + 1) * V_DIM, j * TK:(j + 1) * TK].astype(BF16)
            vt_ref[0, j, h, V_DIM:, :] = ones


def _mla_proj(x, shift, scale, norm_w, table, w_lat_t, qn_w, kvn_w, w_uq2, w_uk, w_vt):
    b, s, d = x.shape
    tok = pl.BlockSpec((1, TM, d), lambda bi, i: (bi, i, 0))
    mod = pl.BlockSpec((1, 1, d), lambda bi, i: (bi, 0, 0))
    qk_spec = pl.BlockSpec((1, N_HEADS, TM, QK_PAD), lambda bi, i: (bi, 0, i, 0))
    return pl.pallas_call(
        _mla_kernel,
        grid=(b, s // TM),
        in_specs=[
            tok, mod, mod, _const_spec((1, d)),
            pl.BlockSpec((1, TM, LANES), lambda bi, i: (bi, i, 0)),
            _const_spec(w_lat_t.shape),
            _const_spec(qn_w.shape), _const_spec(kvn_w.shape),
            _const_spec(w_uq2.shape), _const_spec(w_uk.shape), _const_spec(w_vt.shape),
        ],
        out_specs=[
            qk_spec, qk_spec,
            pl.BlockSpec((1, TM // TK, N_HEADS, V_PAD, TK), lambda bi, i: (bi, i, 0, 0, 0)),
        ],
        out_shape=[
            jax.ShapeDtypeStruct((b, N_HEADS, s, QK_PAD), BF16),
            jax.ShapeDtypeStruct((b, N_HEADS, s, QK_PAD), BF16),
            jax.ShapeDtypeStruct((b, s // TK, N_HEADS, V_PAD, TK), BF16),
        ],
        compiler_params=pltpu.CompilerParams(
            dimension_semantics=("arbitrary", "arbitrary"), vmem_limit_bytes=VMEM_LIMIT),
        name="mla_proj",
    )(x, shift, scale, norm_w, table, w_lat_t, qn_w, kvn_w, w_uq2, w_uk, w_vt)


ATTN_BUFFERS = 3
ATTN_AHEAD = 2
ATTN_GROUP = 3


def _attn_scores(q_ref, k_ref, bias_ref, item, s_out, bm_out, diagonal):
    qn, jn = item
    k = k_ref[0, 0, pl.ds(pl.multiple_of(jn * TK, TK), TK), :]
    q = q_ref[0, 0, pl.ds(pl.multiple_of(qn * TQ, TQ), TQ), :]
    s = lax.dot_general(k, q, (((1,), (1,)), ((), ())), preferred_element_type=F32)
    if diagonal:
        s = s + bias_ref[...]
    s_out[...] = s
    bm_out[...] = jnp.max(s, axis=0, keepdims=True)


def _attn_accumulate(vt_ref, o_ref, acc_ref, item, p_in, al_in, diagonal):
    qp, jp = item
    pv = jnp.dot(vt_ref[0, jp, 0], p_in[...], preferred_element_type=F32)
    acc = al_in[...] * acc_ref[qp] + pv
    acc_ref[qp] = acc
    if diagonal:
        rows = pl.ds(pl.multiple_of(qp * TQ, TQ), TQ)
        out = acc[:V_DIM] / acc[V_DIM:V_DIM + 1]
        o_ref[0, rows, :] = jnp.transpose(out).astype(BF16)


def _attn_step(q_ref, k_ref, vt_ref, bias_ref, o_ref, acc_ref, m_ref, cur, nxt, prv, items, diag):
    s_cur, p_cur, bm_cur, al_cur = cur
    s_nxt, _, bm_nxt, _ = nxt
    _, p_prv, _, al_prv = prv
    item_next, (qc, jc), item_prev = items
    diag_next, diag_prev = diag
    _attn_accumulate(vt_ref, o_ref, acc_ref, item_prev, p_prv, al_prv, diag_prev)
    m_old = jnp.where(jc == 0, -jnp.inf, m_ref[qc])
    m_new = jnp.maximum(m_old, bm_cur[...])
    p_cur[...] = jnp.exp2(s_cur[...] - m_new).astype(BF16)
    al_cur[...] = jnp.exp2(m_old - m_new)
    m_ref[qc] = m_new
    _attn_scores(q_ref, k_ref, bias_ref, item_next, s_nxt, bm_nxt, diag_next)


def _attn_item(t, n_lower):
    def lower(t):
        q = jnp.int32(1)
        for c in range(2, 64):
            if c * (c - 1) // 2 >= n_lower:
                break
            q = q + (t >= c * (c - 1) // 2).astype(jnp.int32)
        return q, t - lax.shift_right_logical(q * (q - 1), 1)
    ql, jl = lower(jnp.minimum(t, n_lower - 1))
    d = t - n_lower
    is_diag = t >= n_lower
    return jnp.where(is_diag, d, ql), jnp.where(is_diag, d, jl)


def _attn_kernel(q_ref, k_ref, vt_ref, bias_ref, o_ref, acc_ref, m_ref, *bufs):
    n_q = q_ref.shape[2] // TQ
    n_lower = n_q * (n_q - 1) // 2
    n_items = n_lower + n_q
    nbuf = ATTN_BUFFERS
    assert TQ == TK and ATTN_GROUP % nbuf == 0 and n_items % nbuf == 0
    ring = tuple(bufs[4 * r:4 * r + 4] for r in range(nbuf))

    def item(t):
        return _attn_item(jnp.clip(t, 0, n_items - 1), n_lower)

    def steps(t0, count, t0_static=None):
        for u in range(count):
            t = t0 + u
            diag = (False, False) if t0_static is None else (
                t0_static + u + ATTN_AHEAD >= n_lower, t0_static + u - 1 >= n_lower)
            _attn_step(q_ref, k_ref, vt_ref, bias_ref, o_ref, acc_ref, m_ref,
                       ring[u % nbuf], ring[(u + ATTN_AHEAD) % nbuf], ring[(u - 1) % nbuf],
                       (item(t + ATTN_AHEAD), item(t), item(t - 1)), diag)

    acc_ref[...] = jnp.zeros_like(acc_ref)
    m_ref[...] = jnp.zeros_like(m_ref)
    _, p_last, _, al_last = ring[nbuf - 1]
    p_last[...] = jnp.zeros_like(p_last)
    al_last[...] = jnp.ones_like(al_last)
    for a in range(ATTN_AHEAD):
        s_buf, _, bm_buf, _ = ring[a]
        _attn_scores(q_ref, k_ref, bias_ref, item(jnp.int32(a)), s_buf, bm_buf, False)

    n_plain = (n_lower - ATTN_AHEAD) // ATTN_GROUP
    lax.fori_loop(0, n_plain, lambda g, c: (steps(g * ATTN_GROUP, ATTN_GROUP), c)[1], 0)
    t_tail = n_plain * ATTN_GROUP
    steps(jnp.int32(t_tail), n_items - t_tail, t_tail)
    _, p_last, _, al_last = ring[(n_items - 1) % nbuf]
    _attn_accumulate(vt_ref, o_ref, acc_ref, item(jnp.int32(n_items - 1)), p_last, al_last, True)


def _attention(q, k, vt):
    b, h, s, _ = q.shape
    n_q = s // TQ
    row = lax.broadcasted_iota(jnp.int32, (TK, TQ), 0)
    col = lax.broadcasted_iota(jnp.int32, (TK, TQ), 1)
    bias = jnp.where(row <= col, 0.0, -jnp.inf).astype(F32)
    stat = pltpu.VMEM((1, TQ), F32)
    parity = [pltpu.VMEM((TK, TQ), F32), pltpu.VMEM((TK, TQ), BF16), stat, stat]
    return pl.pallas_call(
        _attn_kernel,
        grid=(b, h),
        in_specs=[
            pl.BlockSpec((1, 1, s, QK_PAD), lambda bi, hi: (bi, hi, 0, 0)),
            pl.BlockSpec((1, 1, s, QK_PAD), lambda bi, hi: (bi, hi, 0, 0)),
            pl.BlockSpec((1, s // TK, 1, V_PAD, TK), lambda bi, hi: (bi, 0, hi, 0, 0)),
            _const_spec(bias.shape),
        ],
        out_specs=pl.BlockSpec((1, s, V_DIM), lambda bi, hi: (bi, 0, hi)),
        out_shape=jax.ShapeDtypeStruct((b, s, h * V_DIM), BF16),
        scratch_shapes=[pltpu.VMEM((n_q, V_PAD, TQ), F32), pltpu.VMEM((n_q, 1, TQ), F32)]
        + parity * ATTN_BUFFERS,
        compiler_params=pltpu.CompilerParams(
            dimension_semantics=("arbitrary", "arbitrary"),
            vmem_limit_bytes=VMEM_LIMIT),
        name="attn",
    )(q, k, vt, bias)


def _out_kernel(cv_ref, sa_ref, o_ref, sb_ref, ga_ref, gb_ref, x_ref, gate_ref, lw_ref, lb_ref,
                wco_ref, wao_ref, wo_ref, fw_ref, y_ref):
    cv = cv_ref[0].astype(F32)
    mu = jnp.mean(cv, axis=-1, keepdims=True)
    xc = cv - mu
    var = jnp.mean(xc * xc, axis=-1, keepdims=True)
    ln = xc * lax.rsqrt(var + EPS) * lw_ref[...] + lb_ref[...]
    ug = (_silu(ln) * sa_ref[0].astype(F32)).astype(BF16)
    ya = jnp.dot(ug, wco_ref[...], preferred_element_type=F32)
    og = (o_ref[0].astype(F32) * sb_ref[0].astype(F32)).astype(BF16)
    yb = jnp.dot(og, wao_ref[...], preferred_element_type=F32)
    merged = ga_ref[0].astype(F32) * ya + gb_ref[0].astype(F32) * yb
    delta = jnp.dot(merged.astype(BF16), wo_ref[...], preferred_element_type=F32)
    y = x_ref[0] + gate_ref[0] * delta
    y_ref[0] = _rms(y, fw_ref[...])


def _out(cv, sa, o, sb, ga, gb, x, gate, ln_w, ln_b, w_co, w_ao, w_o, fw):
    b, s, d = x.shape
    tok = pl.BlockSpec((1, TM, d), lambda bi, i: (bi, i, 0))
    mod = pl.BlockSpec((1, 1, d), lambda bi, i: (bi, 0, 0))
    wspec = _const_spec((d, d))
    vec = _const_spec((1, d))
    return pl.pallas_call(
        _out_kernel,
        grid=(b, s // TM),
        in_specs=[tok, tok, tok, tok, tok, tok, tok, mod, vec, vec, wspec, wspec, wspec, vec],
        out_specs=tok,
        out_shape=jax.ShapeDtypeStruct((b, s, d), F32),
        compiler_params=pltpu.CompilerParams(
            dimension_semantics=("arbitrary", "arbitrary"), vmem_limit_bytes=VMEM_LIMIT),
        name="out",
    )(cv, sa, o, sb, ga, gb, x, gate, ln_w, ln_b, w_co, w_ao, w_o, fw)


def _rot_cols(w):
    half = w.shape[-1] // 2
    return jnp.concatenate([-w[..., half:], w[..., :half]], axis=-1)


def _layer(x, c_act_pad, table, w_ada, b_ada, norm_w, w_in, conv_w, conv_b, conv_ln_w,
           conv_ln_b, w_conv_out, q_norm_w, w_uq, kv_norm_w, w_ukv, w_attn_out, w_out, final_w):
    b, s, d = x.shape
    mod = _ada(c_act_pad, w_ada, b_ada[None, :])[:b]
    shift, scale, gate = (m[:, None, :] for m in jnp.split(mod, 3, axis=-1))

    assert w_in.shape[1] == IN_COLS
    w_in_t = w_in.T
    w_wide = _wprep(w_in_t)
    w_kpe_t = w_in_t[IN_COL_BGATE - QK_ROPE_DIM:IN_COL_BGATE]
    half = QK_ROPE_DIM // 2
    w_lat_t = jnp.concatenate([w_in_t[IN_COL_LATENT:IN_COL_BGATE], -w_kpe_t[half:], w_kpe_t[:half]],
                              axis=0).astype(BF16)

    wq = w_uq.reshape(Q_LORA_RANK, N_HEADS, QK_NOPE_DIM + QK_ROPE_DIM)
    wq_rope = wq[..., QK_NOPE_DIM:]
    w_uq2 = jnp.concatenate([wq, _rot_cols(wq_rope)], axis=-1)
    w_uq2 = w_uq2.reshape(Q_LORA_RANK, N_HEADS * QK_PAD).astype(BF16)
    wkv = w_ukv.reshape(KV_LORA_RANK, N_HEADS, QK_NOPE_DIM + V_DIM)
    w_uk = wkv[..., :QK_NOPE_DIM].reshape(KV_LORA_RANK, N_HEADS * QK_NOPE_DIM).astype(BF16)
    w_vt = wkv[..., QK_NOPE_DIM:].reshape(KV_LORA_RANK, ATTN_WIDTH).T.astype(BF16)

    nw = norm_w[None, :]
    cw = conv_w.reshape(CONV_KERNEL, CONV_WIDTH // CONV_LANES, 1, CONV_LANES).transpose(1, 0, 2, 3)
    cw = jnp.broadcast_to(cw, (CONV_WIDTH // CONV_LANES, CONV_KERNEL, SUBLANES, CONV_LANES))
    cv, sa, sb, ga, gb = _gateconv(x, shift, scale, nw, w_wide, cw, conv_b[None, :])
    q, k, vt = _mla_proj(x, shift, scale, nw, table, w_lat_t, q_norm_w[None, :],
                         kv_norm_w[None, :], w_uq2, w_uk, w_vt)
    o = _attention(q, k, vt)
    return _out(cv, sa, o, sb, ga, gb, x, gate, conv_ln_w[None, :], conv_ln_b[None, :],
                w_conv_out.astype(BF16), w_attn_out.astype(BF16), w_out.astype(BF16),
                final_w[None, :])


def kernel(x, c, positions, w_ada, b_ada, norm_w, w_in, conv_w, conv_b, conv_ln_w, conv_ln_b,
           w_conv_out, q_norm_w, w_uq, kv_norm_w, w_ukv, w_attn_out, w_out, final_norm_w):
    b, s, d = x.shape
    depth = w_ada.shape[0]
    assert depth == 1, "final rmsnorm is fused into the single layer's output kernel"
    inv_freq = ROPE_THETA ** (-jnp.arange(0, QK_ROPE_DIM, 2, dtype=F32) / QK_ROPE_DIM)
    n_freq = inv_freq.shape[0]
    assert ROPE_PACK * n_freq == LANES
    invf = jnp.tile(inv_freq, ROPE_PACK)[None, :]
    pos_dense = jnp.repeat(positions.astype(F32).reshape(b * s // ROPE_PACK, ROPE_PACK), n_freq, axis=1)
    table = _rope_table(pos_dense, invf).reshape(b, s, LANES)
    c_pad = jnp.pad(c, ((0, 8 - b), (0, 0)))
    return _layer(x, c_pad, table, w_ada[0], b_ada[0], norm_w[0], w_in[0], conv_w[0],
                  conv_b[0], conv_ln_w[0], conv_ln_b[0], w_conv_out[0], q_norm_w[0], w_uq[0],
                  kv_norm_w[0], w_ukv[0], w_attn_out[0], w_out[0], final_norm_w)
```

```python
import functools
import math

import jax
import jax.numpy as jnp
from jax import lax
from jax.experimental import pallas as pl
from jax.experimental.pallas import tpu as pltpu

F32 = jnp.float32
BF16 = jnp.bfloat16

D_MODEL = 1024
CONV_WIDTH = 1024
CONV_KERNEL = 31
N_HEADS = 8
QK_NOPE_DIM = 128
QK_ROPE_DIM = 64
V_DIM = 128
Q_LORA_RANK = 256
KV_LORA_RANK = 256
ATTN_WIDTH = N_HEADS * V_DIM
ROPE_THETA = 10000.0
EPS = 1e-6

LANES = 128
QK_PAD = 2 * LANES
BF16_SUBLANES = 16
V_PAD = V_DIM + BF16_SUBLANES
ROPE_PACK = LANES // (QK_ROPE_DIM // 2)
HALO = 32
VMEM_LIMIT = 56 * 1024 * 1024

Q_SCALE = (QK_NOPE_DIM + QK_ROPE_DIM) ** -0.5 * math.log2(math.e)

IN_COL_LATENT = 3 * CONV_WIDTH
IN_COL_BGATE = IN_COL_LATENT + Q_LORA_RANK + KV_LORA_RANK + QK_ROPE_DIM
IN_COLS = IN_COL_BGATE + ATTN_WIDTH + 2 * D_MODEL
G_VAL, G_GLU, G_AGATE, G_BGATE, G_GA, G_GB = range(6)
N_WIDE_GROUPS = 6
assert CONV_WIDTH == ATTN_WIDTH == D_MODEL

TM = 512
TC = 256
RC = 64
CONV_LANES = 256
PROJ_COLS = 1024
SUBLANES = 8
SHIFT_ROWS = TC + HALO - SUBLANES
SHIFT_BLOCK = 40
assert SHIFT_ROWS % SHIFT_BLOCK == 0 and SHIFT_BLOCK % SUBLANES == 0
TQ = 512
TK = 512


def _sigmoid(x):
    return 1.0 / (1.0 + jnp.exp(-x))


def _silu(x):
    return x * _sigmoid(x)


def _rms(x, w):
    return x * lax.rsqrt(jnp.mean(x * x, axis=-1, keepdims=True) + EPS) * w


def _const_spec(shape):
    return pl.BlockSpec(shape, lambda *_: (0,) * len(shape), pipeline_mode=pl.Buffered(1))


def _ada_kernel(c_ref, w_ref, b_ref, o_ref):
    c = c_ref[...]
    o_ref[...] = jnp.dot(_silu(c), w_ref[...], preferred_element_type=F32) + b_ref[...]


def _ada(c_pad, w_ada, b_ada):
    rows, d = c_pad.shape
    n = w_ada.shape[1]
    tn = 512
    return pl.pallas_call(
        _ada_kernel,
        grid=(n // tn,),
        in_specs=[
            pl.BlockSpec((rows, d), lambda j: (0, 0)),
            pl.BlockSpec((d, tn), lambda j: (0, j)),
            pl.BlockSpec((1, tn), lambda j: (0, j)),
        ],
        out_specs=pl.BlockSpec((rows, tn), lambda j: (0, j)),
        out_shape=jax.ShapeDtypeStruct((rows, n), F32),
        compiler_params=pltpu.CompilerParams(dimension_semantics=("arbitrary",)),
        name="ada",
    )(c_pad, w_ada, b_ada)


def _modulated_norm(x_ref, shift_ref, scale_ref, nw_ref):
    x = x_ref[0]
    h = _rms(x, nw_ref[...]) * (1.0 + scale_ref[0]) + shift_ref[0]
    return h.astype(BF16)


def _wprep_kernel(wt_ref, out_ref):
    blk = wt_ref[...].astype(BF16)
    n = blk.shape[1]
    eye = (lax.broadcasted_iota(jnp.int32, (n, n), 0)
           == lax.broadcasted_iota(jnp.int32, (n, n), 1)).astype(BF16)
    out_ref[...] = lax.dot_general(eye, blk, (((1,), (1,)), ((), ())),
                                   preferred_element_type=F32).astype(BF16)


def _wprep(w_in_t):
    d = w_in_t.shape[1]

    def first_row(g):
        row = jnp.where(g < G_BGATE, g * d, IN_COL_BGATE + (g - G_BGATE) * d)
        return pl.multiple_of(row, math.gcd(d, IN_COL_BGATE))

    return pl.pallas_call(
        _wprep_kernel,
        grid=(N_WIDE_GROUPS,),
        in_specs=[pl.BlockSpec((pl.Element(d), pl.Element(d)), lambda g: (first_row(g), 0))],
        out_specs=pl.BlockSpec((d, d), lambda g: (0, g)),
        out_shape=jax.ShapeDtypeStruct((d, N_WIDE_GROUPS * d), BF16),
        compiler_params=pltpu.CompilerParams(dimension_semantics=("arbitrary",)),
        name="wprep",
    )(w_in_t)


def _gateconv_kernel(x_ref, shift_ref, scale_ref, nw_ref, w_ref, cw_ref, cb_ref,
                     cv_ref, sa_ref, sb_ref, ga_ref, gb_ref, sh_ref, hb_ref):
    k = pl.program_id(1)
    n_lc = CONV_WIDTH // CONV_LANES
    first = HALO - (CONV_KERNEL - 1)
    groups = RC // SUBLANES

    @pl.when(k == 0)
    def _():
        for lc in range(n_lc):
            sh_ref[lc, 0, 0:HALO, :] = jnp.zeros((HALO, CONV_LANES), F32)

    @pl.when(k > 0)
    def _():
        for lc in range(n_lc):
            sh_ref[lc, 0, 0:HALO, :] = sh_ref[lc, 0, TC:TC + HALO, :]

    hb_ref[...] = _modulated_norm(x_ref, shift_ref, scale_ref, nw_ref)

    def proj(group, piece):
        col = group * D_MODEL + piece * PROJ_COLS
        return jnp.dot(hb_ref[...], w_ref[:, col:col + PROJ_COLS], preferred_element_type=F32)

    for piece in range(D_MODEL // PROJ_COLS):
        u = proj(G_VAL, piece) * _sigmoid(proj(G_GLU, piece))
        for j in range(PROJ_COLS // CONV_LANES):
            lc = piece * (PROJ_COLS // CONV_LANES) + j
            sh_ref[lc, 0, HALO:, :] = u[:, j * CONV_LANES:(j + 1) * CONV_LANES]

    def gate_piece(group, out_ref, act, piece):
        cols = slice(piece * PROJ_COLS, (piece + 1) * PROJ_COLS)
        out_ref[0, :, cols] = act(proj(group, piece)).astype(BF16)

    mxu_work = []
    for group, out_ref, act in ((G_AGATE, sa_ref, _silu), (G_BGATE, sb_ref, _silu),
                                (G_GA, ga_ref, _sigmoid), (G_GB, gb_ref, _sigmoid)):
        mxu_work += [(1, functools.partial(gate_piece, group, out_ref, act, p))
                     for p in range(D_MODEL // PROJ_COLS)]

    def shift_chunk(lc):
        for r in range(1, SUBLANES):
            for rb in range(0, SHIFT_ROWS, SHIFT_BLOCK):
                sh_ref[lc, r, rb:rb + SHIFT_BLOCK, :] = sh_ref[lc, 0, rb + r:rb + r + SHIFT_BLOCK, :]

    def conv_chunk(lc, c):
        row0 = c * RC
        lanes = slice(lc * CONV_LANES, (lc + 1) * CONV_LANES)
        acc = jnp.zeros((groups, SUBLANES, CONV_LANES), F32)
        for r in range(SUBLANES):
            offs = [o for o in range(first, first + CONV_KERNEL) if o % SUBLANES == r]
            g0, g1 = offs[0] // SUBLANES, offs[-1] // SUBLANES
            span = groups + g1 - g0
            start = row0 + g0 * SUBLANES
            xs = sh_ref[lc, r, start:start + span * SUBLANES, :]
            xs = xs.reshape(span, SUBLANES, CONV_LANES)
            for o in offs:
                g = o // SUBLANES - g0
                acc = acc + cw_ref[lc, o - first][None] * xs[g:g + groups]
        cv = acc.reshape(RC, CONV_LANES) + cb_ref[:, lanes]
        cv_ref[0, row0:row0 + RC, lanes] = cv.astype(BF16)

    vpu_work = []
    for lc in range(n_lc):
        vpu_work.append(functools.partial(shift_chunk, lc))
        vpu_work += [functools.partial(conv_chunk, lc, c) for c in range(TC // RC)]

    @pl.when(k >= 0)
    def _():
        total = sum(weight for weight, _ in mxu_work)
        done, issued = 0, 0
        for weight, mxu_fn in mxu_work:
            mxu_fn()
            done += weight
            target = (len(vpu_work) * done) // total
            for vpu_fn in vpu_work[issued:target]:
                vpu_fn()
            issued = target
        assert issued == len(vpu_work)


def _gateconv(x, shift, scale, norm_w, w_wide, conv_w, conv_b):
    b, s, d = x.shape
    n_lc = d // CONV_LANES
    tok = pl.BlockSpec((1, TC, d), lambda bi, k: (bi, k, 0))
    mod = pl.BlockSpec((1, 1, d), lambda bi, k: (bi, 0, 0))
    out = jax.ShapeDtypeStruct((b, s, d), BF16)
    return pl.pallas_call(
        _gateconv_kernel,
        grid=(b, s // TC),
        in_specs=[tok, mod, mod, _const_spec((1, d)), _const_spec(w_wide.shape),
                  _const_spec(conv_w.shape), _const_spec((1, d))],
        out_specs=[tok] * 5,
        out_shape=[out] * 5,
        scratch_shapes=[pltpu.VMEM((n_lc, SUBLANES, HALO + TC, CONV_LANES), F32),
                        pltpu.VMEM((TC, d), BF16)],
        compiler_params=pltpu.CompilerParams(
            dimension_semantics=("arbitrary", "arbitrary"), vmem_limit_bytes=VMEM_LIMIT),
        name="gateconv",
    )(x, shift, scale, norm_w, w_wide, conv_w, conv_b)


def _rope_kernel(pos_ref, invf_ref, table_ref):
    ang = pos_ref[...] * invf_ref[...]
    cos_d, sin_d = jnp.cos(ang), jnp.sin(ang)
    n_freq = LANES // ROPE_PACK
    quarter = lax.broadcasted_iota(jnp.int32, ang.shape, 1) // n_freq
    for p in range(ROPE_PACK):
        def placed(src, q):
            return pltpu.roll(src, ((q - p) * n_freq) % LANES, 1)
        row = jnp.where(quarter == 0, placed(cos_d, 0),
                        jnp.where(quarter == 1, placed(cos_d, 1),
                                  jnp.where(quarter == 2, placed(sin_d, 2), placed(sin_d, 3))))
        table_ref[pl.ds(p, ang.shape[0], stride=ROPE_PACK), :] = row


def _rope_table(pos_dense, invf):
    rows = pos_dense.shape[0]
    tr = 512
    return pl.pallas_call(
        _rope_kernel,
        grid=(rows // tr,),
        in_specs=[pl.BlockSpec((tr, LANES), lambda i: (i, 0)), _const_spec(invf.shape)],
        out_specs=pl.BlockSpec((ROPE_PACK * tr, LANES), lambda i: (i, 0)),
        out_shape=jax.ShapeDtypeStruct((ROPE_PACK * rows, LANES), F32),
        compiler_params=pltpu.CompilerParams(dimension_semantics=("arbitrary",)),
        name="rope",
    )(pos_dense, invf)


def _mla_kernel(x_ref, shift_ref, scale_ref, nw_ref, table_ref, w_ref,
                qn_ref, kvn_ref, wuq_ref, wuk_ref, wvt_ref,
                q_ref, k_ref, vt_ref):
    hb = _modulated_norm(x_ref, shift_ref, scale_ref, nw_ref)
    lat = lax.dot_general(hb, w_ref[...], (((1,), (1,)), ((), ())),
                          preferred_element_type=F32)
    cq = lat[:, :Q_LORA_RANK]
    ckv = lat[:, Q_LORA_RANK:Q_LORA_RANK + KV_LORA_RANK]
    kp = lat[:, Q_LORA_RANK + KV_LORA_RANK:]

    table = table_ref[0]
    lane = lax.broadcasted_iota(jnp.int32, table.shape, 1)
    first_half = lane < QK_ROPE_DIM

    def rope(t):
        return jnp.where(first_half, t + pltpu.roll(t, QK_ROPE_DIM, 1), 0.0)

    cqn = _rms(cq, qn_ref[...]).astype(BF16)
    ckvn = _rms(ckv, kvn_ref[...]).astype(BF16)

    qa = jnp.dot(cqn, wuq_ref[...], preferred_element_type=F32)
    kn = jnp.dot(ckvn, wuk_ref[...], preferred_element_type=F32)
    vt = lax.dot_general(wvt_ref[...], ckvn, (((1,), (1,)), ((), ())),
                         preferred_element_type=F32)

    k_hi = rope(kp * table).astype(BF16)
    q_table = table * Q_SCALE
    for h in range(N_HEADS):
        blk = qa[:, h * QK_PAD:(h + 1) * QK_PAD]
        q_ref[0, h, :, :LANES] = (blk[:, :LANES] * Q_SCALE).astype(BF16)
        q_ref[0, h, :, LANES:] = rope(blk[:, LANES:] * q_table).astype(BF16)
        k_ref[0, h, :, :LANES] = kn[:, h * LANES:(h + 1) * LANES].astype(BF16)
        k_ref[0, h, :, LANES:] = k_hi
    ones = jnp.ones((V_PAD - V_DIM, TK), BF16)
    for j in range(TM // TK):
        for h in range(N_HEADS):
            vt_ref[0, j, h, :V_DIM, :] = vt[h * V_DIM:(h + 1) * V_DIM, j * TK:(j + 1) * TK].astype(BF16)
            vt_ref[0, j, h, V_DIM:, :] = ones


def _mla_proj(x, shift, scale, norm_w, table, w_lat_t, qn_w, kvn_w, w_uq2, w_uk, w_vt):
    b, s, d = x.shape
    tok = pl.BlockSpec((1, TM, d), lambda bi, i: (bi, i, 0))
    mod = pl.BlockSpec((1, 1, d), lambda bi, i: (bi, 0, 0))
    qk_spec = pl.BlockSpec((1, N_HEADS, TM, QK_PAD), lambda bi, i: (bi, 0, i, 0))
    return pl.pallas_call(
        _mla_kernel,
        grid=(b, s // TM),
        in_specs=[
            tok, mod, mod, _const_spec((1, d)),
            pl.BlockSpec((1, TM, LANES), lambda bi, i: (bi, i, 0)),
            _const_spec(w_lat_t.shape),
            _const_spec(qn_w.shape), _const_spec(kvn_w.shape),
            _const_spec(w_uq2.shape), _const_spec(w_uk.shape), _const_spec(w_vt.shape),
        ],
        out_specs=[
            qk_spec, qk_spec,
            pl.BlockSpec((1, TM // TK, N_HEADS, V_PAD, TK), lambda bi, i: (bi, i, 0, 0, 0)),
        ],
        out_shape=[
            jax.ShapeDtypeStruct((b, N_HEADS, s, QK_PAD), BF16),
            jax.ShapeDtypeStruct((b, N_HEADS, s, QK_PAD), BF16),
            jax.ShapeDtypeStruct((b, s // TK, N_HEADS, V_PAD, TK), BF16),
        ],
        compiler_params=pltpu.CompilerParams(
            dimension_semantics=("arbitrary", "arbitrary"), vmem_limit_bytes=VMEM_LIMIT),
        name="mla_proj",
    )(x, shift, scale, norm_w, table, w_lat_t, qn_w, kvn_w, w_uq2, w_uk, w_vt)


ATTN_BUFFERS = 3
ATTN_AHEAD = 2
ATTN_GROUP = 3
ATTN_HEADS = 2


def _attn_scores(q_ref, k_ref, bias_ref, hh, item, s_out, bm_out, diagonal):
    qn, jn = item
    k = k_ref[0, hh, pl.ds(pl.multiple_of(jn * TK, TK), TK), :]
    q = q_ref[0, hh, pl.ds(pl.multiple_of(qn * TQ, TQ), TQ), :]
    s = lax.dot_general(k, q, (((1,), (1,)), ((), ())), preferred_element_type=F32)
    if diagonal:
        s = s + bias_ref[...]
    s_out[...] = s
    bm_out[...] = jnp.max(s, axis=0, keepdims=True)


def _attn_accumulate(vt_ref, o_ref, acc_ref, hh, item, p_in, al_in, diagonal):
    qp, jp = item
    pv = jnp.dot(vt_ref[0, jp, hh], p_in[...], preferred_element_type=F32)
    acc = al_in[...] * acc_ref[hh, qp] + pv
    acc_ref[hh, qp] = acc
    if diagonal:
        rows = pl.ds(pl.multiple_of(qp * TQ, TQ), TQ)
        out = acc[:V_DIM] / acc[V_DIM:V_DIM + 1]
        o_ref[0, rows, hh * V_DIM:(hh + 1) * V_DIM] = jnp.transpose(out).astype(BF16)


def _attn_step(q_ref, k_ref, vt_ref, bias_ref, o_ref, acc_ref, m_ref, hh, cur, nxt, prv, items,
               diag):
    s_cur, p_cur, bm_cur, al_cur = cur
    s_nxt, _, bm_nxt, _ = nxt
    _, p_prv, _, al_prv = prv
    item_next, (qc, jc), item_prev = items
    diag_next, diag_prev = diag
    _attn_accumulate(vt_ref, o_ref, acc_ref, hh, item_prev, p_prv, al_prv, diag_prev)
    m_old = jnp.where(jc == 0, -jnp.inf, m_ref[hh, qc])
    m_new = jnp.maximum(m_old, bm_cur[...])
    p_cur[...] = jnp.exp2(s_cur[...] - m_new).astype(BF16)
    al_cur[...] = jnp.exp2(m_old - m_new)
    m_ref[hh, qc] = m_new
    _attn_scores(q_ref, k_ref, bias_ref, hh, item_next, s_nxt, bm_nxt, diag_next)


def _attn_item(t, n_lower):
    def lower(t):
        q = jnp.int32(1)
        for c in range(2, 64):
            if c * (c - 1) // 2 >= n_lower:
                break
            q = q + (t >= c * (c - 1) // 2).astype(jnp.int32)
        return q, t - lax.shift_right_logical(q * (q - 1), 1)
    ql, jl = lower(jnp.minimum(t, n_lower - 1))
    d = t - n_lower
    is_diag = t >= n_lower
    return jnp.where(is_diag, d, ql), jnp.where(is_diag, d, jl)


def _attn_kernel(q_ref, k_ref, vt_ref, bias_ref, o_ref, acc_ref, m_ref, *bufs):
    n_q = q_ref.shape[2] // TQ
    n_lower = n_q * (n_q - 1) // 2
    n_items = n_lower + n_q
    nbuf = ATTN_BUFFERS
    heads = range(ATTN_HEADS)
    assert TQ == TK and ATTN_GROUP % nbuf == 0 and n_items % nbuf == 0
    rings = tuple(tuple(bufs[4 * (hh * nbuf + r):4 * (hh * nbuf + r) + 4] for r in range(nbuf))
                  for hh in heads)

    def item(t):
        return _attn_item(jnp.clip(t, 0, n_items - 1), n_lower)

    def steps(t0, count, t0_static=None):
        for u in range(count):
            t = t0 + u
            diag = (False, False) if t0_static is None else (
                t0_static + u + ATTN_AHEAD >= n_lower, t0_static + u - 1 >= n_lower)
            items = (item(t + ATTN_AHEAD), item(t), item(t - 1))
            for hh in heads:
                ring = rings[hh]
                _attn_step(q_ref, k_ref, vt_ref, bias_ref, o_ref, acc_ref, m_ref, hh,
                           ring[u % nbuf], ring[(u + ATTN_AHEAD) % nbuf], ring[(u - 1) % nbuf],
                           items, diag)

    acc_ref[...] = jnp.zeros_like(acc_ref)
    m_ref[...] = jnp.zeros_like(m_ref)
    for hh in heads:
        _, p_last, _, al_last = rings[hh][nbuf - 1]
        p_last[...] = jnp.zeros_like(p_last)
        al_last[...] = jnp.ones_like(al_last)
        for a in range(ATTN_AHEAD):
            s_buf, _, bm_buf, _ = rings[hh][a]
            _attn_scores(q_ref, k_ref, bias_ref, hh, item(jnp.int32(a)), s_buf, bm_buf, False)

    n_plain = (n_lower - ATTN_AHEAD) // ATTN_GROUP
    lax.fori_loop(0, n_plain, lambda g, c: (steps(g * ATTN_GROUP, ATTN_GROUP), c)[1], 0)
    t_tail = n_plain * ATTN_GROUP
    steps(jnp.int32(t_tail), n_items - t_tail, t_tail)
    for hh in heads:
        _, p_last, _, al_last = rings[hh][(n_items - 1) % nbuf]
        _attn_accumulate(vt_ref, o_ref, acc_ref, hh, item(jnp.int32(n_items - 1)), p_last, al_last,
                         True)


def _attention(q, k, vt):
    b, h, s, _ = q.shape
    n_q = s // TQ
    row = lax.broadcasted_iota(jnp.int32, (TK, TQ), 0)
    col = lax.broadcasted_iota(jnp.int32, (TK, TQ), 1)
    bias = jnp.where(row <= col, 0.0, -jnp.inf).astype(F32)
    stat = pltpu.VMEM((1, TQ), F32)
    parity = [pltpu.VMEM((TK, TQ), F32), pltpu.VMEM((TK, TQ), BF16), stat, stat]
    hs = ATTN_HEADS
    return pl.pallas_call(
        _attn_kernel,
        grid=(b, h // hs),
        in_specs=[
            pl.BlockSpec((1, hs, s, QK_PAD), lambda bi, hi: (bi, hi, 0, 0)),
            pl.BlockSpec((1, hs, s, QK_PAD), lambda bi, hi: (bi, hi, 0, 0)),
            pl.BlockSpec((1, s // TK, hs, V_PAD, TK), lambda bi, hi: (bi, 0, hi, 0, 0)),
            _const_spec(bias.shape),
        ],
        out_specs=pl.BlockSpec((1, s, hs * V_DIM), lambda bi, hi: (bi, 0, hi)),
        out_shape=jax.ShapeDtypeStruct((b, s, h * V_DIM), BF16),
        scratch_shapes=[pltpu.VMEM((hs, n_q, V_PAD, TQ), F32), pltpu.VMEM((hs, n_q, 1, TQ), F32)]
        + parity * (ATTN_BUFFERS * hs),
        compiler_params=pltpu.CompilerParams(
            dimension_semantics=("arbitrary", "arbitrary"),
            vmem_limit_bytes=VMEM_LIMIT),
        name="attn",
    )(q, k, vt, bias)


def _out_kernel(cv_ref, sa_ref, o_ref, sb_ref, ga_ref, gb_ref, x_ref, gate_ref, lw_ref, lb_ref,
                wco_ref, wao_ref, wo_ref, fw_ref, y_ref):
    cv = cv_ref[0].astype(F32)
    mu = jnp.mean(cv, axis=-1, keepdims=True)
    xc = cv - mu
    var = jnp.mean(xc * xc, axis=-1, keepdims=True)
    ln = xc * lax.rsqrt(var + EPS) * lw_ref[...] + lb_ref[...]
    ug = (_silu(ln) * sa_ref[0].astype(F32)).astype(BF16)
    ya = jnp.dot(ug, wco_ref[...], preferred_element_type=F32)
    og = (o_ref[0].astype(F32) * sb_ref[0].astype(F32)).astype(BF16)
    yb = jnp.dot(og, wao_ref[...], preferred_element_type=F32)
    merged = ga_ref[0].astype(F32) * ya + gb_ref[0].astype(F32) * yb
    delta = jnp.dot(merged.astype(BF16), wo_ref[...], preferred_element_type=F32)
    y = x_ref[0] + gate_ref[0] * delta
    y_ref[0] = _rms(y, fw_ref[...])


def _out(cv, sa, o, sb, ga, gb, x, gate, ln_w, ln_b, w_co, w_ao, w_o, fw):
    b, s, d = x.shape
    tok = pl.BlockSpec((1, TM, d), lambda bi, i: (bi, i, 0))
    mod = pl.BlockSpec((1, 1, d), lambda bi, i: (bi, 0, 0))
    wspec = _const_spec((d, d))
    vec = _const_spec((1, d))
    return pl.pallas_call(
        _out_kernel,
        grid=(b, s // TM),
        in_specs=[tok, tok, tok, tok, tok, tok, tok, mod, vec, vec, wspec, wspec, wspec, vec],
        out_specs=tok,
        out_shape=jax.ShapeDtypeStruct((b, s, d), F32),
        compiler_params=pltpu.CompilerParams(
            dimension_semantics=("arbitrary", "arbitrary"), vmem_limit_bytes=VMEM_LIMIT),
        name="out",
    )(cv, sa, o, sb, ga, gb, x, gate, ln_w, ln_b, w_co, w_ao, w_o, fw)


def _rot_cols(w):
    half = w.shape[-1] // 2
    return jnp.concatenate([-w[..., half:], w[..., :half]], axis=-1)


def _layer(x, c_act_pad, table, w_ada, b_ada, norm_w, w_in, conv_w, conv_b, conv_ln_w,
           conv_ln_b, w_conv_out, q_norm_w, w_uq, kv_norm_w, w_ukv, w_attn_out, w_out, final_w):
    b, s, d = x.shape
    mod = _ada(c_act_pad, w_ada, b_ada[None, :])[:b]
    shift, scale, gate = (m[:, None, :] for m in jnp.split(mod, 3, axis=-1))

    assert w_in.shape[1] == IN_COLS
    w_in_t = w_in.T
    w_wide = _wprep(w_in_t)
    w_kpe_t = w_in_t[IN_COL_BGATE - QK_ROPE_DIM:IN_COL_BGATE]
    half = QK_ROPE_DIM // 2
    w_lat_t = jnp.concatenate([w_in_t[IN_COL_LATENT:IN_COL_BGATE], -w_kpe_t[half:], w_kpe_t[:half]],
                              axis=0).astype(BF16)

    wq = w_uq.reshape(Q_LORA_RANK, N_HEADS, QK_NOPE_DIM + QK_ROPE_DIM)
    wq_rope = wq[..., QK_NOPE_DIM:]
    w_uq2 = jnp.concatenate([wq, _rot_cols(wq_rope)], axis=-1)
    w_uq2 = w_uq2.reshape(Q_LORA_RANK, N_HEADS * QK_PAD).astype(BF16)
    wkv = w_ukv.reshape(KV_LORA_RANK, N_HEADS, QK_NOPE_DIM + V_DIM)
    w_uk = wkv[..., :QK_NOPE_DIM].reshape(KV_LORA_RANK, N_HEADS * QK_NOPE_DIM).astype(BF16)
    w_vt = wkv[..., QK_NOPE_DIM:].reshape(KV_LORA_RANK, ATTN_WIDTH).T.astype(BF16)

    nw = norm_w[None, :]
    cw = conv_w.reshape(CONV_KERNEL, CONV_WIDTH // CONV_LANES, 1, CONV_LANES).transpose(1, 0, 2, 3)
    cw = jnp.broadcast_to(cw, (CONV_WIDTH // CONV_LANES, CONV_KERNEL, SUBLANES, CONV_LANES))
    cv, sa, sb, ga, gb = _gateconv(x, shift, scale, nw, w_wide, cw, conv_b[None, :])
    q, k, vt = _mla_proj(x, shift, scale, nw, table, w_lat_t, q_norm_w[None, :],
                         kv_norm_w[None, :], w_uq2, w_uk, w_vt)
    o = _attention(q, k, vt)
    return _out(cv, sa, o, sb, ga, gb, x, gate, conv_ln_w[None, :], conv_ln_b[None, :],
                w_conv_out.astype(BF16), w_attn_out.astype(BF16), w_out.astype(BF16),
                final_w[None, :])


def kernel(x, c, positions, w_ada, b_ada, norm_w, w_in, conv_w, conv_b, conv_ln_w, conv_ln_b,
           w_conv_out, q_norm_w, w_uq, kv_norm_w, w_ukv, w_attn_out, w_out, final_norm_w):
    b, s, d = x.shape
    depth = w_ada.shape[0]
    assert depth == 1, "final rmsnorm is fused into the single layer's output kernel"
    inv_freq = ROPE_THETA ** (-jnp.arange(0, QK_ROPE_DIM, 2, dtype=F32) / QK_ROPE_DIM)
    n_freq = inv_freq.shape[0]
    assert ROPE_PACK * n_freq == LANES
    invf = jnp.tile(inv_freq, ROPE_PACK)[None, :]
    pos_dense = jnp.repeat(positions.astype(F32).reshape(b * s // ROPE_PACK, ROPE_PACK), n_freq, axis=1)
    table = _rope_table(pos_dense, invf).reshape(b, s, LANES)
    c_pad = jnp.pad(c, ((0, 8 - b), (0, 0)))
    return _layer(x, c_pad, table, w_ada[0], b_ada[0], norm_w[0], w_in[0], conv_w[0],
                  conv_b[0], conv_ln_w[0], conv_ln_b[0], w_conv_out[0], q_norm_w[0], w_uq[0],
                  kv_norm_w[0], w_ukv[0], w_attn_out[0], w_out[0], final_norm_w)
```

```python
import functools
import math

import jax
import jax.numpy as jnp
from jax import lax
from jax.experimental import pallas as pl
from jax.experimental.pallas import tpu as pltpu

F32 = jnp.float32
BF16 = jnp.bfloat16

D_MODEL = 1024
CONV_WIDTH = 1024
CONV_KERNEL = 31
N_HEADS = 8
QK_NOPE_DIM = 128
QK_ROPE_DIM = 64
V_DIM = 128
Q_LORA_RANK = 256
KV_LORA_RANK = 256
ATTN_WIDTH = N_HEADS * V_DIM
ROPE_THETA = 10000.0
EPS = 1e-6

LANES = 128
QK_PAD = 2 * LANES
BF16_SUBLANES = 16
V_PAD = V_DIM + BF16_SUBLANES
ROPE_PACK = LANES // (QK_ROPE_DIM // 2)
HALO = 32
VMEM_LIMIT = 56 * 1024 * 1024

Q_SCALE = (QK_NOPE_DIM + QK_ROPE_DIM) ** -0.5 * math.log2(math.e)

IN_COL_LATENT = 3 * CONV_WIDTH
IN_COL_BGATE = IN_COL_LATENT + Q_LORA_RANK + KV_LORA_RANK + QK_ROPE_DIM
IN_COLS = IN_COL_BGATE + ATTN_WIDTH + 2 * D_MODEL
G_VAL, G_GLU, G_AGATE, G_BGATE, G_GA, G_GB = range(6)
N_WIDE_GROUPS = 6
assert CONV_WIDTH == ATTN_WIDTH == D_MODEL

TM = 512
TC = 256
RC = 64
CONV_LANES = 256
PROJ_COLS = 1024
GC_STREAMS = 2
SUBLANES = 8
SHIFT_ROWS = TC + HALO - SUBLANES
SHIFT_BLOCK = 40
assert SHIFT_ROWS % SHIFT_BLOCK == 0 and SHIFT_BLOCK % SUBLANES == 0
TQ = 512
TK = 512


def _sigmoid(x):
    return 1.0 / (1.0 + jnp.exp(-x))


def _silu(x):
    return x * _sigmoid(x)


def _rms(x, w):
    return x * lax.rsqrt(jnp.mean(x * x, axis=-1, keepdims=True) + EPS) * w


def _const_spec(shape):
    return pl.BlockSpec(shape, lambda *_: (0,) * len(shape), pipeline_mode=pl.Buffered(1))


def _ada_kernel(c_ref, w_ref, b_ref, o_ref):
    c = c_ref[...]
    o_ref[...] = jnp.dot(_silu(c), w_ref[...], preferred_element_type=F32) + b_ref[...]


def _ada(c_pad, w_ada, b_ada):
    rows, d = c_pad.shape
    n = w_ada.shape[1]
    tn = 512
    return pl.pallas_call(
        _ada_kernel,
        grid=(n // tn,),
        in_specs=[
            pl.BlockSpec((rows, d), lambda j: (0, 0)),
            pl.BlockSpec((d, tn), lambda j: (0, j)),
            pl.BlockSpec((1, tn), lambda j: (0, j)),
        ],
        out_specs=pl.BlockSpec((rows, tn), lambda j: (0, j)),
        out_shape=jax.ShapeDtypeStruct((rows, n), F32),
        compiler_params=pltpu.CompilerParams(dimension_semantics=("arbitrary",)),
        name="ada",
    )(c_pad, w_ada, b_ada)


def _modulated_norm(x_ref, shift_ref, scale_ref, nw_ref):
    x = x_ref[0]
    h = _rms(x, nw_ref[...]) * (1.0 + scale_ref[0]) + shift_ref[0]
    return h.astype(BF16)


def _wprep_kernel(wt_ref, out_ref):
    blk = wt_ref[...].astype(BF16)
    n = blk.shape[1]
    eye = (lax.broadcasted_iota(jnp.int32, (n, n), 0)
           == lax.broadcasted_iota(jnp.int32, (n, n), 1)).astype(BF16)
    out_ref[...] = lax.dot_general(eye, blk, (((1,), (1,)), ((), ())),
                                   preferred_element_type=F32).astype(BF16)


def _wprep(w_in_t):
    d = w_in_t.shape[1]

    def first_row(g):
        row = jnp.where(g < G_BGATE, g * d, IN_COL_BGATE + (g - G_BGATE) * d)
        return pl.multiple_of(row, math.gcd(d, IN_COL_BGATE))

    return pl.pallas_call(
        _wprep_kernel,
        grid=(N_WIDE_GROUPS,),
        in_specs=[pl.BlockSpec((pl.Element(d), pl.Element(d)), lambda g: (first_row(g), 0))],
        out_specs=pl.BlockSpec((d, d), lambda g: (0, g)),
        out_shape=jax.ShapeDtypeStruct((d, N_WIDE_GROUPS * d), BF16),
        compiler_params=pltpu.CompilerParams(dimension_semantics=("arbitrary",)),
        name="wprep",
    )(w_in_t)


def _gateconv_kernel(x_ref, shift_ref, scale_ref, nw_ref, w_ref, cw_ref, cb_ref,
                     cv_ref, sa_ref, sb_ref, ga_ref, gb_ref, sh_ref, hb_ref):
    k = pl.program_id(1)
    n_lc = CONV_WIDTH // CONV_LANES
    first = HALO - (CONV_KERNEL - 1)
    groups = RC // SUBLANES
    streams = range(GC_STREAMS)

    @pl.when(k == 0)
    def _():
        for st in streams:
            for lc in range(n_lc):
                sh_ref[st, lc, 0, 0:HALO, :] = jnp.zeros((HALO, CONV_LANES), F32)

    @pl.when(k > 0)
    def _():
        for st in streams:
            for lc in range(n_lc):
                sh_ref[st, lc, 0, 0:HALO, :] = sh_ref[st, lc, 0, TC:TC + HALO, :]

    def proj(st, group, piece):
        col = group * D_MODEL + piece * PROJ_COLS
        return jnp.dot(hb_ref[st], w_ref[:, col:col + PROJ_COLS], preferred_element_type=F32)

    for st in streams:
        h = _rms(x_ref[st], nw_ref[...]) * (1.0 + scale_ref[st]) + shift_ref[st]
        hb_ref[st] = h.astype(BF16)
        for piece in range(D_MODEL // PROJ_COLS):
            u = proj(st, G_VAL, piece) * _sigmoid(proj(st, G_GLU, piece))
            for j in range(PROJ_COLS // CONV_LANES):
                lc = piece * (PROJ_COLS // CONV_LANES) + j
                sh_ref[st, lc, 0, HALO:, :] = u[:, j * CONV_LANES:(j + 1) * CONV_LANES]

    def gate_piece(st, group, out_ref, act, piece):
        cols = slice(piece * PROJ_COLS, (piece + 1) * PROJ_COLS)
        out_ref[st, :, cols] = act(proj(st, group, piece)).astype(BF16)

    def shift_chunk(st, lc):
        for r in range(1, SUBLANES):
            for rb in range(0, SHIFT_ROWS, SHIFT_BLOCK):
                sh_ref[st, lc, r, rb:rb + SHIFT_BLOCK, :] = (
                    sh_ref[st, lc, 0, rb + r:rb + r + SHIFT_BLOCK, :])

    def conv_chunk(st, lc, c):
        row0 = c * RC
        lanes = slice(lc * CONV_LANES, (lc + 1) * CONV_LANES)
        acc = jnp.zeros((groups, SUBLANES, CONV_LANES), F32)
        for r in range(SUBLANES):
            offs = [o for o in range(first, first + CONV_KERNEL) if o % SUBLANES == r]
            g0, g1 = offs[0] // SUBLANES, offs[-1] // SUBLANES
            span = groups + g1 - g0
            start = row0 + g0 * SUBLANES
            xs = sh_ref[st, lc, r, start:start + span * SUBLANES, :]
            xs = xs.reshape(span, SUBLANES, CONV_LANES)
            for o in offs:
                g = o // SUBLANES - g0
                acc = acc + cw_ref[lc, o - first][None] * xs[g:g + groups]
        cv = acc.reshape(RC, CONV_LANES) + cb_ref[:, lanes]
        cv_ref[st, row0:row0 + RC, lanes] = cv.astype(BF16)

    def stream_work(st):
        mxu_work, vpu_work, text = [], [], []
        for group, out_ref, act in ((G_AGATE, sa_ref, _silu), (G_BGATE, sb_ref, _silu),
                                    (G_GA, ga_ref, _sigmoid), (G_GB, gb_ref, _sigmoid)):
            mxu_work += [functools.partial(gate_piece, st, group, out_ref, act, p)
                         for p in range(D_MODEL // PROJ_COLS)]
        for lc in range(n_lc):
            vpu_work.append(functools.partial(shift_chunk, st, lc))
            vpu_work += [functools.partial(conv_chunk, st, lc, c) for c in range(TC // RC)]
        issued = 0
        for i, mxu_fn in enumerate(mxu_work):
            target = (len(vpu_work) * (i + 1)) // len(mxu_work)
            text.append([mxu_fn] + vpu_work[issued:target])
            issued = target
        return text

    @pl.when(k >= 0)
    def _():
        for parts in zip(*[stream_work(st) for st in streams]):
            for part in parts:
                for fn in part:
                    fn()


def _gateconv(x, shift, scale, norm_w, w_wide, conv_w, conv_b):
    b, s, d = x.shape
    n_lc = d // CONV_LANES
    gs = GC_STREAMS
    tok = pl.BlockSpec((gs, TC, d), lambda bi, k: (bi, k, 0))
    mod = pl.BlockSpec((gs, 1, d), lambda bi, k: (bi, 0, 0))
    out = jax.ShapeDtypeStruct((b, s, d), BF16)
    return pl.pallas_call(
        _gateconv_kernel,
        grid=(b // gs, s // TC),
        in_specs=[tok, mod, mod, _const_spec((1, d)), _const_spec(w_wide.shape),
                  _const_spec(conv_w.shape), _const_spec((1, d))],
        out_specs=[tok] * 5,
        out_shape=[out] * 5,
        scratch_shapes=[pltpu.VMEM((gs, n_lc, SUBLANES, HALO + TC, CONV_LANES), F32),
                        pltpu.VMEM((gs, TC, d), BF16)],
        compiler_params=pltpu.CompilerParams(
            dimension_semantics=("arbitrary", "arbitrary"), vmem_limit_bytes=VMEM_LIMIT),
        name="gateconv",
    )(x, shift, scale, norm_w, w_wide, conv_w, conv_b)


def _rope_kernel(pos_ref, invf_ref, table_ref):
    ang = pos_ref[...] * invf_ref[...]
    cos_d, sin_d = jnp.cos(ang), jnp.sin(ang)
    n_freq = LANES // ROPE_PACK
    quarter = lax.broadcasted_iota(jnp.int32, ang.shape, 1) // n_freq
    for p in range(ROPE_PACK):
        def placed(src, q):
            return pltpu.roll(src, ((q - p) * n_freq) % LANES, 1)
        row = jnp.where(quarter == 0, placed(cos_d, 0),
                        jnp.where(quarter == 1, placed(cos_d, 1),
                                  jnp.where(quarter == 2, placed(sin_d, 2), placed(sin_d, 3))))
        table_ref[pl.ds(p, ang.shape[0], stride=ROPE_PACK), :] = row


def _rope_table(pos_dense, invf):
    rows = pos_dense.shape[0]
    tr = 512
    return pl.pallas_call(
        _rope_kernel,
        grid=(rows // tr,),
        in_specs=[pl.BlockSpec((tr, LANES), lambda i: (i, 0)), _const_spec(invf.shape)],
        out_specs=pl.BlockSpec((ROPE_PACK * tr, LANES), lambda i: (i, 0)),
        out_shape=jax.ShapeDtypeStruct((ROPE_PACK * rows, LANES), F32),
        compiler_params=pltpu.CompilerParams(dimension_semantics=("arbitrary",)),
        name="rope",
    )(pos_dense, invf)


def _mla_kernel(x_ref, shift_ref, scale_ref, nw_ref, table_ref, w_ref,
                qn_ref, kvn_ref, wuq_ref, wuk_ref, wvt_ref,
                q_ref, k_ref, vt_ref):
    hb = _modulated_norm(x_ref, shift_ref, scale_ref, nw_ref)
    lat = lax.dot_general(hb, w_ref[...], (((1,), (1,)), ((), ())),
                          preferred_element_type=F32)
    cq = lat[:, :Q_LORA_RANK]
    ckv = lat[:, Q_LORA_RANK:Q_LORA_RANK + KV_LORA_RANK]
    kp = lat[:, Q_LORA_RANK + KV_LORA_RANK:]

    table = table_ref[0]
    lane = lax.broadcasted_iota(jnp.int32, table.shape, 1)
    first_half = lane < QK_ROPE_DIM

    def rope(t):
        return jnp.where(first_half, t + pltpu.roll(t, QK_ROPE_DIM, 1), 0.0)

    cqn = _rms(cq, qn_ref[...]).astype(BF16)
    ckvn = _rms(ckv, kvn_ref[...]).astype(BF16)

    qa = jnp.dot(cqn, wuq_ref[...], preferred_element_type=F32)
    kn = jnp.dot(ckvn, wuk_ref[...], preferred_element_type=F32)
    vt = lax.dot_general(wvt_ref[...], ckvn, (((1,), (1,)), ((), ())),
                         preferred_element_type=F32)

    k_hi = rope(kp * table).astype(BF16)
    q_table = table * Q_SCALE
    for h in range(N_HEADS):
        blk = qa[:, h * QK_PAD:(h + 1) * QK_PAD]
        q_ref[0, h, :, :LANES] = (blk[:, :LANES] * Q_SCALE).astype(BF16)
        q_ref[0, h, :, LANES:] = rope(blk[:, LANES:] * q_table).astype(BF16)
        k_ref[0, h, :, :LANES] = kn[:, h * LANES:(h + 1) * LANES].astype(BF16)
        k_ref[0, h, :, LANES:] = k_hi
    ones = jnp.ones((V_PAD - V_DIM, TK), BF16)
    for j in range(TM // TK):
        for h in range(N_HEADS):
            vt_ref[0, j, h, :V_DIM, :] = vt[h * V_DIM:(h + 1) * V_DIM, j * TK:(j + 1) * TK].astype(BF16)
            vt_ref[0, j, h, V_DIM:, :] = ones


def _mla_proj(x, shift, scale, norm_w, table, w_lat_t, qn_w, kvn_w, w_uq2, w_uk, w_vt):
    b, s, d = x.shape
    tok = pl.BlockSpec((1, TM, d), lambda bi, i: (bi, i, 0))
    mod = pl.BlockSpec((1, 1, d), lambda bi, i: (bi, 0, 0))
    qk_spec = pl.BlockSpec((1, N_HEADS, TM, QK_PAD), lambda bi, i: (bi, 0, i, 0))
    return pl.pallas_call(
        _mla_kernel,
        grid=(b, s // TM),
        in_specs=[
            tok, mod, mod, _const_spec((1, d)),
            pl.BlockSpec((1, TM, LANES), lambda bi, i: (bi, i, 0)),
            _const_spec(w_lat_t.shape),
            _const_spec(qn_w.shape), _const_spec(kvn_w.shape),
            _const_spec(w_uq2.shape), _const_spec(w_uk.shape), _const_spec(w_vt.shape),
        ],
        out_specs=[
            qk_spec, qk_spec,
            pl.BlockSpec((1, TM // TK, N_HEADS, V_PAD, TK), lambda bi, i: (bi, i, 0, 0, 0)),
        ],
        out_shape=[
            jax.ShapeDtypeStruct((b, N_HEADS, s, QK_PAD), BF16),
            jax.ShapeDtypeStruct((b, N_HEADS, s, QK_PAD), BF16),
            jax.ShapeDtypeStruct((b, s // TK, N_HEADS, V_PAD, TK), BF16),
        ],
        compiler_params=pltpu.CompilerParams(
            dimension_semantics=("arbitrary", "arbitrary"), vmem_limit_bytes=VMEM_LIMIT),
        name="mla_proj",
    )(x, shift, scale, norm_w, table, w_lat_t, qn_w, kvn_w, w_uq2, w_uk, w_vt)


ATTN_BUFFERS = 3
ATTN_AHEAD = 2
ATTN_GROUP = 3
ATTN_HEADS = 2


def _attn_scores(q_ref, k_ref, bias_ref, hh, item, s_out, bm_out, diagonal):
    qn, jn = item
    k = k_ref[0, hh, pl.ds(pl.multiple_of(jn * TK, TK), TK), :]
    q = q_ref[0, hh, pl.ds(pl.multiple_of(qn * TQ, TQ), TQ), :]
    s = lax.dot_general(k, q, (((1,), (1,)), ((), ())), preferred_element_type=F32)
    if diagonal:
        s = s + bias_ref[...]
    s_out[...] = s
    bm_out[...] = jnp.max(s, axis=0, keepdims=True)


def _attn_accumulate(vt_ref, o_ref, acc_ref, hh, item, p_in, al_in, diagonal):
    qp, jp = item
    pv = jnp.dot(vt_ref[0, jp, hh], p_in[...], preferred_element_type=F32)
    acc = al_in[...] * acc_ref[hh, qp] + pv
    acc_ref[hh, qp] = acc
    if diagonal:
        rows = pl.ds(pl.multiple_of(qp * TQ, TQ), TQ)
        out = acc[:V_DIM] / acc[V_DIM:V_DIM + 1]
        o_ref[0, rows, hh * V_DIM:(hh + 1) * V_DIM] = jnp.transpose(out).astype(BF16)


def _attn_step(q_ref, k_ref, vt_ref, bias_ref, o_ref, acc_ref, m_ref, hh, cur, nxt, prv, items,
               diag):
    s_cur, p_cur, bm_cur, al_cur = cur
    s_nxt, _, bm_nxt, _ = nxt
    _, p_prv, _, al_prv = prv
    item_next, (qc, jc), item_prev = items
    diag_next, diag_prev = diag
    _attn_accumulate(vt_ref, o_ref, acc_ref, hh, item_prev, p_prv, al_prv, diag_prev)
    m_old = jnp.where(jc == 0, -jnp.inf, m_ref[hh, qc])
    m_new = jnp.maximum(m_old, bm_cur[...])
    p_cur[...] = jnp.exp2(s_cur[...] - m_new).astype(BF16)
    al_cur[...] = jnp.exp2(m_old - m_new)
    m_ref[hh, qc] = m_new
    _attn_scores(q_ref, k_ref, bias_ref, hh, item_next, s_nxt, bm_nxt, diag_next)


def _attn_item(t, n_lower):
    def lower(t):
        q = jnp.int32(1)
        for c in range(2, 64):
            if c * (c - 1) // 2 >= n_lower:
                break
            q = q + (t >= c * (c - 1) // 2).astype(jnp.int32)
        return q, t - lax.shift_right_logical(q * (q - 1), 1)
    ql, jl = lower(jnp.minimum(t, n_lower - 1))
    d = t - n_lower
    is_diag = t >= n_lower
    return jnp.where(is_diag, d, ql), jnp.where(is_diag, d, jl)


def _attn_kernel(q_ref, k_ref, vt_ref, bias_ref, o_ref, acc_ref, m_ref, *bufs):
    n_q = q_ref.shape[2] // TQ
    n_lower = n_q * (n_q - 1) // 2
    n_items = n_lower + n_q
    nbuf = ATTN_BUFFERS
    heads = range(ATTN_HEADS)
    assert TQ == TK and ATTN_GROUP % nbuf == 0 and n_items % nbuf == 0
    rings = tuple(tuple(bufs[4 * (hh * nbuf + r):4 * (hh * nbuf + r) + 4] for r in range(nbuf))
                  for hh in heads)

    def item(t):
        return _attn_item(jnp.clip(t, 0, n_items - 1), n_lower)

    def steps(t0, count, t0_static=None):
        for u in range(count):
            t = t0 + u
            diag = (False, False) if t0_static is None else (
                t0_static + u + ATTN_AHEAD >= n_lower, t0_static + u - 1 >= n_lower)
            items = (item(t + ATTN_AHEAD), item(t), item(t - 1))
            for hh in heads:
                ring = rings[hh]
                _attn_step(q_ref, k_ref, vt_ref, bias_ref, o_ref, acc_ref, m_ref, hh,
                           ring[u % nbuf], ring[(u + ATTN_AHEAD) % nbuf], ring[(u - 1) % nbuf],
                           items, diag)

    acc_ref[...] = jnp.zeros_like(acc_ref)
    m_ref[...] = jnp.zeros_like(m_ref)
    for hh in heads:
        _, p_last, _, al_last = rings[hh][nbuf - 1]
        p_last[...] = jnp.zeros_like(p_last)
        al_last[...] = jnp.ones_like(al_last)
        for a in range(ATTN_AHEAD):
            s_buf, _, bm_buf, _ = rings[hh][a]
            _attn_scores(q_ref, k_ref, bias_ref, hh, item(jnp.int32(a)), s_buf, bm_buf, False)

    n_plain = (n_lower - ATTN_AHEAD) // ATTN_GROUP
    lax.fori_loop(0, n_plain, lambda g, c: (steps(g * ATTN_GROUP, ATTN_GROUP), c)[1], 0)
    t_tail = n_plain * ATTN_GROUP
    steps(jnp.int32(t_tail), n_items - t_tail, t_tail)
    for hh in heads:
        _, p_last, _, al_last = rings[hh][(n_items - 1) % nbuf]
        _attn_accumulate(vt_ref, o_ref, acc_ref, hh, item(jnp.int32(n_items - 1)), p_last, al_last,
                         True)


def _attention(q, k, vt):
    b, h, s, _ = q.shape
    n_q = s // TQ
    row = lax.broadcasted_iota(jnp.int32, (TK, TQ), 0)
    col = lax.broadcasted_iota(jnp.int32, (TK, TQ), 1)
    bias = jnp.where(row <= col, 0.0, -jnp.inf).astype(F32)
    stat = pltpu.VMEM((1, TQ), F32)
    parity = [pltpu.VMEM((TK, TQ), F32), pltpu.VMEM((TK, TQ), BF16), stat, stat]
    hs = ATTN_HEADS
    return pl.pallas_call(
        _attn_kernel,
        grid=(b, h // hs),
        in_specs=[
            pl.BlockSpec((1, hs, s, QK_PAD), lambda bi, hi: (bi, hi, 0, 0)),
            pl.BlockSpec((1, hs, s, QK_PAD), lambda bi, hi: (bi, hi, 0, 0)),
            pl.BlockSpec((1, s // TK, hs, V_PAD, TK), lambda bi, hi: (bi, 0, hi, 0, 0)),
            _const_spec(bias.shape),
        ],
        out_specs=pl.BlockSpec((1, s, hs * V_DIM), lambda bi, hi: (bi, 0, hi)),
        out_shape=jax.ShapeDtypeStruct((b, s, h * V_DIM), BF16),
        scratch_shapes=[pltpu.VMEM((hs, n_q, V_PAD, TQ), F32), pltpu.VMEM((hs, n_q, 1, TQ), F32)]
        + parity * (ATTN_BUFFERS * hs),
        compiler_params=pltpu.CompilerParams(
            dimension_semantics=("arbitrary", "arbitrary"),
            vmem_limit_bytes=VMEM_LIMIT),
        name="attn",
    )(q, k, vt, bias)


def _out_kernel(cv_ref, sa_ref, o_ref, sb_ref, ga_ref, gb_ref, x_ref, gate_ref, lw_ref, lb_ref,
                wco_ref, wao_ref, wo_ref, fw_ref, y_ref):
    cv = cv_ref[0].astype(F32)
    mu = jnp.mean(cv, axis=-1, keepdims=True)
    xc = cv - mu
    var = jnp.mean(xc * xc, axis=-1, keepdims=True)
    ln = xc * lax.rsqrt(var + EPS) * lw_ref[...] + lb_ref[...]
    ug = (_silu(ln) * sa_ref[0].astype(F32)).astype(BF16)
    ya = jnp.dot(ug, wco_ref[...], preferred_element_type=F32)
    og = (o_ref[0].astype(F32) * sb_ref[0].astype(F32)).astype(BF16)
    yb = jnp.dot(og, wao_ref[...], preferred_element_type=F32)
    merged = ga_ref[0].astype(F32) * ya + gb_ref[0].astype(F32) * yb
    delta = jnp.dot(merged.astype(BF16), wo_ref[...], preferred_element_type=F32)
    y = x_ref[0] + gate_ref[0] * delta
    y_ref[0] = _rms(y, fw_ref[...])


def _out(cv, sa, o, sb, ga, gb, x, gate, ln_w, ln_b, w_co, w_ao, w_o, fw):
    b, s, d = x.shape
    tok = pl.BlockSpec((1, TM, d), lambda bi, i: (bi, i, 0))
    mod = pl.BlockSpec((1, 1, d), lambda bi, i: (bi, 0, 0))
    wspec = _const_spec((d, d))
    vec = _const_spec((1, d))
    return pl.pallas_call(
        _out_kernel,
        grid=(b, s // TM),
        in_specs=[tok, tok, tok, tok, tok, tok, tok, mod, vec, vec, wspec, wspec, wspec, vec],
        out_specs=tok,
        out_shape=jax.ShapeDtypeStruct((b, s, d), F32),
        compiler_params=pltpu.CompilerParams(
            dimension_semantics=("arbitrary", "arbitrary"), vmem_limit_bytes=VMEM_LIMIT),
        name="out",
    )(cv, sa, o, sb, ga, gb, x, gate, ln_w, ln_b, w_co, w_ao, w_o, fw)


def _rot_cols(w):
    half = w.shape[-1] // 2
    return jnp.concatenate([-w[..., half:], w[..., :half]], axis=-1)


def _layer(x, c_act_pad, table, w_ada, b_ada, norm_w, w_in, conv_w, conv_b, conv_ln_w,
           conv_ln_b, w_conv_out, q_norm_w, w_uq, kv_norm_w, w_ukv, w_attn_out, w_out, final_w):
    b, s, d = x.shape
    mod = _ada(c_act_pad, w_ada, b_ada[None, :])[:b]
    shift, scale, gate = (m[:, None, :] for m in jnp.split(mod, 3, axis=-1))

    assert w_in.shape[1] == IN_COLS
    w_in_t = w_in.T
    w_wide = _wprep(w_in_t)
    w_kpe_t = w_in_t[IN_COL_BGATE - QK_ROPE_DIM:IN_COL_BGATE]
    half = QK_ROPE_DIM // 2
    w_lat_t = jnp.concatenate([w_in_t[IN_COL_LATENT:IN_COL_BGATE], -w_kpe_t[half:], w_kpe_t[:half]],
                              axis=0).astype(BF16)

    wq = w_uq.reshape(Q_LORA_RANK, N_HEADS, QK_NOPE_DIM + QK_ROPE_DIM)
    wq_rope = wq[..., QK_NOPE_DIM:]
    w_uq2 = jnp.concatenate([wq, _rot_cols(wq_rope)], axis=-1)
    w_uq2 = w_uq2.reshape(Q_LORA_RANK, N_HEADS * QK_PAD).astype(BF16)
    wkv = w_ukv.reshape(KV_LORA_RANK, N_HEADS, QK_NOPE_DIM + V_DIM)
    w_uk = wkv[..., :QK_NOPE_DIM].reshape(KV_LORA_RANK, N_HEADS * QK_NOPE_DIM).astype(BF16)
    w_vt = wkv[..., QK_NOPE_DIM:].reshape(KV_LORA_RANK, ATTN_WIDTH).T.astype(BF16)

    nw = norm_w[None, :]
    cw = conv_w.reshape(CONV_KERNEL, CONV_WIDTH // CONV_LANES, 1, CONV_LANES).transpose(1, 0, 2, 3)
    cw = jnp.broadcast_to(cw, (CONV_WIDTH // CONV_LANES, CONV_KERNEL, SUBLANES, CONV_LANES))
    cv, sa, sb, ga, gb = _gateconv(x, shift, scale, nw, w_wide, cw, conv_b[None, :])
    q, k, vt = _mla_proj(x, shift, scale, nw, table, w_lat_t, q_norm_w[None, :],
                         kv_norm_w[None, :], w_uq2, w_uk, w_vt)
    o = _attention(q, k, vt)
    return _out(cv, sa, o, sb, ga, gb, x, gate, conv_ln_w[None, :], conv_ln_b[None, :],
                w_conv_out.astype(BF16), w_attn_out.astype(BF16), w_out.astype(BF16),
                final_w[None, :])


def kernel(x, c, positions, w_ada, b_ada, norm_w, w_in, conv_w, conv_b, conv_ln_w, conv_ln_b,
           w_conv_out, q_norm_w, w_uq, kv_norm_w, w_ukv, w_attn_out, w_out, final_norm_w):
    b, s, d = x.shape
    depth = w_ada.shape[0]
    assert depth == 1, "final rmsnorm is fused into the single layer's output kernel"
    inv_freq = ROPE_THETA ** (-jnp.arange(0, QK_ROPE_DIM, 2, dtype=F32) / QK_ROPE_DIM)
    n_freq = inv_freq.shape[0]
    assert ROPE_PACK * n_freq == LANES
    invf = jnp.tile(inv_freq, ROPE_PACK)[None, :]
    pos_dense = jnp.repeat(positions.astype(F32).reshape(b * s // ROPE_PACK, ROPE_PACK), n_freq, axis=1)
    table = _rope_table(pos_dense, invf).reshape(b, s, LANES)
    c_pad = jnp.pad(c, ((0, 8 - b), (0, 0)))
    return _layer(x, c_pad, table, w_ada[0], b_ada[0], norm_w[0], w_in[0], conv_w[0],
                  conv_b[0], conv_ln_w[0], conv_ln_b[0], w_conv_out[0], q_norm_w[0], w_uq[0],
                  kv_norm_w[0], w_ukv[0], w_attn_out[0], w_out[0], final_norm_w)
```

```python
import functools
import math

import jax
import jax.numpy as jnp
from jax import lax
from jax.experimental import pallas as pl
from jax.experimental.pallas import tpu as pltpu

F32 = jnp.float32
BF16 = jnp.bfloat16

D_MODEL = 1024
CONV_WIDTH = 1024
CONV_KERNEL = 31
N_HEADS = 8
QK_NOPE_DIM = 128
QK_ROPE_DIM = 64
V_DIM = 128
Q_LORA_RANK = 256
KV_LORA_RANK = 256
ATTN_WIDTH = N_HEADS * V_DIM
ROPE_THETA = 10000.0
EPS = 1e-6

LANES = 128
QK_PAD = 2 * LANES
BF16_SUBLANES = 16
V_PAD = V_DIM + BF16_SUBLANES
ROPE_PACK = LANES // (QK_ROPE_DIM // 2)
HALO = 32
VMEM_LIMIT = 56 * 1024 * 1024

Q_SCALE = (QK_NOPE_DIM + QK_ROPE_DIM) ** -0.5 * math.log2(math.e)

IN_COL_LATENT = 3 * CONV_WIDTH
IN_COL_BGATE = IN_COL_LATENT + Q_LORA_RANK + KV_LORA_RANK + QK_ROPE_DIM
IN_COLS = IN_COL_BGATE + ATTN_WIDTH + 2 * D_MODEL
G_VAL, G_GLU, G_AGATE, G_BGATE, G_GA, G_GB = range(6)
N_WIDE_GROUPS = 6
assert CONV_WIDTH == ATTN_WIDTH == D_MODEL

TM = 512
TC = 256
RC = 64
CONV_LANES = 256
PROJ_COLS = 1024
GC_STREAMS = 2
TO = 256
OUT_STREAMS = 2
SUBLANES = 8
SHIFT_ROWS = TC + HALO - SUBLANES
SHIFT_BLOCK = 40
assert SHIFT_ROWS % SHIFT_BLOCK == 0 and SHIFT_BLOCK % SUBLANES == 0
TQ = 512
TK = 512


def _sigmoid(x):
    return 1.0 / (1.0 + jnp.exp(-x))


def _silu(x):
    return x * _sigmoid(x)


def _rms(x, w):
    return x * lax.rsqrt(jnp.mean(x * x, axis=-1, keepdims=True) + EPS) * w


def _const_spec(shape):
    return pl.BlockSpec(shape, lambda *_: (0,) * len(shape), pipeline_mode=pl.Buffered(1))


def _ada_kernel(c_ref, w_ref, b_ref, o_ref):
    c = c_ref[...]
    o_ref[...] = jnp.dot(_silu(c), w_ref[...], preferred_element_type=F32) + b_ref[...]


def _ada(c_pad, w_ada, b_ada):
    rows, d = c_pad.shape
    n = w_ada.shape[1]
    tn = 512
    return pl.pallas_call(
        _ada_kernel,
        grid=(n // tn,),
        in_specs=[
            pl.BlockSpec((rows, d), lambda j: (0, 0)),
            pl.BlockSpec((d, tn), lambda j: (0, j)),
            pl.BlockSpec((1, tn), lambda j: (0, j)),
        ],
        out_specs=pl.BlockSpec((rows, tn), lambda j: (0, j)),
        out_shape=jax.ShapeDtypeStruct((rows, n), F32),
        compiler_params=pltpu.CompilerParams(dimension_semantics=("arbitrary",)),
        name="ada",
    )(c_pad, w_ada, b_ada)


def _modulated_norm(x_ref, shift_ref, scale_ref, nw_ref):
    x = x_ref[0]
    h = _rms(x, nw_ref[...]) * (1.0 + scale_ref[0]) + shift_ref[0]
    return h.astype(BF16)


def _wprep_kernel(wt_ref, out_ref):
    blk = wt_ref[...].astype(BF16)
    n = blk.shape[1]
    eye = (lax.broadcasted_iota(jnp.int32, (n, n), 0)
           == lax.broadcasted_iota(jnp.int32, (n, n), 1)).astype(BF16)
    out_ref[...] = lax.dot_general(eye, blk, (((1,), (1,)), ((), ())),
                                   preferred_element_type=F32).astype(BF16)


def _wprep(w_in_t):
    d = w_in_t.shape[1]

    def first_row(g):
        row = jnp.where(g < G_BGATE, g * d, IN_COL_BGATE + (g - G_BGATE) * d)
        return pl.multiple_of(row, math.gcd(d, IN_COL_BGATE))

    return pl.pallas_call(
        _wprep_kernel,
        grid=(N_WIDE_GROUPS,),
        in_specs=[pl.BlockSpec((pl.Element(d), pl.Element(d)), lambda g: (first_row(g), 0))],
        out_specs=pl.BlockSpec((d, d), lambda g: (0, g)),
        out_shape=jax.ShapeDtypeStruct((d, N_WIDE_GROUPS * d), BF16),
        compiler_params=pltpu.CompilerParams(dimension_semantics=("arbitrary",)),
        name="wprep",
    )(w_in_t)


def _gateconv_kernel(x_ref, shift_ref, scale_ref, nw_ref, w_ref, cw_ref, cb_ref,
                     cv_ref, sa_ref, sb_ref, ga_ref, gb_ref, sh_ref, hb_ref):
    k = pl.program_id(1)
    n_lc = CONV_WIDTH // CONV_LANES
    first = HALO - (CONV_KERNEL - 1)
    groups = RC // SUBLANES
    streams = range(GC_STREAMS)

    @pl.when(k == 0)
    def _():
        for st in streams:
            for lc in range(n_lc):
                sh_ref[st, lc, 0, 0:HALO, :] = jnp.zeros((HALO, CONV_LANES), F32)

    @pl.when(k > 0)
    def _():
        for st in streams:
            for lc in range(n_lc):
                sh_ref[st, lc, 0, 0:HALO, :] = sh_ref[st, lc, 0, TC:TC + HALO, :]

    def proj(st, group, piece):
        col = group * D_MODEL + piece * PROJ_COLS
        return jnp.dot(hb_ref[st], w_ref[:, col:col + PROJ_COLS], preferred_element_type=F32)

    for st in streams:
        h = _rms(x_ref[st], nw_ref[...]) * (1.0 + scale_ref[st]) + shift_ref[st]
        hb_ref[st] = h.astype(BF16)
        for piece in range(D_MODEL // PROJ_COLS):
            u = proj(st, G_VAL, piece) * _sigmoid(proj(st, G_GLU, piece))
            for j in range(PROJ_COLS // CONV_LANES):
                lc = piece * (PROJ_COLS // CONV_LANES) + j
                sh_ref[st, lc, 0, HALO:, :] = u[:, j * CONV_LANES:(j + 1) * CONV_LANES]

    def gate_piece(st, group, out_ref, act, piece):
        cols = slice(piece * PROJ_COLS, (piece + 1) * PROJ_COLS)
        out_ref[st, :, cols] = act(proj(st, group, piece)).astype(BF16)

    def shift_chunk(st, lc):
        for r in range(1, SUBLANES):
            for rb in range(0, SHIFT_ROWS, SHIFT_BLOCK):
                sh_ref[st, lc, r, rb:rb + SHIFT_BLOCK, :] = (
                    sh_ref[st, lc, 0, rb + r:rb + r + SHIFT_BLOCK, :])

    def conv_chunk(st, lc, c):
        row0 = c * RC
        lanes = slice(lc * CONV_LANES, (lc + 1) * CONV_LANES)
        acc = jnp.zeros((groups, SUBLANES, CONV_LANES), F32)
        for r in range(SUBLANES):
            offs = [o for o in range(first, first + CONV_KERNEL) if o % SUBLANES == r]
            g0, g1 = offs[0] // SUBLANES, offs[-1] // SUBLANES
            span = groups + g1 - g0
            start = row0 + g0 * SUBLANES
            xs = sh_ref[st, lc, r, start:start + span * SUBLANES, :]
            xs = xs.reshape(span, SUBLANES, CONV_LANES)
            for o in offs:
                g = o // SUBLANES - g0
                acc = acc + cw_ref[lc, o - first][None] * xs[g:g + groups]
        cv = acc.reshape(RC, CONV_LANES) + cb_ref[:, lanes]
        cv_ref[st, row0:row0 + RC, lanes] = cv.astype(BF16)

    def stream_work(st):
        mxu_work, vpu_work, text = [], [], []
        for group, out_ref, act in ((G_AGATE, sa_ref, _silu), (G_BGATE, sb_ref, _silu),
                                    (G_GA, ga_ref, _sigmoid), (G_GB, gb_ref, _sigmoid)):
            mxu_work += [functools.partial(gate_piece, st, group, out_ref, act, p)
                         for p in range(D_MODEL // PROJ_COLS)]
        for lc in range(n_lc):
            vpu_work.append(functools.partial(shift_chunk, st, lc))
            vpu_work += [functools.partial(conv_chunk, st, lc, c) for c in range(TC // RC)]
        issued = 0
        for i, mxu_fn in enumerate(mxu_work):
            target = (len(vpu_work) * (i + 1)) // len(mxu_work)
            text.append([mxu_fn] + vpu_work[issued:target])
            issued = target
        return text

    @pl.when(k >= 0)
    def _():
        for parts in zip(*[stream_work(st) for st in streams]):
            for part in parts:
                for fn in part:
                    fn()


def _gateconv(x, shift, scale, norm_w, w_wide, conv_w, conv_b):
    b, s, d = x.shape
    n_lc = d // CONV_LANES
    gs = GC_STREAMS
    tok = pl.BlockSpec((gs, TC, d), lambda bi, k: (bi, k, 0))
    mod = pl.BlockSpec((gs, 1, d), lambda bi, k: (bi, 0, 0))
    out = jax.ShapeDtypeStruct((b, s, d), BF16)
    return pl.pallas_call(
        _gateconv_kernel,
        grid=(b // gs, s // TC),
        in_specs=[tok, mod, mod, _const_spec((1, d)), _const_spec(w_wide.shape),
                  _const_spec(conv_w.shape), _const_spec((1, d))],
        out_specs=[tok] * 5,
        out_shape=[out] * 5,
        scratch_shapes=[pltpu.VMEM((gs, n_lc, SUBLANES, HALO + TC, CONV_LANES), F32),
                        pltpu.VMEM((gs, TC, d), BF16)],
        compiler_params=pltpu.CompilerParams(
            dimension_semantics=("arbitrary", "arbitrary"), vmem_limit_bytes=VMEM_LIMIT),
        name="gateconv",
    )(x, shift, scale, norm_w, w_wide, conv_w, conv_b)


def _rope_kernel(pos_ref, invf_ref, table_ref):
    ang = pos_ref[...] * invf_ref[...]
    cos_d, sin_d = jnp.cos(ang), jnp.sin(ang)
    n_freq = LANES // ROPE_PACK
    quarter = lax.broadcasted_iota(jnp.int32, ang.shape, 1) // n_freq
    for p in range(ROPE_PACK):
        def placed(src, q):
            return pltpu.roll(src, ((q - p) * n_freq) % LANES, 1)
        row = jnp.where(quarter == 0, placed(cos_d, 0),
                        jnp.where(quarter == 1, placed(cos_d, 1),
                                  jnp.where(quarter == 2, placed(sin_d, 2), placed(sin_d, 3))))
        table_ref[pl.ds(p, ang.shape[0], stride=ROPE_PACK), :] = row


def _rope_table(pos_dense, invf):
    rows = pos_dense.shape[0]
    tr = 512
    return pl.pallas_call(
        _rope_kernel,
        grid=(rows // tr,),
        in_specs=[pl.BlockSpec((tr, LANES), lambda i: (i, 0)), _const_spec(invf.shape)],
        out_specs=pl.BlockSpec((ROPE_PACK * tr, LANES), lambda i: (i, 0)),
        out_shape=jax.ShapeDtypeStruct((ROPE_PACK * rows, LANES), F32),
        compiler_params=pltpu.CompilerParams(dimension_semantics=("arbitrary",)),
        name="rope",
    )(pos_dense, invf)


def _mla_kernel(x_ref, shift_ref, scale_ref, nw_ref, table_ref, w_ref,
                qn_ref, kvn_ref, wuq_ref, wuk_ref, wvt_ref,
                q_ref, k_ref, vt_ref):
    hb = _modulated_norm(x_ref, shift_ref, scale_ref, nw_ref)
    lat = lax.dot_general(hb, w_ref[...], (((1,), (1,)), ((), ())),
                          preferred_element_type=F32)
    cq = lat[:, :Q_LORA_RANK]
    ckv = lat[:, Q_LORA_RANK:Q_LORA_RANK + KV_LORA_RANK]
    kp = lat[:, Q_LORA_RANK + KV_LORA_RANK:]

    table = table_ref[0]
    lane = lax.broadcasted_iota(jnp.int32, table.shape, 1)
    first_half = lane < QK_ROPE_DIM

    def rope(t):
        return jnp.where(first_half, t + pltpu.roll(t, QK_ROPE_DIM, 1), 0.0)

    cqn = _rms(cq, qn_ref[...]).astype(BF16)
    ckvn = _rms(ckv, kvn_ref[...]).astype(BF16)

    qa = jnp.dot(cqn, wuq_ref[...], preferred_element_type=F32)
    kn = jnp.dot(ckvn, wuk_ref[...], preferred_element_type=F32)
    vt = lax.dot_general(wvt_ref[...], ckvn, (((1,), (1,)), ((), ())),
                         preferred_element_type=F32)

    k_hi = rope(kp * table).astype(BF16)
    q_table = table * Q_SCALE
    for h in range(N_HEADS):
        blk = qa[:, h * QK_PAD:(h + 1) * QK_PAD]
        q_ref[0, h, :, :LANES] = (blk[:, :LANES] * Q_SCALE).astype(BF16)
        q_ref[0, h, :, LANES:] = rope(blk[:, LANES:] * q_table).astype(BF16)
        k_ref[0, h, :, :LANES] = kn[:, h * LANES:(h + 1) * LANES].astype(BF16)
        k_ref[0, h, :, LANES:] = k_hi
    ones = jnp.ones((V_PAD - V_DIM, TK), BF16)
    for j in range(TM // TK):
        for h in range(N_HEADS):
            vt_ref[0, j, h, :V_DIM, :] = vt[h * V_DIM:(h + 1) * V_DIM, j * TK:(j + 1) * TK].astype(BF16)
            vt_ref[0, j, h, V_DIM:, :] = ones


def _mla_proj(x, shift, scale, norm_w, table, w_lat_t, qn_w, kvn_w, w_uq2, w_uk, w_vt):
    b, s, d = x.shape
    tok = pl.BlockSpec((1, TM, d), lambda bi, i: (bi, i, 0))
    mod = pl.BlockSpec((1, 1, d), lambda bi, i: (bi, 0, 0))
    qk_spec = pl.BlockSpec((1, N_HEADS, TM, QK_PAD), lambda bi, i: (bi, 0, i, 0))
    return pl.pallas_call(
        _mla_kernel,
        grid=(b, s // TM),
        in_specs=[
            tok, mod, mod, _const_spec((1, d)),
            pl.BlockSpec((1, TM, LANES), lambda bi, i: (bi, i, 0)),
            _const_spec(w_lat_t.shape),
            _const_spec(qn_w.shape), _const_spec(kvn_w.shape),
            _const_spec(w_uq2.shape), _const_spec(w_uk.shape), _const_spec(w_vt.shape),
        ],
        out_specs=[
            qk_spec, qk_spec,
            pl.BlockSpec((1, TM // TK, N_HEADS, V_PAD, TK), lambda bi, i: (bi, i, 0, 0, 0)),
        ],
        out_shape=[
            jax.ShapeDtypeStruct((b, N_HEADS, s, QK_PAD), BF16),
            jax.ShapeDtypeStruct((b, N_HEADS, s, QK_PAD), BF16),
            jax.ShapeDtypeStruct((b, s // TK, N_HEADS, V_PAD, TK), BF16),
        ],
        compiler_params=pltpu.CompilerParams(
            dimension_semantics=("arbitrary", "arbitrary"), vmem_limit_bytes=VMEM_LIMIT),
        name="mla_proj",
    )(x, shift, scale, norm_w, table, w_lat_t, qn_w, kvn_w, w_uq2, w_uk, w_vt)


ATTN_BUFFERS = 3
ATTN_AHEAD = 2
ATTN_GROUP = 3
ATTN_HEADS = 2


def _attn_scores(q_ref, k_ref, bias_ref, hh, item, s_out, bm_out, diagonal):
    qn, jn = item
    k = k_ref[0, hh, pl.ds(pl.multiple_of(jn * TK, TK), TK), :]
    q = q_ref[0, hh, pl.ds(pl.multiple_of(qn * TQ, TQ), TQ), :]
    s = lax.dot_general(k, q, (((1,), (1,)), ((), ())), preferred_element_type=F32)
    if diagonal:
        s = s + bias_ref[...]
    s_out[...] = s
    bm_out[...] = jnp.max(s, axis=0, keepdims=True)


def _attn_accumulate(vt_ref, o_ref, acc_ref, hh, item, p_in, al_in, diagonal):
    qp, jp = item
    pv = jnp.dot(vt_ref[0, jp, hh], p_in[...], preferred_element_type=F32)
    acc = al_in[...] * acc_ref[hh, qp] + pv
    acc_ref[hh, qp] = acc
    if diagonal:
        rows = pl.ds(pl.multiple_of(qp * TQ, TQ), TQ)
        out = acc[:V_DIM] / acc[V_DIM:V_DIM + 1]
        o_ref[0, rows, hh * V_DIM:(hh + 1) * V_DIM] = jnp.transpose(out).astype(BF16)


def _attn_step(q_ref, k_ref, vt_ref, bias_ref, o_ref, acc_ref, m_ref, hh, cur, nxt, prv, items,
               diag):
    s_cur, p_cur, bm_cur, al_cur = cur
    s_nxt, _, bm_nxt, _ = nxt
    _, p_prv, _, al_prv = prv
    item_next, (qc, jc), item_prev = items
    diag_next, diag_prev = diag
    _attn_accumulate(vt_ref, o_ref, acc_ref, hh, item_prev, p_prv, al_prv, diag_prev)
    m_old = jnp.where(jc == 0, -jnp.inf, m_ref[hh, qc])
    m_new = jnp.maximum(m_old, bm_cur[...])
    p_cur[...] = jnp.exp2(s_cur[...] - m_new).astype(BF16)
    al_cur[...] = jnp.exp2(m_old - m_new)
    m_ref[hh, qc] = m_new
    _attn_scores(q_ref, k_ref, bias_ref, hh, item_next, s_nxt, bm_nxt, diag_next)


def _attn_item(t, n_lower):
    def lower(t):
        q = jnp.int32(1)
        for c in range(2, 64):
            if c * (c - 1) // 2 >= n_lower:
                break
            q = q + (t >= c * (c - 1) // 2).astype(jnp.int32)
        return q, t - lax.shift_right_logical(q * (q - 1), 1)
    ql, jl = lower(jnp.minimum(t, n_lower - 1))
    d = t - n_lower
    is_diag = t >= n_lower
    return jnp.where(is_diag, d, ql), jnp.where(is_diag, d, jl)


def _attn_kernel(q_ref, k_ref, vt_ref, bias_ref, o_ref, acc_ref, m_ref, *bufs):
    n_q = q_ref.shape[2] // TQ
    n_lower = n_q * (n_q - 1) // 2
    n_items = n_lower + n_q
    nbuf = ATTN_BUFFERS
    heads = range(ATTN_HEADS)
    assert TQ == TK and ATTN_GROUP % nbuf == 0 and n_items % nbuf == 0
    rings = tuple(tuple(bufs[4 * (hh * nbuf + r):4 * (hh * nbuf + r) + 4] for r in range(nbuf))
                  for hh in heads)

    def item(t):
        return _attn_item(jnp.clip(t, 0, n_items - 1), n_lower)

    def steps(t0, count, t0_static=None):
        for u in range(count):
            t = t0 + u
            diag = (False, False) if t0_static is None else (
                t0_static + u + ATTN_AHEAD >= n_lower, t0_static + u - 1 >= n_lower)
            items = (item(t + ATTN_AHEAD), item(t), item(t - 1))
            for hh in heads:
                ring = rings[hh]
                _attn_step(q_ref, k_ref, vt_ref, bias_ref, o_ref, acc_ref, m_ref, hh,
                           ring[u % nbuf], ring[(u + ATTN_AHEAD) % nbuf], ring[(u - 1) % nbuf],
                           items, diag)

    acc_ref[...] = jnp.zeros_like(acc_ref)
    m_ref[...] = jnp.zeros_like(m_ref)
    for hh in heads:
        _, p_last, _, al_last = rings[hh][nbuf - 1]
        p_last[...] = jnp.zeros_like(p_last)
        al_last[...] = jnp.ones_like(al_last)
        for a in range(ATTN_AHEAD):
            s_buf, _, bm_buf, _ = rings[hh][a]
            _attn_scores(q_ref, k_ref, bias_ref, hh, item(jnp.int32(a)), s_buf, bm_buf, False)

    n_plain = (n_lower - ATTN_AHEAD) // ATTN_GROUP
    lax.fori_loop(0, n_plain, lambda g, c: (steps(g * ATTN_GROUP, ATTN_GROUP), c)[1], 0)
    t_tail = n_plain * ATTN_GROUP
    steps(jnp.int32(t_tail), n_items - t_tail, t_tail)
    for hh in heads:
        _, p_last, _, al_last = rings[hh][(n_items - 1) % nbuf]
        _attn_accumulate(vt_ref, o_ref, acc_ref, hh, item(jnp.int32(n_items - 1)), p_last, al_last,
                         True)


def _attention(q, k, vt):
    b, h, s, _ = q.shape
    n_q = s // TQ
    row = lax.broadcasted_iota(jnp.int32, (TK, TQ), 0)
    col = lax.broadcasted_iota(jnp.int32, (TK, TQ), 1)
    bias = jnp.where(row <= col, 0.0, -jnp.inf).astype(F32)
    stat = pltpu.VMEM((1, TQ), F32)
    parity = [pltpu.VMEM((TK, TQ), F32), pltpu.VMEM((TK, TQ), BF16), stat, stat]
    hs = ATTN_HEADS
    return pl.pallas_call(
        _attn_kernel,
        grid=(b, h // hs),
        in_specs=[
            pl.BlockSpec((1, hs, s, QK_PAD), lambda bi, hi: (bi, hi, 0, 0)),
            pl.BlockSpec((1, hs, s, QK_PAD), lambda bi, hi: (bi, hi, 0, 0)),
            pl.BlockSpec((1, s // TK, hs, V_PAD, TK), lambda bi, hi: (bi, 0, hi, 0, 0)),
            _const_spec(bias.shape),
        ],
        out_specs=pl.BlockSpec((1, s, hs * V_DIM), lambda bi, hi: (bi, 0, hi)),
        out_shape=jax.ShapeDtypeStruct((b, s, h * V_DIM), BF16),
        scratch_shapes=[pltpu.VMEM((hs, n_q, V_PAD, TQ), F32), pltpu.VMEM((hs, n_q, 1, TQ), F32)]
        + parity * (ATTN_BUFFERS * hs),
        compiler_params=pltpu.CompilerParams(
            dimension_semantics=("arbitrary", "arbitrary"),
            vmem_limit_bytes=VMEM_LIMIT),
        name="attn",
    )(q, k, vt, bias)


def _out_kernel(cv_ref, sa_ref, o_ref, sb_ref, ga_ref, gb_ref, x_ref, gate_ref, lw_ref, lb_ref,
                wco_ref, wao_ref, wo_ref, fw_ref, y_ref):
    for st in range(OUT_STREAMS):
        cv = cv_ref[st].astype(F32)
        mu = jnp.mean(cv, axis=-1, keepdims=True)
        xc = cv - mu
        var = jnp.mean(xc * xc, axis=-1, keepdims=True)
        ln = xc * lax.rsqrt(var + EPS) * lw_ref[...] + lb_ref[...]
        ug = (_silu(ln) * sa_ref[st].astype(F32)).astype(BF16)
        ya = jnp.dot(ug, wco_ref[...], preferred_element_type=F32)
        og = (o_ref[st].astype(F32) * sb_ref[st].astype(F32)).astype(BF16)
        yb = jnp.dot(og, wao_ref[...], preferred_element_type=F32)
        merged = ga_ref[st].astype(F32) * ya + gb_ref[st].astype(F32) * yb
        delta = jnp.dot(merged.astype(BF16), wo_ref[...], preferred_element_type=F32)
        y = x_ref[st] + gate_ref[st] * delta
        y_ref[st] = _rms(y, fw_ref[...])


def _out(cv, sa, o, sb, ga, gb, x, gate, ln_w, ln_b, w_co, w_ao, w_o, fw):
    b, s, d = x.shape
    tok = pl.BlockSpec((OUT_STREAMS, TO, d), lambda bi, i: (bi, i, 0))
    mod = pl.BlockSpec((OUT_STREAMS, 1, d), lambda bi, i: (bi, 0, 0))
    wspec = _const_spec((d, d))
    vec = _const_spec((1, d))
    return pl.pallas_call(
        _out_kernel,
        grid=(b // OUT_STREAMS, s // TO),
        in_specs=[tok, tok, tok, tok, tok, tok, tok, mod, vec, vec, wspec, wspec, wspec, vec],
        out_specs=tok,
        out_shape=jax.ShapeDtypeStruct((b, s, d), F32),
        compiler_params=pltpu.CompilerParams(
            dimension_semantics=("arbitrary", "arbitrary"), vmem_limit_bytes=VMEM_LIMIT),
        name="out",
    )(cv, sa, o, sb, ga, gb, x, gate, ln_w, ln_b, w_co, w_ao, w_o, fw)


def _rot_cols(w):
    half = w.shape[-1] // 2
    return jnp.concatenate([-w[..., half:], w[..., :half]], axis=-1)


def _layer(x, c_act_pad, table, w_ada, b_ada, norm_w, w_in, conv_w, conv_b, conv_ln_w,
           conv_ln_b, w_conv_out, q_norm_w, w_uq, kv_norm_w, w_ukv, w_attn_out, w_out, final_w):
    b, s, d = x.shape
    mod = _ada(c_act_pad, w_ada, b_ada[None, :])[:b]
    shift, scale, gate = (m[:, None, :] for m in jnp.split(mod, 3, axis=-1))

    assert w_in.shape[1] == IN_COLS
    w_in_t = w_in.T
    w_wide = _wprep(w_in_t)
    w_kpe_t = w_in_t[IN_COL_BGATE - QK_ROPE_DIM:IN_COL_BGATE]
    half = QK_ROPE_DIM // 2
    w_lat_t = jnp.concatenate([w_in_t[IN_COL_LATENT:IN_COL_BGATE], -w_kpe_t[half:], w_kpe_t[:half]],
                              axis=0)
    w_lat_t = lax.optimization_barrier(w_lat_t).astype(BF16)

    wq = w_uq.reshape(Q_LORA_RANK, N_HEADS, QK_NOPE_DIM + QK_ROPE_DIM)
    wq_rope = wq[..., QK_NOPE_DIM:]
    w_uq2 = jnp.concatenate([wq, _rot_cols(wq_rope)], axis=-1)
    w_uq2 = w_uq2.reshape(Q_LORA_RANK, N_HEADS * QK_PAD).astype(BF16)
    wkv = w_ukv.reshape(KV_LORA_RANK, N_HEADS, QK_NOPE_DIM + V_DIM)
    w_uk = wkv[..., :QK_NOPE_DIM].reshape(KV_LORA_RANK, N_HEADS * QK_NOPE_DIM).astype(BF16)
    w_vt = wkv[..., QK_NOPE_DIM:].reshape(KV_LORA_RANK, ATTN_WIDTH).T.astype(BF16)

    nw = norm_w[None, :]
    cw = conv_w.reshape(CONV_KERNEL, CONV_WIDTH // CONV_LANES, 1, CONV_LANES).transpose(1, 0, 2, 3)
    cw = jnp.broadcast_to(cw, (CONV_WIDTH // CONV_LANES, CONV_KERNEL, SUBLANES, CONV_LANES))
    cv, sa, sb, ga, gb = _gateconv(x, shift, scale, nw, w_wide, cw, conv_b[None, :])
    q, k, vt = _mla_proj(x, shift, scale, nw, table, w_lat_t, q_norm_w[None, :],
                         kv_norm_w[None, :], w_uq2, w_uk, w_vt)
    o = _attention(q, k, vt)
    return _out(cv, sa, o, sb, ga, gb, x, gate, conv_ln_w[None, :], conv_ln_b[None, :],
                w_conv_out.astype(BF16), w_attn_out.astype(BF16), w_out.astype(BF16),
                final_w[None, :])


def kernel(x, c, positions, w_ada, b_ada, norm_w, w_in, conv_w, conv_b, conv_ln_w, conv_ln_b,
           w_conv_out, q_norm_w, w_uq, kv_norm_w, w_ukv, w_attn_out, w_out, final_norm_w):
    b, s, d = x.shape
    depth = w_ada.shape[0]
    assert depth == 1, "final rmsnorm is fused into the single layer's output kernel"
    inv_freq = ROPE_THETA ** (-jnp.arange(0, QK_ROPE_DIM, 2, dtype=F32) / QK_ROPE_DIM)
    n_freq = inv_freq.shape[0]
    assert ROPE_PACK * n_freq == LANES
    invf = jnp.tile(inv_freq, ROPE_PACK)[None, :]
    pos_dense = jnp.repeat(positions.astype(F32).reshape(b * s // ROPE_PACK, ROPE_PACK), n_freq, axis=1)
    table = _rope_table(pos_dense, invf).reshape(b, s, LANES)
    c_pad = jnp.pad(c, ((0, 8 - b), (0, 0)))
    return _layer(x, c_pad, table, w_ada[0], b_ada[0], norm_w[0], w_in[0], conv_w[0],
                  conv_b[0], conv_ln_w[0], conv_ln_b[0], w_conv_out[0], q_norm_w[0], w_uq[0],
                  kv_norm_w[0], w_ukv[0], w_attn_out[0], w_out[0], final_norm_w)
```

```python
import functools
import math

import jax
import jax.numpy as jnp
from jax import lax
from jax.experimental import pallas as pl
from jax.experimental.pallas import tpu as pltpu

F32 = jnp.float32
BF16 = jnp.bfloat16

D_MODEL = 1024
CONV_WIDTH = 1024
CONV_KERNEL = 31
N_HEADS = 8
QK_NOPE_DIM = 128
QK_ROPE_DIM = 64
V_DIM = 128
Q_LORA_RANK = 256
KV_LORA_RANK = 256
ATTN_WIDTH = N_HEADS * V_DIM
ROPE_THETA = 10000.0
EPS = 1e-6

LANES = 128
QK_PAD = 2 * LANES
BF16_SUBLANES = 16
V_PAD = V_DIM + BF16_SUBLANES
ROPE_PACK = LANES // (QK_ROPE_DIM // 2)
HALO = 32
VMEM_LIMIT = 56 * 1024 * 1024

Q_SCALE = (QK_NOPE_DIM + QK_ROPE_DIM) ** -0.5 * math.log2(math.e)

IN_COL_LATENT = 3 * CONV_WIDTH
IN_COL_BGATE = IN_COL_LATENT + Q_LORA_RANK + KV_LORA_RANK + QK_ROPE_DIM
IN_COLS = IN_COL_BGATE + ATTN_WIDTH + 2 * D_MODEL
G_VAL, G_GLU, G_AGATE, G_BGATE, G_GA, G_GB = range(6)
N_WIDE_GROUPS = 6
assert CONV_WIDTH == ATTN_WIDTH == D_MODEL

TM = 1024
TC = 256
RC = 64
CONV_LANES = 256
PROJ_COLS = 1024
GC_STREAMS = 2
TO = 512
OUT_STREAMS = 1
SUBLANES = 8
SHIFT_ROWS = TC + HALO - SUBLANES
SHIFT_BLOCK = 40
assert SHIFT_ROWS % SHIFT_BLOCK == 0 and SHIFT_BLOCK % SUBLANES == 0
TQ = 512
TK = 512


def _sigmoid(x):
    return 1.0 / (1.0 + jnp.exp(-x))


def _silu(x):
    return x * _sigmoid(x)


def _rms(x, w):
    return x * lax.rsqrt(jnp.mean(x * x, axis=-1, keepdims=True) + EPS) * w


def _const_spec(shape):
    return pl.BlockSpec(shape, lambda *_: (0,) * len(shape), pipeline_mode=pl.Buffered(1))


def _ada_kernel(c_ref, w_ref, b_ref, o_ref):
    c = c_ref[...]
    o_ref[...] = jnp.dot(_silu(c), w_ref[...], preferred_element_type=F32) + b_ref[...]


def _ada(c_pad, w_ada, b_ada):
    rows, d = c_pad.shape
    n = w_ada.shape[1]
    tn = 512
    return pl.pallas_call(
        _ada_kernel,
        grid=(n // tn,),
        in_specs=[
            pl.BlockSpec((rows, d), lambda j: (0, 0)),
            pl.BlockSpec((d, tn), lambda j: (0, j)),
            pl.BlockSpec((1, tn), lambda j: (0, j)),
        ],
        out_specs=pl.BlockSpec((rows, tn), lambda j: (0, j)),
        out_shape=jax.ShapeDtypeStruct((rows, n), F32),
        compiler_params=pltpu.CompilerParams(dimension_semantics=("arbitrary",)),
        name="ada",
    )(c_pad, w_ada, b_ada)


def _modulated_norm(x_ref, shift_ref, scale_ref, nw_ref):
    x = x_ref[0]
    h = _rms(x, nw_ref[...]) * (1.0 + scale_ref[0]) + shift_ref[0]
    return h.astype(BF16)


def _wprep_kernel(wt_ref, out_ref):
    blk = wt_ref[...].astype(BF16)
    n = blk.shape[1]
    eye = (lax.broadcasted_iota(jnp.int32, (n, n), 0)
           == lax.broadcasted_iota(jnp.int32, (n, n), 1)).astype(BF16)
    out_ref[...] = lax.dot_general(eye, blk, (((1,), (1,)), ((), ())),
                                   preferred_element_type=F32).astype(BF16)


def _wprep(w_in_t):
    d = w_in_t.shape[1]

    def first_row(g):
        row = jnp.where(g < G_BGATE, g * d, IN_COL_BGATE + (g - G_BGATE) * d)
        return pl.multiple_of(row, math.gcd(d, IN_COL_BGATE))

    return pl.pallas_call(
        _wprep_kernel,
        grid=(N_WIDE_GROUPS,),
        in_specs=[pl.BlockSpec((pl.Element(d), pl.Element(d)), lambda g: (first_row(g), 0))],
        out_specs=pl.BlockSpec((d, d), lambda g: (0, g)),
        out_shape=jax.ShapeDtypeStruct((d, N_WIDE_GROUPS * d), BF16),
        compiler_params=pltpu.CompilerParams(dimension_semantics=("arbitrary",)),
        name="wprep",
    )(w_in_t)


def _gateconv_kernel(x_ref, shift_ref, scale_ref, nw_ref, w_ref, cw_ref, cb_ref,
                     cv_ref, sa_ref, sb_ref, ga_ref, gb_ref, sh_ref, hb_ref):
    k = pl.program_id(1)
    n_lc = CONV_WIDTH // CONV_LANES
    first = HALO - (CONV_KERNEL - 1)
    groups = RC // SUBLANES
    streams = range(GC_STREAMS)

    @pl.when(k == 0)
    def _():
        for st in streams:
            for lc in range(n_lc):
                sh_ref[st, lc, 0, 0:HALO, :] = jnp.zeros((HALO, CONV_LANES), F32)

    @pl.when(k > 0)
    def _():
        for st in streams:
            for lc in range(n_lc):
                sh_ref[st, lc, 0, 0:HALO, :] = sh_ref[st, lc, 0, TC:TC + HALO, :]

    def proj(st, group, piece):
        col = group * D_MODEL + piece * PROJ_COLS
        return jnp.dot(hb_ref[st], w_ref[:, col:col + PROJ_COLS], preferred_element_type=F32)

    for st in streams:
        h = _rms(x_ref[st], nw_ref[...]) * (1.0 + scale_ref[st]) + shift_ref[st]
        hb_ref[st] = h.astype(BF16)
        for piece in range(D_MODEL // PROJ_COLS):
            u = proj(st, G_VAL, piece) * _sigmoid(proj(st, G_GLU, piece))
            for j in range(PROJ_COLS // CONV_LANES):
                lc = piece * (PROJ_COLS // CONV_LANES) + j
                sh_ref[st, lc, 0, HALO:, :] = u[:, j * CONV_LANES:(j + 1) * CONV_LANES]

    def gate_piece(st, group, out_ref, act, piece):
        cols = slice(piece * PROJ_COLS, (piece + 1) * PROJ_COLS)
        out_ref[st, :, cols] = act(proj(st, group, piece)).astype(BF16)

    def shift_chunk(st, lc):
        for r in range(1, SUBLANES):
            for rb in range(0, SHIFT_ROWS, SHIFT_BLOCK):
                sh_ref[st, lc, r, rb:rb + SHIFT_BLOCK, :] = (
                    sh_ref[st, lc, 0, rb + r:rb + r + SHIFT_BLOCK, :])

    def conv_chunk(st, lc, c):
        row0 = c * RC
        lanes = slice(lc * CONV_LANES, (lc + 1) * CONV_LANES)
        acc = jnp.zeros((groups, SUBLANES, CONV_LANES), F32)
        for r in range(SUBLANES):
            offs = [o for o in range(first, first + CONV_KERNEL) if o % SUBLANES == r]
            g0, g1 = offs[0] // SUBLANES, offs[-1] // SUBLANES
            span = groups + g1 - g0
            start = row0 + g0 * SUBLANES
            xs = sh_ref[st, lc, r, start:start + span * SUBLANES, :]
            xs = xs.reshape(span, SUBLANES, CONV_LANES)
            for o in offs:
                g = o // SUBLANES - g0
                acc = acc + cw_ref[lc, o - first][None] * xs[g:g + groups]
        cv = acc.reshape(RC, CONV_LANES) + cb_ref[:, lanes]
        cv_ref[st, row0:row0 + RC, lanes] = cv.astype(BF16)

    def stream_work(st):
        mxu_work, vpu_work, text = [], [], []
        for group, out_ref, act in ((G_AGATE, sa_ref, _silu), (G_BGATE, sb_ref, _silu),
                                    (G_GA, ga_ref, _sigmoid), (G_GB, gb_ref, _sigmoid)):
            mxu_work += [functools.partial(gate_piece, st, group, out_ref, act, p)
                         for p in range(D_MODEL // PROJ_COLS)]
        for lc in range(n_lc):
            vpu_work.append(functools.partial(shift_chunk, st, lc))
            vpu_work += [functools.partial(conv_chunk, st, lc, c) for c in range(TC // RC)]
        issued = 0
        for i, mxu_fn in enumerate(mxu_work):
            target = (len(vpu_work) * (i + 1)) // len(mxu_work)
            text.append([mxu_fn] + vpu_work[issued:target])
            issued = target
        return text

    @pl.when(k >= 0)
    def _():
        for parts in zip(*[stream_work(st) for st in streams]):
            for part in parts:
                for fn in part:
                    fn()


def _gateconv(x, shift, scale, norm_w, w_wide, conv_w, conv_b):
    b, s, d = x.shape
    n_lc = d // CONV_LANES
    gs = GC_STREAMS
    tok = pl.BlockSpec((gs, TC, d), lambda bi, k: (bi, k, 0))
    mod = pl.BlockSpec((gs, 1, d), lambda bi, k: (bi, 0, 0))
    out = jax.ShapeDtypeStruct((b, s, d), BF16)
    return pl.pallas_call(
        _gateconv_kernel,
        grid=(b // gs, s // TC),
        in_specs=[tok, mod, mod, _const_spec((1, d)), _const_spec(w_wide.shape),
                  _const_spec(conv_w.shape), _const_spec((1, d))],
        out_specs=[tok] * 5,
        out_shape=[out] * 5,
        scratch_shapes=[pltpu.VMEM((gs, n_lc, SUBLANES, HALO + TC, CONV_LANES), F32),
                        pltpu.VMEM((gs, TC, d), BF16)],
        compiler_params=pltpu.CompilerParams(
            dimension_semantics=("arbitrary", "arbitrary"), vmem_limit_bytes=VMEM_LIMIT),
        name="gateconv",
    )(x, shift, scale, norm_w, w_wide, conv_w, conv_b)


def _rope_kernel(pos_ref, invf_ref, table_ref):
    ang = pos_ref[...] * invf_ref[...]
    cos_d, sin_d = jnp.cos(ang), jnp.sin(ang)
    n_freq = LANES // ROPE_PACK
    quarter = lax.broadcasted_iota(jnp.int32, ang.shape, 1) // n_freq
    for p in range(ROPE_PACK):
        def placed(src, q):
            return pltpu.roll(src, ((q - p) * n_freq) % LANES, 1)
        row = jnp.where(quarter == 0, placed(cos_d, 0),
                        jnp.where(quarter == 1, placed(cos_d, 1),
                                  jnp.where(quarter == 2, placed(sin_d, 2), placed(sin_d, 3))))
        table_ref[pl.ds(p, ang.shape[0], stride=ROPE_PACK), :] = row


def _rope_table(pos_dense, invf):
    rows = pos_dense.shape[0]
    tr = 512
    return pl.pallas_call(
        _rope_kernel,
        grid=(rows // tr,),
        in_specs=[pl.BlockSpec((tr, LANES), lambda i: (i, 0)), _const_spec(invf.shape)],
        out_specs=pl.BlockSpec((ROPE_PACK * tr, LANES), lambda i: (i, 0)),
        out_shape=jax.ShapeDtypeStruct((ROPE_PACK * rows, LANES), F32),
        compiler_params=pltpu.CompilerParams(dimension_semantics=("arbitrary",)),
        name="rope",
    )(pos_dense, invf)


def _mla_kernel(x_ref, shift_ref, scale_ref, nw_ref, table_ref, w_ref,
                qn_ref, kvn_ref, wuq_ref, wuk_ref, wvt_ref,
                q_ref, k_ref, vt_ref):
    hb = _modulated_norm(x_ref, shift_ref, scale_ref, nw_ref)
    lat = lax.dot_general(hb, w_ref[...], (((1,), (1,)), ((), ())),
                          preferred_element_type=F32)
    cq = lat[:, :Q_LORA_RANK]
    ckv = lat[:, Q_LORA_RANK:Q_LORA_RANK + KV_LORA_RANK]
    kp = lat[:, Q_LORA_RANK + KV_LORA_RANK:]

    table = table_ref[0]
    lane = lax.broadcasted_iota(jnp.int32, table.shape, 1)
    first_half = lane < QK_ROPE_DIM

    def rope(t):
        return jnp.where(first_half, t + pltpu.roll(t, QK_ROPE_DIM, 1), 0.0)

    cqn = _rms(cq, qn_ref[...]).astype(BF16)
    ckvn = _rms(ckv, kvn_ref[...]).astype(BF16)

    qa = jnp.dot(cqn, wuq_ref[...], preferred_element_type=F32)
    kn = jnp.dot(ckvn, wuk_ref[...], preferred_element_type=F32)
    vt = lax.dot_general(wvt_ref[...], ckvn, (((1,), (1,)), ((), ())),
                         preferred_element_type=F32)

    k_hi = rope(kp * table).astype(BF16)
    q_table = table * Q_SCALE
    for h in range(N_HEADS):
        blk = qa[:, h * QK_PAD:(h + 1) * QK_PAD]
        q_ref[0, h, :, :LANES] = (blk[:, :LANES] * Q_SCALE).astype(BF16)
        q_ref[0, h, :, LANES:] = rope(blk[:, LANES:] * q_table).astype(BF16)
        k_ref[0, h, :, :LANES] = kn[:, h * LANES:(h + 1) * LANES].astype(BF16)
        k_ref[0, h, :, LANES:] = k_hi
    ones = jnp.ones((V_PAD - V_DIM, TK), BF16)
    for j in range(TM // TK):
        for h in range(N_HEADS):
            vt_ref[0, j, h, :V_DIM, :] = vt[h * V_DIM:(h + 1) * V_DIM, j * TK:(j + 1) * TK].astype(BF16)
            vt_ref[0, j, h, V_DIM:, :] = ones


def _mla_proj(x, shift, scale, norm_w, table, w_lat_t, qn_w, kvn_w, w_uq2, w_uk, w_vt):
    b, s, d = x.shape
    tok = pl.BlockSpec((1, TM, d), lambda bi, i: (bi, i, 0))
    mod = pl.BlockSpec((1, 1, d), lambda bi, i: (bi, 0, 0))
    qk_spec = pl.BlockSpec((1, N_HEADS, TM, QK_PAD), lambda bi, i: (bi, 0, i, 0))
    return pl.pallas_call(
        _mla_kernel,
        grid=(b, s // TM),
        in_specs=[
            tok, mod, mod, _const_spec((1, d)),
            pl.BlockSpec((1, TM, LANES), lambda bi, i: (bi, i, 0)),
            _const_spec(w_lat_t.shape),
            _const_spec(qn_w.shape), _const_spec(kvn_w.shape),
            _const_spec(w_uq2.shape), _const_spec(w_uk.shape), _const_spec(w_vt.shape),
        ],
        out_specs=[
            qk_spec, qk_spec,
            pl.BlockSpec((1, TM // TK, N_HEADS, V_PAD, TK), lambda bi, i: (bi, i, 0, 0, 0)),
        ],
        out_shape=[
            jax.ShapeDtypeStruct((b, N_HEADS, s, QK_PAD), BF16),
            jax.ShapeDtypeStruct((b, N_HEADS, s, QK_PAD), BF16),
            jax.ShapeDtypeStruct((b, s // TK, N_HEADS, V_PAD, TK), BF16),
        ],
        compiler_params=pltpu.CompilerParams(
            dimension_semantics=("arbitrary", "arbitrary"), vmem_limit_bytes=VMEM_LIMIT),
        name="mla_proj",
    )(x, shift, scale, norm_w, table, w_lat_t, qn_w, kvn_w, w_uq2, w_uk, w_vt)


ATTN_BUFFERS = 3
ATTN_AHEAD = 2
ATTN_GROUP = 3
ATTN_HEADS = 2


def _attn_scores(q_ref, k_ref, bias_ref, hh, item, s_out, bm_out, diagonal):
    qn, jn = item
    k = k_ref[0, hh, pl.ds(pl.multiple_of(jn * TK, TK), TK), :]
    q = q_ref[0, hh, pl.ds(pl.multiple_of(qn * TQ, TQ), TQ), :]
    s = lax.dot_general(k, q, (((1,), (1,)), ((), ())), preferred_element_type=F32)
    if diagonal:
        s = s + bias_ref[...]
    s_out[...] = s
    bm_out[...] = jnp.max(s, axis=0, keepdims=True)


def _attn_accumulate(vt_ref, o_ref, acc_ref, hh, item, p_in, al_in, diagonal):
    qp, jp = item
    pv = jnp.dot(vt_ref[0, jp, hh], p_in[...], preferred_element_type=F32)
    acc = al_in[...] * acc_ref[hh, qp] + pv
    acc_ref[hh, qp] = acc
    if diagonal:
        rows = pl.ds(pl.multiple_of(qp * TQ, TQ), TQ)
        out = acc[:V_DIM] / acc[V_DIM:V_DIM + 1]
        o_ref[0, rows, hh * V_DIM:(hh + 1) * V_DIM] = jnp.transpose(out).astype(BF16)


def _attn_step(q_ref, k_ref, vt_ref, bias_ref, o_ref, acc_ref, m_ref, hh, cur, nxt, prv, items,
               diag):
    s_cur, p_cur, bm_cur, al_cur = cur
    s_nxt, _, bm_nxt, _ = nxt
    _, p_prv, _, al_prv = prv
    item_next, (qc, jc), item_prev = items
    diag_next, diag_prev = diag
    _attn_accumulate(vt_ref, o_ref, acc_ref, hh, item_prev, p_prv, al_prv, diag_prev)
    m_old = jnp.where(jc == 0, -jnp.inf, m_ref[hh, qc])
    m_new = jnp.maximum(m_old, bm_cur[...])
    p_cur[...] = jnp.exp2(s_cur[...] - m_new).astype(BF16)
    al_cur[...] = jnp.exp2(m_old - m_new)
    m_ref[hh, qc] = m_new
    _attn_scores(q_ref, k_ref, bias_ref, hh, item_next, s_nxt, bm_nxt, diag_next)


def _attn_item(t, n_lower):
    def lower(t):
        q = jnp.int32(1)
        for c in range(2, 64):
            if c * (c - 1) // 2 >= n_lower:
                break
            q = q + (t >= c * (c - 1) // 2).astype(jnp.int32)
        return q, t - lax.shift_right_logical(q * (q - 1), 1)
    ql, jl = lower(jnp.minimum(t, n_lower - 1))
    d = t - n_lower
    is_diag = t >= n_lower
    return jnp.where(is_diag, d, ql), jnp.where(is_diag, d, jl)


def _attn_kernel(q_ref, k_ref, vt_ref, bias_ref, o_ref, acc_ref, m_ref, *bufs):
    n_q = q_ref.shape[2] // TQ
    n_lower = n_q * (n_q - 1) // 2
    n_items = n_lower + n_q
    nbuf = ATTN_BUFFERS
    heads = range(ATTN_HEADS)
    assert TQ == TK and ATTN_GROUP % nbuf == 0 and n_items % nbuf == 0
    rings = tuple(tuple(bufs[4 * (hh * nbuf + r):4 * (hh * nbuf + r) + 4] for r in range(nbuf))
                  for hh in heads)

    def item(t):
        return _attn_item(jnp.clip(t, 0, n_items - 1), n_lower)

    def steps(t0, count, t0_static=None):
        for u in range(count):
            t = t0 + u
            diag = (False, False) if t0_static is None else (
                t0_static + u + ATTN_AHEAD >= n_lower, t0_static + u - 1 >= n_lower)
            items = (item(t + ATTN_AHEAD), item(t), item(t - 1))
            for hh in heads:
                ring = rings[hh]
                _attn_step(q_ref, k_ref, vt_ref, bias_ref, o_ref, acc_ref, m_ref, hh,
                           ring[u % nbuf], ring[(u + ATTN_AHEAD) % nbuf], ring[(u - 1) % nbuf],
                           items, diag)

    acc_ref[...] = jnp.zeros_like(acc_ref)
    m_ref[...] = jnp.zeros_like(m_ref)
    for hh in heads:
        _, p_last, _, al_last = rings[hh][nbuf - 1]
        p_last[...] = jnp.zeros_like(p_last)
        al_last[...] = jnp.ones_like(al_last)
        for a in range(ATTN_AHEAD):
            s_buf, _, bm_buf, _ = rings[hh][a]
            _attn_scores(q_ref, k_ref, bias_ref, hh, item(jnp.int32(a)), s_buf, bm_buf, False)

    n_plain = (n_lower - ATTN_AHEAD) // ATTN_GROUP
    lax.fori_loop(0, n_plain, lambda g, c: (steps(g * ATTN_GROUP, ATTN_GROUP), c)[1], 0)
    t_tail = n_plain * ATTN_GROUP
    steps(jnp.int32(t_tail), n_items - t_tail, t_tail)
    for hh in heads:
        _, p_last, _, al_last = rings[hh][(n_items - 1) % nbuf]
        _attn_accumulate(vt_ref, o_ref, acc_ref, hh, item(jnp.int32(n_items - 1)), p_last, al_last,
                         True)


def _attention(q, k, vt):
    b, h, s, _ = q.shape
    n_q = s // TQ
    row = lax.broadcasted_iota(jnp.int32, (TK, TQ), 0)
    col = lax.broadcasted_iota(jnp.int32, (TK, TQ), 1)
    bias = jnp.where(row <= col, 0.0, -jnp.inf).astype(F32)
    stat = pltpu.VMEM((1, TQ), F32)
    parity = [pltpu.VMEM((TK, TQ), F32), pltpu.VMEM((TK, TQ), BF16), stat, stat]
    hs = ATTN_HEADS
    return pl.pallas_call(
        _attn_kernel,
        grid=(b, h // hs),
        in_specs=[
            pl.BlockSpec((1, hs, s, QK_PAD), lambda bi, hi: (bi, hi, 0, 0)),
            pl.BlockSpec((1, hs, s, QK_PAD), lambda bi, hi: (bi, hi, 0, 0)),
            pl.BlockSpec((1, s // TK, hs, V_PAD, TK), lambda bi, hi: (bi, 0, hi, 0, 0)),
            _const_spec(bias.shape),
        ],
        out_specs=pl.BlockSpec((1, s, hs * V_DIM), lambda bi, hi: (bi, 0, hi)),
        out_shape=jax.ShapeDtypeStruct((b, s, h * V_DIM), BF16),
        scratch_shapes=[pltpu.VMEM((hs, n_q, V_PAD, TQ), F32), pltpu.VMEM((hs, n_q, 1, TQ), F32)]
        + parity * (ATTN_BUFFERS * hs),
        compiler_params=pltpu.CompilerParams(
            dimension_semantics=("arbitrary", "arbitrary"),
            vmem_limit_bytes=VMEM_LIMIT),
        name="attn",
    )(q, k, vt, bias)


def _out_kernel(cv_ref, sa_ref, o_ref, sb_ref, ga_ref, gb_ref, x_ref, gate_ref, lw_ref, lb_ref,
                wco_ref, wao_ref, wo_ref, fw_ref, y_ref):
    for st in range(OUT_STREAMS):
        cv = cv_ref[st].astype(F32)
        mu = jnp.mean(cv, axis=-1, keepdims=True)
        xc = cv - mu
        var = jnp.mean(xc * xc, axis=-1, keepdims=True)
        ln = xc * lax.rsqrt(var + EPS) * lw_ref[...] + lb_ref[...]
        ug = (_silu(ln) * sa_ref[st].astype(F32)).astype(BF16)
        ya = jnp.dot(ug, wco_ref[...], preferred_element_type=F32)
        og = (o_ref[st].astype(F32) * sb_ref[st].astype(F32)).astype(BF16)
        yb = jnp.dot(og, wao_ref[...], preferred_element_type=F32)
        merged = ga_ref[st].astype(F32) * ya + gb_ref[st].astype(F32) * yb
        delta = jnp.dot(merged.astype(BF16), wo_ref[...], preferred_element_type=F32)
        y = x_ref[st] + gate_ref[st] * delta
        y_ref[st] = _rms(y, fw_ref[...])


def _out(cv, sa, o, sb, ga, gb, x, gate, ln_w, ln_b, w_co, w_ao, w_o, fw):
    b, s, d = x.shape
    tok = pl.BlockSpec((OUT_STREAMS, TO, d), lambda bi, i: (bi, i, 0))
    mod = pl.BlockSpec((OUT_STREAMS, 1, d), lambda bi, i: (bi, 0, 0))
    wspec = _const_spec((d, d))
    vec = _const_spec((1, d))
    return pl.pallas_call(
        _out_kernel,
        grid=(b // OUT_STREAMS, s // TO),
        in_specs=[tok, tok, tok, tok, tok, tok, tok, mod, vec, vec, wspec, wspec, wspec, vec],
        out_specs=tok,
        out_shape=jax.ShapeDtypeStruct((b, s, d), F32),
        compiler_params=pltpu.CompilerParams(
            dimension_semantics=("arbitrary", "arbitrary"), vmem_limit_bytes=VMEM_LIMIT),
        name="out",
    )(cv, sa, o, sb, ga, gb, x, gate, ln_w, ln_b, w_co, w_ao, w_o, fw)


def _rot_cols(w):
    half = w.shape[-1] // 2
    return jnp.concatenate([-w[..., half:], w[..., :half]], axis=-1)


def _layer(x, c_act_pad, table, w_ada, b_ada, norm_w, w_in, conv_w, conv_b, conv_ln_w,
           conv_ln_b, w_conv_out, q_norm_w, w_uq, kv_norm_w, w_ukv, w_attn_out, w_out, final_w):
    b, s, d = x.shape
    mod = _ada(c_act_pad, w_ada, b_ada[None, :])[:b]
    shift, scale, gate = (m[:, None, :] for m in jnp.split(mod, 3, axis=-1))

    assert w_in.shape[1] == IN_COLS
    w_in_t = w_in.T
    w_wide = _wprep(w_in_t)
    w_kpe_t = w_in_t[IN_COL_BGATE - QK_ROPE_DIM:IN_COL_BGATE]
    half = QK_ROPE_DIM // 2
    w_lat_t = jnp.concatenate([w_in_t[IN_COL_LATENT:IN_COL_BGATE], -w_kpe_t[half:], w_kpe_t[:half]],
                              axis=0)
    w_lat_t = lax.optimization_barrier(w_lat_t).astype(BF16)

    wq = w_uq.reshape(Q_LORA_RANK, N_HEADS, QK_NOPE_DIM + QK_ROPE_DIM)
    wq_rope = wq[..., QK_NOPE_DIM:]
    w_uq2 = jnp.concatenate([wq, _rot_cols(wq_rope)], axis=-1)
    w_uq2 = w_uq2.reshape(Q_LORA_RANK, N_HEADS * QK_PAD).astype(BF16)
    wkv = w_ukv.reshape(KV_LORA_RANK, N_HEADS, QK_NOPE_DIM + V_DIM)
    w_uk = wkv[..., :QK_NOPE_DIM].reshape(KV_LORA_RANK, N_HEADS * QK_NOPE_DIM).astype(BF16)
    w_vt = wkv[..., QK_NOPE_DIM:].reshape(KV_LORA_RANK, ATTN_WIDTH).T.astype(BF16)

    nw = norm_w[None, :]
    cw = conv_w.reshape(CONV_KERNEL, CONV_WIDTH // CONV_LANES, 1, CONV_LANES).transpose(1, 0, 2, 3)
    cw = jnp.broadcast_to(cw, (CONV_WIDTH // CONV_LANES, CONV_KERNEL, SUBLANES, CONV_LANES))
    cv, sa, sb, ga, gb = _gateconv(x, shift, scale, nw, w_wide, cw, conv_b[None, :])
    q, k, vt = _mla_proj(x, shift, scale, nw, table, w_lat_t, q_norm_w[None, :],
                         kv_norm_w[None, :], w_uq2, w_uk, w_vt)
    o = _attention(q, k, vt)
    return _out(cv, sa, o, sb, ga, gb, x, gate, conv_ln_w[None, :], conv_ln_b[None, :],
                w_conv_out.astype(BF16), w_attn_out.astype(BF16), w_out.astype(BF16),
                final_w[None, :])


def kernel(x, c, positions, w_ada, b_ada, norm_w, w_in, conv_w, conv_b, conv_ln_w, conv_ln_b,
           w_conv_out, q_norm_w, w_uq, kv_norm_w, w_ukv, w_attn_out, w_out, final_norm_w):
    b, s, d = x.shape
    depth = w_ada.shape[0]
    assert depth == 1, "final rmsnorm is fused into the single layer's output kernel"
    inv_freq = ROPE_THETA ** (-jnp.arange(0, QK_ROPE_DIM, 2, dtype=F32) / QK_ROPE_DIM)
    n_freq = inv_freq.shape[0]
    assert ROPE_PACK * n_freq == LANES
    invf = jnp.tile(inv_freq, ROPE_PACK)[None, :]
    pos_dense = jnp.repeat(positions.astype(F32).reshape(b * s // ROPE_PACK, ROPE_PACK), n_freq, axis=1)
    table = _rope_table(pos_dense, invf).reshape(b, s, LANES)
    c_pad = jnp.pad(c, ((0, 8 - b), (0, 0)))
    return _layer(x, c_pad, table, w_ada[0], b_ada[0], norm_w[0], w_in[0], conv_w[0],
                  conv_b[0], conv_ln_w[0], conv_ln_b[0], w_conv_out[0], q_norm_w[0], w_uq[0],
                  kv_norm_w[0], w_ukv[0], w_attn_out[0], w_out[0], final_norm_w)
```

```python
import functools
import math

import jax
import jax.numpy as jnp
from jax import lax
from jax.experimental import pallas as pl
from jax.experimental.pallas import tpu as pltpu

F32 = jnp.float32
BF16 = jnp.bfloat16

D_MODEL = 1024
CONV_WIDTH = 1024
CONV_KERNEL = 31
N_HEADS = 8
QK_NOPE_DIM = 128
QK_ROPE_DIM = 64
V_DIM = 128
Q_LORA_RANK = 256
KV_LORA_RANK = 256
ATTN_WIDTH = N_HEADS * V_DIM
ROPE_THETA = 10000.0
EPS = 1e-6

LANES = 128
QK_PAD = 2 * LANES
BF16_SUBLANES = 16
V_PAD = V_DIM + BF16_SUBLANES
ROPE_PACK = LANES // (QK_ROPE_DIM // 2)
HALO = 32
VMEM_LIMIT = 56 * 1024 * 1024

Q_SCALE = (QK_NOPE_DIM + QK_ROPE_DIM) ** -0.5 * math.log2(math.e)

IN_COL_LATENT = 3 * CONV_WIDTH
IN_COL_BGATE = IN_COL_LATENT + Q_LORA_RANK + KV_LORA_RANK + QK_ROPE_DIM
IN_COLS = IN_COL_BGATE + ATTN_WIDTH + 2 * D_MODEL
G_VAL, G_GLU, G_AGATE, G_BGATE, G_GA, G_GB = range(6)
N_WIDE_GROUPS = 6
assert CONV_WIDTH == ATTN_WIDTH == D_MODEL

TM = 1024
TC = 256
RC = 64
CONV_LANES = 256
PROJ_COLS = 1024
GC_STREAMS = 2
TO = 512
OUT_STREAMS = 1
SUBLANES = 8
SHIFT_ROWS = TC + HALO - SUBLANES
SHIFT_BLOCK = 40
assert SHIFT_ROWS % SHIFT_BLOCK == 0 and SHIFT_BLOCK % SUBLANES == 0
TQ = 512
TK = 512


def _sigmoid(x):
    return 1.0 / (1.0 + jnp.exp(-x))


def _silu(x):
    return x * _sigmoid(x)


def _rms(x, w):
    return x * lax.rsqrt(jnp.mean(x * x, axis=-1, keepdims=True) + EPS) * w


def _const_spec(shape):
    return pl.BlockSpec(shape, lambda *_: (0,) * len(shape), pipeline_mode=pl.Buffered(1))


def _ada_kernel(c_ref, w_ref, b_ref, o_ref):
    c = c_ref[...]
    o_ref[...] = jnp.dot(_silu(c), w_ref[...], preferred_element_type=F32) + b_ref[...]


def _ada(c_pad, w_ada, b_ada):
    rows, d = c_pad.shape
    n = w_ada.shape[1]
    tn = 512
    return pl.pallas_call(
        _ada_kernel,
        grid=(n // tn,),
        in_specs=[
            pl.BlockSpec((rows, d), lambda j: (0, 0)),
            pl.BlockSpec((d, tn), lambda j: (0, j)),
            pl.BlockSpec((1, tn), lambda j: (0, j)),
        ],
        out_specs=pl.BlockSpec((rows, tn), lambda j: (0, j)),
        out_shape=jax.ShapeDtypeStruct((rows, n), F32),
        compiler_params=pltpu.CompilerParams(dimension_semantics=("arbitrary",)),
        name="ada",
    )(c_pad, w_ada, b_ada)


def _modulated_norm(x_ref, shift_ref, scale_ref, nw_ref):
    x = x_ref[0]
    h = _rms(x, nw_ref[...]) * (1.0 + scale_ref[0]) + shift_ref[0]
    return h.astype(BF16)


def _wprep_kernel(wt_ref, out_ref):
    blk = wt_ref[...].astype(BF16)
    n = blk.shape[1]
    eye = (lax.broadcasted_iota(jnp.int32, (n, n), 0)
           == lax.broadcasted_iota(jnp.int32, (n, n), 1)).astype(BF16)
    out_ref[...] = lax.dot_general(eye, blk, (((1,), (1,)), ((), ())),
                                   preferred_element_type=F32).astype(BF16)


def _wprep(w_in_t):
    d = w_in_t.shape[1]

    def first_row(g):
        row = jnp.where(g < G_BGATE, g * d, IN_COL_BGATE + (g - G_BGATE) * d)
        return pl.multiple_of(row, math.gcd(d, IN_COL_BGATE))

    return pl.pallas_call(
        _wprep_kernel,
        grid=(N_WIDE_GROUPS,),
        in_specs=[pl.BlockSpec((pl.Element(d), pl.Element(d)), lambda g: (first_row(g), 0))],
        out_specs=pl.BlockSpec((d, d), lambda g: (0, g)),
        out_shape=jax.ShapeDtypeStruct((d, N_WIDE_GROUPS * d), BF16),
        compiler_params=pltpu.CompilerParams(dimension_semantics=("arbitrary",)),
        name="wprep",
    )(w_in_t)


def _gateconv_kernel(x_ref, shift_ref, scale_ref, nw_ref, w_ref, cw_ref, cb_ref,
                     cv_ref, sa_ref, sb_ref, ga_ref, gb_ref, sh_ref, hb_ref):
    k = pl.program_id(1)
    n_lc = CONV_WIDTH // CONV_LANES
    first = HALO - (CONV_KERNEL - 1)
    groups = RC // SUBLANES
    streams = range(GC_STREAMS)

    @pl.when(k == 0)
    def _():
        for st in streams:
            for lc in range(n_lc):
                sh_ref[st, lc, 0, 0:HALO, :] = jnp.zeros((HALO, CONV_LANES), F32)

    @pl.when(k > 0)
    def _():
        for st in streams:
            for lc in range(n_lc):
                sh_ref[st, lc, 0, 0:HALO, :] = sh_ref[st, lc, 0, TC:TC + HALO, :]

    def proj(st, group, piece):
        col = group * D_MODEL + piece * PROJ_COLS
        return jnp.dot(hb_ref[st], w_ref[:, col:col + PROJ_COLS], preferred_element_type=F32)

    for st in streams:
        h = _rms(x_ref[st], nw_ref[...]) * (1.0 + scale_ref[st]) + shift_ref[st]
        hb_ref[st] = h.astype(BF16)
        for piece in range(D_MODEL // PROJ_COLS):
            u = proj(st, G_VAL, piece) * _sigmoid(proj(st, G_GLU, piece))
            for j in range(PROJ_COLS // CONV_LANES):
                lc = piece * (PROJ_COLS // CONV_LANES) + j
                sh_ref[st, lc, 0, HALO:, :] = u[:, j * CONV_LANES:(j + 1) * CONV_LANES]

    def gate_piece(st, group, out_ref, act, piece):
        cols = slice(piece * PROJ_COLS, (piece + 1) * PROJ_COLS)
        out_ref[st, :, cols] = act(proj(st, group, piece)).astype(BF16)

    def shift_chunk(st, lc):
        for r in range(1, SUBLANES):
            for rb in range(0, SHIFT_ROWS, SHIFT_BLOCK):
                sh_ref[st, lc, r, rb:rb + SHIFT_BLOCK, :] = (
                    sh_ref[st, lc, 0, rb + r:rb + r + SHIFT_BLOCK, :])

    def conv_chunk(st, lc, c):
        row0 = c * RC
        lanes = slice(lc * CONV_LANES, (lc + 1) * CONV_LANES)
        acc = jnp.zeros((groups, SUBLANES, CONV_LANES), F32)
        for r in range(SUBLANES):
            offs = [o for o in range(first, first + CONV_KERNEL) if o % SUBLANES == r]
            g0, g1 = offs[0] // SUBLANES, offs[-1] // SUBLANES
            span = groups + g1 - g0
            start = row0 + g0 * SUBLANES
            xs = sh_ref[st, lc, r, start:start + span * SUBLANES, :]
            xs = xs.reshape(span, SUBLANES, CONV_LANES)
            for o in offs:
                g = o // SUBLANES - g0
                acc = acc + cw_ref[lc, o - first][None] * xs[g:g + groups]
        cv = acc.reshape(RC, CONV_LANES) + cb_ref[:, lanes]
        cv_ref[st, row0:row0 + RC, lanes] = cv.astype(BF16)

    def stream_work(st):
        mxu_work, vpu_work, text = [], [], []
        for group, out_ref, act in ((G_AGATE, sa_ref, _silu), (G_BGATE, sb_ref, _silu),
                                    (G_GA, ga_ref, _sigmoid), (G_GB, gb_ref, _sigmoid)):
            mxu_work += [functools.partial(gate_piece, st, group, out_ref, act, p)
                         for p in range(D_MODEL // PROJ_COLS)]
        for lc in range(n_lc):
            vpu_work.append(functools.partial(shift_chunk, st, lc))
            vpu_work += [functools.partial(conv_chunk, st, lc, c) for c in range(TC // RC)]
        issued = 0
        for i, mxu_fn in enumerate(mxu_work):
            target = (len(vpu_work) * (i + 1)) // len(mxu_work)
            text.append([mxu_fn] + vpu_work[issued:target])
            issued = target
        return text

    @pl.when(k >= 0)
    def _():
        for parts in zip(*[stream_work(st) for st in streams]):
            for part in parts:
                for fn in part:
                    fn()


def _gateconv(x, shift, scale, norm_w, w_wide, conv_w, conv_b):
    b, s, d = x.shape
    n_lc = d // CONV_LANES
    gs = GC_STREAMS
    tok = pl.BlockSpec((gs, TC, d), lambda bi, k: (bi, k, 0))
    mod = pl.BlockSpec((gs, 1, d), lambda bi, k: (bi, 0, 0))
    out = jax.ShapeDtypeStruct((b, s, d), BF16)
    return pl.pallas_call(
        _gateconv_kernel,
        grid=(b // gs, s // TC),
        in_specs=[tok, mod, mod, _const_spec((1, d)), _const_spec(w_wide.shape),
                  _const_spec(conv_w.shape), _const_spec((1, d))],
        out_specs=[tok] * 5,
        out_shape=[out] * 5,
        scratch_shapes=[pltpu.VMEM((gs, n_lc, SUBLANES, HALO + TC, CONV_LANES), F32),
                        pltpu.VMEM((gs, TC, d), BF16)],
        compiler_params=pltpu.CompilerParams(
            dimension_semantics=("arbitrary", "arbitrary"), vmem_limit_bytes=VMEM_LIMIT),
        name="gateconv",
    )(x, shift, scale, norm_w, w_wide, conv_w, conv_b)


def _rope_kernel(pos_ref, invf_ref, table_ref):
    ang = pos_ref[...] * invf_ref[...]
    cos_d, sin_d = jnp.cos(ang), jnp.sin(ang)
    n_freq = LANES // ROPE_PACK
    quarter = lax.broadcasted_iota(jnp.int32, ang.shape, 1) // n_freq
    for p in range(ROPE_PACK):
        def placed(src, q):
            return pltpu.roll(src, ((q - p) * n_freq) % LANES, 1)
        row = jnp.where(quarter == 0, placed(cos_d, 0),
                        jnp.where(quarter == 1, placed(cos_d, 1),
                                  jnp.where(quarter == 2, placed(sin_d, 2), placed(sin_d, 3))))
        table_ref[pl.ds(p, ang.shape[0], stride=ROPE_PACK), :] = row


def _rope_table(pos_dense, invf):
    rows = pos_dense.shape[0]
    tr = 512
    return pl.pallas_call(
        _rope_kernel,
        grid=(rows // tr,),
        in_specs=[pl.BlockSpec((tr, LANES), lambda i: (i, 0)), _const_spec(invf.shape)],
        out_specs=pl.BlockSpec((ROPE_PACK * tr, LANES), lambda i: (i, 0)),
        out_shape=jax.ShapeDtypeStruct((ROPE_PACK * rows, LANES), F32),
        compiler_params=pltpu.CompilerParams(dimension_semantics=("arbitrary",)),
        name="rope",
    )(pos_dense, invf)


def _mla_kernel(x_ref, shift_ref, scale_ref, nw_ref, table_ref, w_ref,
                qn_ref, kvn_ref, wuq_ref, wuk_ref, wvt_ref,
                q_ref, k_ref, vt_ref):
    hb = _modulated_norm(x_ref, shift_ref, scale_ref, nw_ref)
    lat = lax.dot_general(hb, w_ref[...], (((1,), (1,)), ((), ())),
                          preferred_element_type=F32)
    cq = lat[:, :Q_LORA_RANK]
    ckv = lat[:, Q_LORA_RANK:Q_LORA_RANK + KV_LORA_RANK]
    kp = lat[:, Q_LORA_RANK + KV_LORA_RANK:]

    table = table_ref[0]
    lane = lax.broadcasted_iota(jnp.int32, table.shape, 1)
    first_half = lane < QK_ROPE_DIM

    def rope(t):
        return jnp.where(first_half, t + pltpu.roll(t, QK_ROPE_DIM, 1), 0.0)

    cqn = _rms(cq, qn_ref[...]).astype(BF16)
    ckvn = _rms(ckv, kvn_ref[...]).astype(BF16)

    qa = jnp.dot(cqn, wuq_ref[...], preferred_element_type=F32)
    kn = jnp.dot(ckvn, wuk_ref[...], preferred_element_type=F32)
    vt = lax.dot_general(wvt_ref[...], ckvn, (((1,), (1,)), ((), ())),
                         preferred_element_type=F32)

    k_hi = rope(kp * table).astype(BF16)
    q_table = table * Q_SCALE
    for h in range(N_HEADS):
        blk = qa[:, h * QK_PAD:(h + 1) * QK_PAD]
        q_ref[0, h, :, :LANES] = (blk[:, :LANES] * Q_SCALE).astype(BF16)
        q_ref[0, h, :, LANES:] = rope(blk[:, LANES:] * q_table).astype(BF16)
        k_ref[0, h, :, :LANES] = kn[:, h * LANES:(h + 1) * LANES].astype(BF16)
        k_ref[0, h, :, LANES:] = k_hi
    ones = jnp.ones((V_PAD - V_DIM, TK), BF16)
    for j in range(TM // TK):
        for h in range(N_HEADS):
            vt_ref[0, j, h, :V_DIM, :] = vt[h * V_DIM:(h + 1) * V_DIM, j * TK:(j + 1) * TK].astype(BF16)
            vt_ref[0, j, h, V_DIM:, :] = ones


def _mla_proj(x, shift, scale, norm_w, table, w_lat_t, qn_w, kvn_w, w_uq2, w_uk, w_vt):
    b, s, d = x.shape
    tok = pl.BlockSpec((1, TM, d), lambda bi, i: (bi, i, 0))
    mod = pl.BlockSpec((1, 1, d), lambda bi, i: (bi, 0, 0))
    qk_spec = pl.BlockSpec((1, N_HEADS, TM, QK_PAD), lambda bi, i: (bi, 0, i, 0))
    return pl.pallas_call(
        _mla_kernel,
        grid=(b, s // TM),
        in_specs=[
            tok, mod, mod, _const_spec((1, d)),
            pl.BlockSpec((1, TM, LANES), lambda bi, i: (bi, i, 0)),
            _const_spec(w_lat_t.shape),
            _const_spec(qn_w.shape), _const_spec(kvn_w.shape),
            _const_spec(w_uq2.shape), _const_spec(w_uk.shape), _const_spec(w_vt.shape),
        ],
        out_specs=[
            qk_spec, qk_spec,
            pl.BlockSpec((1, TM // TK, N_HEADS, V_PAD, TK), lambda bi, i: (bi, i, 0, 0, 0)),
        ],
        out_shape=[
            jax.ShapeDtypeStruct((b, N_HEADS, s, QK_PAD), BF16),
            jax.ShapeDtypeStruct((b, N_HEADS, s, QK_PAD), BF16),
            jax.ShapeDtypeStruct((b, s // TK, N_HEADS, V_PAD, TK), BF16),
        ],
        compiler_params=pltpu.CompilerParams(
            dimension_semantics=("arbitrary", "arbitrary"), vmem_limit_bytes=VMEM_LIMIT),
        name="mla_proj",
    )(x, shift, scale, norm_w, table, w_lat_t, qn_w, kvn_w, w_uq2, w_uk, w_vt)


ATTN_BUFFERS = 3
ATTN_AHEAD = 2
ATTN_GROUP = 3
ATTN_HEADS = 2


def _attn_scores(q_ref, k_ref, bias_ref, hh, item, s_out, bm_out, diagonal):
    qn, jn = item
    k = k_ref[0, hh, pl.ds(pl.multiple_of(jn * TK, TK), TK), :]
    q = q_ref[0, hh, pl.ds(pl.multiple_of(qn * TQ, TQ), TQ), :]
    s = lax.dot_general(k, q, (((1,), (1,)), ((), ())), preferred_element_type=F32)
    if diagonal:
        s = s + bias_ref[...]
    s_out[...] = s
    bm_out[...] = jnp.max(s, axis=0, keepdims=True)


def _attn_accumulate(vt_ref, o_ref, acc_ref, hh, item, p_in, al_in, diagonal):
    qp, jp = item
    pv = jnp.dot(vt_ref[0, jp, hh], p_in[...], preferred_element_type=F32)
    acc = al_in[...] * acc_ref[qp] + pv
    acc_ref[qp] = acc
    if diagonal:
        rows = pl.ds(pl.multiple_of(qp * TQ, TQ), TQ)
        out = acc[:V_DIM] / acc[V_DIM:V_DIM + 1]
        o_ref[0, rows, hh * V_DIM:(hh + 1) * V_DIM] = jnp.transpose(out).astype(BF16)


def _attn_step(q_ref, k_ref, vt_ref, bias_ref, o_ref, acc_ref, m_ref, hh, cur, nxt, prv, items,
               diag):
    s_cur, p_cur, bm_cur, al_cur = cur
    s_nxt, _, bm_nxt, _ = nxt
    _, p_prv, _, al_prv = prv
    item_next, (qc, jc), item_prev = items
    diag_next, diag_prev = diag
    _attn_accumulate(vt_ref, o_ref, acc_ref, hh, item_prev, p_prv, al_prv, diag_prev)
    m_old = jnp.where(jc == 0, -jnp.inf, m_ref[qc])
    m_new = jnp.maximum(m_old, bm_cur[...])
    p_cur[...] = jnp.exp2(s_cur[...] - m_new).astype(BF16)
    al_cur[...] = jnp.exp2(m_old - m_new)
    m_ref[qc] = m_new
    _attn_scores(q_ref, k_ref, bias_ref, hh, item_next, s_nxt, bm_nxt, diag_next)


def _attn_item(t, n_lower):
    def lower(t):
        q = jnp.int32(1)
        for c in range(2, 64):
            if c * (c - 1) // 2 >= n_lower:
                break
            q = q + (t >= c * (c - 1) // 2).astype(jnp.int32)
        return q, t - lax.shift_right_logical(q * (q - 1), 1)
    ql, jl = lower(jnp.minimum(t, n_lower - 1))
    d = t - n_lower
    is_diag = t >= n_lower
    return jnp.where(is_diag, d, ql), jnp.where(is_diag, d, jl)


def _attn_kernel(q_ref, k_ref, vt_ref, bias_ref, o_ref, *bufs):
    n_q = q_ref.shape[2] // TQ
    n_lower = n_q * (n_q - 1) // 2
    n_items = n_lower + n_q
    nbuf = ATTN_BUFFERS
    heads = range(ATTN_HEADS)
    assert TQ == TK and ATTN_GROUP % nbuf == 0 and n_items % nbuf == 0
    per_head = len(bufs) // ATTN_HEADS
    accs = tuple(bufs[hh * per_head] for hh in heads)
    maxs = tuple(bufs[hh * per_head + 1] for hh in heads)
    rings = tuple(tuple(bufs[hh * per_head + 2 + 4 * r:hh * per_head + 6 + 4 * r]
                        for r in range(nbuf)) for hh in heads)

    def item(t):
        return _attn_item(jnp.clip(t, 0, n_items - 1), n_lower)

    def steps(t0, count, t0_static=None):
        for u in range(count):
            t = t0 + u
            diag = (False, False) if t0_static is None else (
                t0_static + u + ATTN_AHEAD >= n_lower, t0_static + u - 1 >= n_lower)
            items = (item(t + ATTN_AHEAD), item(t), item(t - 1))
            for hh in heads:
                ring = rings[hh]
                _attn_step(q_ref, k_ref, vt_ref, bias_ref, o_ref, accs[hh], maxs[hh], hh,
                           ring[u % nbuf], ring[(u + ATTN_AHEAD) % nbuf], ring[(u - 1) % nbuf],
                           items, diag)

    for hh in heads:
        accs[hh][...] = jnp.zeros_like(accs[hh])
        maxs[hh][...] = jnp.zeros_like(maxs[hh])
        _, p_last, _, al_last = rings[hh][nbuf - 1]
        p_last[...] = jnp.zeros_like(p_last)
        al_last[...] = jnp.ones_like(al_last)
        for a in range(ATTN_AHEAD):
            s_buf, _, bm_buf, _ = rings[hh][a]
            _attn_scores(q_ref, k_ref, bias_ref, hh, item(jnp.int32(a)), s_buf, bm_buf, False)

    n_plain = (n_lower - ATTN_AHEAD) // ATTN_GROUP
    lax.fori_loop(0, n_plain, lambda g, c: (steps(g * ATTN_GROUP, ATTN_GROUP), c)[1], 0)
    t_tail = n_plain * ATTN_GROUP
    steps(jnp.int32(t_tail), n_items - t_tail, t_tail)
    for hh in heads:
        _, p_last, _, al_last = rings[hh][(n_items - 1) % nbuf]
        _attn_accumulate(vt_ref, o_ref, accs[hh], hh, item(jnp.int32(n_items - 1)), p_last,
                         al_last, True)


def _attention(q, k, vt):
    b, h, s, _ = q.shape
    n_q = s // TQ
    row = lax.broadcasted_iota(jnp.int32, (TK, TQ), 0)
    col = lax.broadcasted_iota(jnp.int32, (TK, TQ), 1)
    bias = jnp.where(row <= col, 0.0, -jnp.inf).astype(F32)
    stat = pltpu.VMEM((1, TQ), F32)
    slot = [pltpu.VMEM((TK, TQ), F32), pltpu.VMEM((TK, TQ), BF16), stat, stat]
    head_scratch = ([pltpu.VMEM((n_q, V_PAD, TQ), F32), pltpu.VMEM((n_q, 1, TQ), F32)]
                    + slot * ATTN_BUFFERS)
    hs = ATTN_HEADS
    return pl.pallas_call(
        _attn_kernel,
        grid=(b, h // hs),
        in_specs=[
            pl.BlockSpec((1, hs, s, QK_PAD), lambda bi, hi: (bi, hi, 0, 0)),
            pl.BlockSpec((1, hs, s, QK_PAD), lambda bi, hi: (bi, hi, 0, 0)),
            pl.BlockSpec((1, s // TK, hs, V_PAD, TK), lambda bi, hi: (bi, 0, hi, 0, 0)),
            _const_spec(bias.shape),
        ],
        out_specs=pl.BlockSpec((1, s, hs * V_DIM), lambda bi, hi: (bi, 0, hi)),
        out_shape=jax.ShapeDtypeStruct((b, s, h * V_DIM), BF16),
        scratch_shapes=head_scratch * hs,
        compiler_params=pltpu.CompilerParams(
            dimension_semantics=("arbitrary", "arbitrary"),
            vmem_limit_bytes=VMEM_LIMIT),
        name="attn",
    )(q, k, vt, bias)


def _out_kernel(cv_ref, sa_ref, o_ref, sb_ref, ga_ref, gb_ref, x_ref, gate_ref, lw_ref, lb_ref,
                wco_ref, wao_ref, wo_ref, fw_ref, y_ref):
    for st in range(OUT_STREAMS):
        cv = cv_ref[st].astype(F32)
        mu = jnp.mean(cv, axis=-1, keepdims=True)
        xc = cv - mu
        var = jnp.mean(xc * xc, axis=-1, keepdims=True)
        ln = xc * lax.rsqrt(var + EPS) * lw_ref[...] + lb_ref[...]
        ug = (_silu(ln) * sa_ref[st].astype(F32)).astype(BF16)
        ya = jnp.dot(ug, wco_ref[...], preferred_element_type=F32)
        og = (o_ref[st].astype(F32) * sb_ref[st].astype(F32)).astype(BF16)
        yb = jnp.dot(og, wao_ref[...], preferred_element_type=F32)
        merged = ga_ref[st].astype(F32) * ya + gb_ref[st].astype(F32) * yb
        delta = jnp.dot(merged.astype(BF16), wo_ref[...], preferred_element_type=F32)
        y = x_ref[st] + gate_ref[st] * delta
        y_ref[st] = _rms(y, fw_ref[...])


def _out(cv, sa, o, sb, ga, gb, x, gate, ln_w, ln_b, w_co, w_ao, w_o, fw):
    b, s, d = x.shape
    tok = pl.BlockSpec((OUT_STREAMS, TO, d), lambda bi, i: (bi, i, 0))
    mod = pl.BlockSpec((OUT_STREAMS, 1, d), lambda bi, i: (bi, 0, 0))
    wspec = _const_spec((d, d))
    vec = _const_spec((1, d))
    return pl.pallas_call(
        _out_kernel,
        grid=(b // OUT_STREAMS, s // TO),
        in_specs=[tok, tok, tok, tok, tok, tok, tok, mod, vec, vec, wspec, wspec, wspec, vec],
        out_specs=tok,
        out_shape=jax.ShapeDtypeStruct((b, s, d), F32),
        compiler_params=pltpu.CompilerParams(
            dimension_semantics=("arbitrary", "arbitrary"), vmem_limit_bytes=VMEM_LIMIT),
        name="out",
    )(cv, sa, o, sb, ga, gb, x, gate, ln_w, ln_b, w_co, w_ao, w_o, fw)


def _rot_cols(w):
    half = w.shape[-1] // 2
    return jnp.concatenate([-w[..., half:], w[..., :half]], axis=-1)


def _layer(x, c_act_pad, table, w_ada, b_ada, norm_w, w_in, conv_w, conv_b, conv_ln_w,
           conv_ln_b, w_conv_out, q_norm_w, w_uq, kv_norm_w, w_ukv, w_attn_out, w_out, final_w):
    b, s, d = x.shape
    mod = _ada(c_act_pad, w_ada, b_ada[None, :])[:b]
    shift, scale, gate = (m[:, None, :] for m in jnp.split(mod, 3, axis=-1))

    assert w_in.shape[1] == IN_COLS
    w_in_t = w_in.T
    w_wide = _wprep(w_in_t)
    w_kpe_t = w_in_t[IN_COL_BGATE - QK_ROPE_DIM:IN_COL_BGATE]
    half = QK_ROPE_DIM // 2
    w_lat_t = jnp.concatenate([w_in_t[IN_COL_LATENT:IN_COL_BGATE], -w_kpe_t[half:], w_kpe_t[:half]],
                              axis=0)
    w_lat_t = lax.optimization_barrier(w_lat_t).astype(BF16)

    wq = w_uq.reshape(Q_LORA_RANK, N_HEADS, QK_NOPE_DIM + QK_ROPE_DIM)
    wq_rope = wq[..., QK_NOPE_DIM:]
    w_uq2 = jnp.concatenate([wq, _rot_cols(wq_rope)], axis=-1)
    w_uq2 = w_uq2.reshape(Q_LORA_RANK, N_HEADS * QK_PAD).astype(BF16)
    wkv = w_ukv.reshape(KV_LORA_RANK, N_HEADS, QK_NOPE_DIM + V_DIM)
    w_uk = wkv[..., :QK_NOPE_DIM].reshape(KV_LORA_RANK, N_HEADS * QK_NOPE_DIM).astype(BF16)
    w_vt = wkv[..., QK_NOPE_DIM:].reshape(KV_LORA_RANK, ATTN_WIDTH).T.astype(BF16)

    nw = norm_w[None, :]
    cw = conv_w.reshape(CONV_KERNEL, CONV_WIDTH // CONV_LANES, 1, CONV_LANES).transpose(1, 0, 2, 3)
    cw = jnp.broadcast_to(cw, (CONV_WIDTH // CONV_LANES, CONV_KERNEL, SUBLANES, CONV_LANES))
    cv, sa, sb, ga, gb = _gateconv(x, shift, scale, nw, w_wide, cw, conv_b[None, :])
    q, k, vt = _mla_proj(x, shift, scale, nw, table, w_lat_t, q_norm_w[None, :],
                         kv_norm_w[None, :], w_uq2, w_uk, w_vt)
    o = _attention(q, k, vt)
    return _out(cv, sa, o, sb, ga, gb, x, gate, conv_ln_w[None, :], conv_ln_b[None, :],
                w_conv_out.astype(BF16), w_attn_out.astype(BF16), w_out.astype(BF16),
                final_w[None, :])


def kernel(x, c, positions, w_ada, b_ada, norm_w, w_in, conv_w, conv_b, conv_ln_w, conv_ln_b,
           w_conv_out, q_norm_w, w_uq, kv_norm_w, w_ukv, w_attn_out, w_out, final_norm_w):
    b, s, d = x.shape
    depth = w_ada.shape[0]
    assert depth == 1, "final rmsnorm is fused into the single layer's output kernel"
    inv_freq = ROPE_THETA ** (-jnp.arange(0, QK_ROPE_DIM, 2, dtype=F32) / QK_ROPE_DIM)
    n_freq = inv_freq.shape[0]
    assert ROPE_PACK * n_freq == LANES
    invf = jnp.tile(inv_freq, ROPE_PACK)[None, :]
    pos_dense = jnp.repeat(positions.astype(F32).reshape(b * s // ROPE_PACK, ROPE_PACK), n_freq, axis=1)
    table = _rope_table(pos_dense, invf).reshape(b, s, LANES)
    c_pad = jnp.pad(c, ((0, 8 - b), (0, 0)))
    return _layer(x, c_pad, table, w_ada[0], b_ada[0], norm_w[0], w_in[0], conv_w[0],
                  conv_b[0], conv_ln_w[0], conv_ln_b[0], w_conv_out[0], q_norm_w[0], w_uq[0],
                  kv_norm_w[0], w_ukv[0], w_attn_out[0], w_out[0], final_norm_w)
```

```python
import functools
import math

import jax
import jax.numpy as jnp
from jax import lax
from jax.experimental import pallas as pl
from jax.experimental.pallas import tpu as pltpu

F32 = jnp.float32
BF16 = jnp.bfloat16

D_MODEL = 1024
CONV_WIDTH = 1024
CONV_KERNEL = 31
N_HEADS = 8
QK_NOPE_DIM = 128
QK_ROPE_DIM = 64
V_DIM = 128
Q_LORA_RANK = 256
KV_LORA_RANK = 256
ATTN_WIDTH = N_HEADS * V_DIM
ROPE_THETA = 10000.0
EPS = 1e-6

LANES = 128
QK_PAD = 2 * LANES
BF16_SUBLANES = 16
V_PAD = V_DIM + BF16_SUBLANES
ROPE_PACK = LANES // (QK_ROPE_DIM // 2)
HALO = 32
VMEM_LIMIT = 56 * 1024 * 1024

Q_SCALE = (QK_NOPE_DIM + QK_ROPE_DIM) ** -0.5 * math.log2(math.e)

IN_COL_LATENT = 3 * CONV_WIDTH
IN_COL_BGATE = IN_COL_LATENT + Q_LORA_RANK + KV_LORA_RANK + QK_ROPE_DIM
IN_COLS = IN_COL_BGATE + ATTN_WIDTH + 2 * D_MODEL
G_VAL, G_GLU, G_AGATE, G_BGATE, G_GA, G_GB = range(6)
N_WIDE_GROUPS = 6
assert CONV_WIDTH == ATTN_WIDTH == D_MODEL

TM = 1024
TC = 256
RC = 64
CONV_LANES = 128
PROJ_COLS = 512
GC_STREAMS = 2
TO = 512
OUT_STREAMS = 1
SUBLANES = 8
SHIFT_ROWS = TC + HALO - SUBLANES
SHIFT_BLOCK = 40
assert SHIFT_ROWS % SHIFT_BLOCK == 0 and SHIFT_BLOCK % SUBLANES == 0
TQ = 512
TK = 512


def _sigmoid(x):
    return 1.0 / (1.0 + jnp.exp(-x))


def _silu(x):
    return x * _sigmoid(x)


def _rms(x, w):
    return x * lax.rsqrt(jnp.mean(x * x, axis=-1, keepdims=True) + EPS) * w


def _const_spec(shape):
    return pl.BlockSpec(shape, lambda *_: (0,) * len(shape), pipeline_mode=pl.Buffered(1))


def _ada_kernel(c_ref, w_ref, b_ref, o_ref):
    c = c_ref[...]
    o_ref[...] = jnp.dot(_silu(c), w_ref[...], preferred_element_type=F32) + b_ref[...]


def _ada(c_pad, w_ada, b_ada):
    rows, d = c_pad.shape
    n = w_ada.shape[1]
    tn = 512
    return pl.pallas_call(
        _ada_kernel,
        grid=(n // tn,),
        in_specs=[
            pl.BlockSpec((rows, d), lambda j: (0, 0)),
            pl.BlockSpec((d, tn), lambda j: (0, j)),
            pl.BlockSpec((1, tn), lambda j: (0, j)),
        ],
        out_specs=pl.BlockSpec((rows, tn), lambda j: (0, j)),
        out_shape=jax.ShapeDtypeStruct((rows, n), F32),
        compiler_params=pltpu.CompilerParams(dimension_semantics=("arbitrary",)),
        name="ada",
    )(c_pad, w_ada, b_ada)


def _modulated_norm(x_ref, shift_ref, scale_ref, nw_ref):
    x = x_ref[0]
    h = _rms(x, nw_ref[...]) * (1.0 + scale_ref[0]) + shift_ref[0]
    return h.astype(BF16)


def _wprep_kernel(wt_ref, out_ref):
    blk = wt_ref[...].astype(BF16)
    n = blk.shape[1]
    eye = (lax.broadcasted_iota(jnp.int32, (n, n), 0)
           == lax.broadcasted_iota(jnp.int32, (n, n), 1)).astype(BF16)
    out_ref[...] = lax.dot_general(eye, blk, (((1,), (1,)), ((), ())),
                                   preferred_element_type=F32).astype(BF16)


def _wprep(w_in_t):
    d = w_in_t.shape[1]

    def first_row(g):
        row = jnp.where(g < G_BGATE, g * d, IN_COL_BGATE + (g - G_BGATE) * d)
        return pl.multiple_of(row, math.gcd(d, IN_COL_BGATE))

    return pl.pallas_call(
        _wprep_kernel,
        grid=(N_WIDE_GROUPS,),
        in_specs=[pl.BlockSpec((pl.Element(d), pl.Element(d)), lambda g: (first_row(g), 0))],
        out_specs=pl.BlockSpec((d, d), lambda g: (0, g)),
        out_shape=jax.ShapeDtypeStruct((d, N_WIDE_GROUPS * d), BF16),
        compiler_params=pltpu.CompilerParams(dimension_semantics=("arbitrary",)),
        name="wprep",
    )(w_in_t)


def _gateconv_kernel(x_ref, shift_ref, scale_ref, nw_ref, w_ref, cw_ref, cb_ref,
                     cv_ref, sa_ref, sb_ref, ga_ref, gb_ref, sh_ref, hb_ref):
    k = pl.program_id(1)
    n_lc = CONV_WIDTH // CONV_LANES
    first = HALO - (CONV_KERNEL - 1)
    groups = RC // SUBLANES
    streams = range(GC_STREAMS)

    @pl.when(k == 0)
    def _():
        for st in streams:
            for lc in range(n_lc):
                sh_ref[st, lc, 0, 0:HALO, :] = jnp.zeros((HALO, CONV_LANES), F32)

    @pl.when(k > 0)
    def _():
        for st in streams:
            for lc in range(n_lc):
                sh_ref[st, lc, 0, 0:HALO, :] = sh_ref[st, lc, 0, TC:TC + HALO, :]

    def proj(st, group, piece):
        col = group * D_MODEL + piece * PROJ_COLS
        return jnp.dot(hb_ref[st], w_ref[:, col:col + PROJ_COLS], preferred_element_type=F32)

    for st in streams:
        h = _rms(x_ref[st], nw_ref[...]) * (1.0 + scale_ref[st]) + shift_ref[st]
        hb_ref[st] = h.astype(BF16)
        for piece in range(D_MODEL // PROJ_COLS):
            u = proj(st, G_VAL, piece) * _sigmoid(proj(st, G_GLU, piece))
            for j in range(PROJ_COLS // CONV_LANES):
                lc = piece * (PROJ_COLS // CONV_LANES) + j
                sh_ref[st, lc, 0, HALO:, :] = u[:, j * CONV_LANES:(j + 1) * CONV_LANES]

    def gate_piece(st, group, out_ref, act, piece):
        cols = slice(piece * PROJ_COLS, (piece + 1) * PROJ_COLS)
        out_ref[st, :, cols] = act(proj(st, group, piece)).astype(BF16)

    def shift_chunk(st, lc):
        for r in range(1, SUBLANES):
            for rb in range(0, SHIFT_ROWS, SHIFT_BLOCK):
                sh_ref[st, lc, r, rb:rb + SHIFT_BLOCK, :] = (
                    sh_ref[st, lc, 0, rb + r:rb + r + SHIFT_BLOCK, :])

    def conv_chunk(st, lc, c):
        row0 = c * RC
        lanes = slice(lc * CONV_LANES, (lc + 1) * CONV_LANES)
        acc = jnp.zeros((groups, SUBLANES, CONV_LANES), F32)
        for r in range(SUBLANES):
            offs = [o for o in range(first, first + CONV_KERNEL) if o % SUBLANES == r]
            g0, g1 = offs[0] // SUBLANES, offs[-1] // SUBLANES
            span = groups + g1 - g0
            start = row0 + g0 * SUBLANES
            xs = sh_ref[st, lc, r, start:start + span * SUBLANES, :]
            xs = xs.reshape(span, SUBLANES, CONV_LANES)
            for o in offs:
                g = o // SUBLANES - g0
                acc = acc + cw_ref[lc, o - first][None] * xs[g:g + groups]
        cv = acc.reshape(RC, CONV_LANES) + cb_ref[:, lanes]
        cv_ref[st, row0:row0 + RC, lanes] = cv.astype(BF16)

    def stream_work(st):
        mxu_work, vpu_work, text = [], [], []
        for group, out_ref, act in ((G_AGATE, sa_ref, _silu), (G_BGATE, sb_ref, _silu),
                                    (G_GA, ga_ref, _sigmoid), (G_GB, gb_ref, _sigmoid)):
            mxu_work += [functools.partial(gate_piece, st, group, out_ref, act, p)
                         for p in range(D_MODEL // PROJ_COLS)]
        for lc in range(n_lc):
            vpu_work.append(functools.partial(shift_chunk, st, lc))
            vpu_work += [functools.partial(conv_chunk, st, lc, c) for c in range(TC // RC)]
        issued = 0
        for i, mxu_fn in enumerate(mxu_work):
            target = (len(vpu_work) * (i + 1)) // len(mxu_work)
            text.append([mxu_fn] + vpu_work[issued:target])
            issued = target
        return text

    @pl.when(k >= 0)
    def _():
        for parts in zip(*[stream_work(st) for st in streams]):
            for part in parts:
                for fn in part:
                    fn()


def _gateconv(x, shift, scale, norm_w, w_wide, conv_w, conv_b):
    b, s, d = x.shape
    n_lc = d // CONV_LANES
    gs = GC_STREAMS
    tok = pl.BlockSpec((gs, TC, d), lambda bi, k: (bi, k, 0))
    mod = pl.BlockSpec((gs, 1, d), lambda bi, k: (bi, 0, 0))
    out = jax.ShapeDtypeStruct((b, s, d), BF16)
    return pl.pallas_call(
        _gateconv_kernel,
        grid=(b // gs, s // TC),
        in_specs=[tok, mod, mod, _const_spec((1, d)), _const_spec(w_wide.shape),
                  _const_spec(conv_w.shape), _const_spec((1, d))],
        out_specs=[tok] * 5,
        out_shape=[out] * 5,
        scratch_shapes=[pltpu.VMEM((gs, n_lc, SUBLANES, HALO + TC, CONV_LANES), F32),
                        pltpu.VMEM((gs, TC, d), BF16)],
        compiler_params=pltpu.CompilerParams(
            dimension_semantics=("arbitrary", "arbitrary"), vmem_limit_bytes=VMEM_LIMIT),
        name="gateconv",
    )(x, shift, scale, norm_w, w_wide, conv_w, conv_b)


def _rope_kernel(pos_ref, invf_ref, table_ref):
    ang = pos_ref[...] * invf_ref[...]
    cos_d, sin_d = jnp.cos(ang), jnp.sin(ang)
    n_freq = LANES // ROPE_PACK
    quarter = lax.broadcasted_iota(jnp.int32, ang.shape, 1) // n_freq
    for p in range(ROPE_PACK):
        def placed(src, q):
            return pltpu.roll(src, ((q - p) * n_freq) % LANES, 1)
        row = jnp.where(quarter == 0, placed(cos_d, 0),
                        jnp.where(quarter == 1, placed(cos_d, 1),
                                  jnp.where(quarter == 2, placed(sin_d, 2), placed(sin_d, 3))))
        table_ref[pl.ds(p, ang.shape[0], stride=ROPE_PACK), :] = row


def _rope_table(pos_dense, invf):
    rows = pos_dense.shape[0]
    tr = 512
    return pl.pallas_call(
        _rope_kernel,
        grid=(rows // tr,),
        in_specs=[pl.BlockSpec((tr, LANES), lambda i: (i, 0)), _const_spec(invf.shape)],
        out_specs=pl.BlockSpec((ROPE_PACK * tr, LANES), lambda i: (i, 0)),
        out_shape=jax.ShapeDtypeStruct((ROPE_PACK * rows, LANES), F32),
        compiler_params=pltpu.CompilerParams(dimension_semantics=("arbitrary",)),
        name="rope",
    )(pos_dense, invf)


def _mla_kernel(x_ref, shift_ref, scale_ref, nw_ref, table_ref, w_ref,
                qn_ref, kvn_ref, wuq_ref, wuk_ref, wvt_ref,
                q_ref, k_ref, vt_ref):
    hb = _modulated_norm(x_ref, shift_ref, scale_ref, nw_ref)
    lat = lax.dot_general(hb, w_ref[...], (((1,), (1,)), ((), ())),
                          preferred_element_type=F32)
    cq = lat[:, :Q_LORA_RANK]
    ckv = lat[:, Q_LORA_RANK:Q_LORA_RANK + KV_LORA_RANK]
    kp = lat[:, Q_LORA_RANK + KV_LORA_RANK:]

    table = table_ref[0]
    lane = lax.broadcasted_iota(jnp.int32, table.shape, 1)
    first_half = lane < QK_ROPE_DIM

    def rope(t):
        return jnp.where(first_half, t + pltpu.roll(t, QK_ROPE_DIM, 1), 0.0)

    cqn = _rms(cq, qn_ref[...]).astype(BF16)
    ckvn = _rms(ckv, kvn_ref[...]).astype(BF16)

    qa = jnp.dot(cqn, wuq_ref[...], preferred_element_type=F32)
    kn = jnp.dot(ckvn, wuk_ref[...], preferred_element_type=F32)
    vt = lax.dot_general(wvt_ref[...], ckvn, (((1,), (1,)), ((), ())),
                         preferred_element_type=F32)

    k_hi = rope(kp * table).astype(BF16)
    q_table = table * Q_SCALE
    for h in range(N_HEADS):
        blk = qa[:, h * QK_PAD:(h + 1) * QK_PAD]
        q_ref[0, h, :, :LANES] = (blk[:, :LANES] * Q_SCALE).astype(BF16)
        q_ref[0, h, :, LANES:] = rope(blk[:, LANES:] * q_table).astype(BF16)
        k_ref[0, h, :, :LANES] = kn[:, h * LANES:(h + 1) * LANES].astype(BF16)
        k_ref[0, h, :, LANES:] = k_hi
    ones = jnp.ones((V_PAD - V_DIM, TK), BF16)
    for j in range(TM // TK):
        for h in range(N_HEADS):
            vt_ref[0, j, h, :V_DIM, :] = vt[h * V_DIM:(h + 1) * V_DIM, j * TK:(j + 1) * TK].astype(BF16)
            vt_ref[0, j, h, V_DIM:, :] = ones


def _mla_proj(x, shift, scale, norm_w, table, w_lat_t, qn_w, kvn_w, w_uq2, w_uk, w_vt):
    b, s, d = x.shape
    tok = pl.BlockSpec((1, TM, d), lambda bi, i: (bi, i, 0))
    mod = pl.BlockSpec((1, 1, d), lambda bi, i: (bi, 0, 0))
    qk_spec = pl.BlockSpec((1, N_HEADS, TM, QK_PAD), lambda bi, i: (bi, 0, i, 0))
    return pl.pallas_call(
        _mla_kernel,
        grid=(b, s // TM),
        in_specs=[
            tok, mod, mod, _const_spec((1, d)),
            pl.BlockSpec((1, TM, LANES), lambda bi, i: (bi, i, 0)),
            _const_spec(w_lat_t.shape),
            _const_spec(qn_w.shape), _const_spec(kvn_w.shape),
            _const_spec(w_uq2.shape), _const_spec(w_uk.shape), _const_spec(w_vt.shape),
        ],
        out_specs=[
            qk_spec, qk_spec,
            pl.BlockSpec((1, TM // TK, N_HEADS, V_PAD, TK), lambda bi, i: (bi, i, 0, 0, 0)),
        ],
        out_shape=[
            jax.ShapeDtypeStruct((b, N_HEADS, s, QK_PAD), BF16),
            jax.ShapeDtypeStruct((b, N_HEADS, s, QK_PAD), BF16),
            jax.ShapeDtypeStruct((b, s // TK, N_HEADS, V_PAD, TK), BF16),
        ],
        compiler_params=pltpu.CompilerParams(
            dimension_semantics=("arbitrary", "arbitrary"), vmem_limit_bytes=VMEM_LIMIT),
        name="mla_proj",
    )(x, shift, scale, norm_w, table, w_lat_t, qn_w, kvn_w, w_uq2, w_uk, w_vt)


ATTN_BUFFERS = 3
ATTN_AHEAD = 2
ATTN_GROUP = 3
ATTN_HEADS = 2


def _attn_scores(q_ref, k_ref, bias_ref, hh, item, s_out, bm_out, diagonal):
    qn, jn = item
    k = k_ref[0, hh, pl.ds(pl.multiple_of(jn * TK, TK), TK), :]
    q = q_ref[0, hh, pl.ds(pl.multiple_of(qn * TQ, TQ), TQ), :]
    s = lax.dot_general(k, q, (((1,), (1,)), ((), ())), preferred_element_type=F32)
    if diagonal:
        s = s + bias_ref[...]
    s_out[...] = s
    bm_out[...] = jnp.max(s, axis=0, keepdims=True)


def _attn_accumulate(vt_ref, o_ref, acc_ref, hh, item, p_in, al_in, diagonal):
    qp, jp = item
    pv = jnp.dot(vt_ref[0, jp, hh], p_in[...], preferred_element_type=F32)
    acc = al_in[...] * acc_ref[qp] + pv
    acc_ref[qp] = acc
    if diagonal:
        rows = pl.ds(pl.multiple_of(qp * TQ, TQ), TQ)
        out = acc[:V_DIM] / acc[V_DIM:V_DIM + 1]
        o_ref[0, rows, hh * V_DIM:(hh + 1) * V_DIM] = jnp.transpose(out).astype(BF16)


def _attn_step(q_ref, k_ref, vt_ref, bias_ref, o_ref, acc_ref, m_ref, hh, cur, nxt, prv, items,
               diag):
    s_cur, p_cur, bm_cur, al_cur = cur
    s_nxt, _, bm_nxt, _ = nxt
    _, p_prv, _, al_prv = prv
    item_next, (qc, jc), item_prev = items
    diag_next, diag_prev = diag
    _attn_accumulate(vt_ref, o_ref, acc_ref, hh, item_prev, p_prv, al_prv, diag_prev)
    m_old = jnp.where(jc == 0, -jnp.inf, m_ref[qc])
    m_new = jnp.maximum(m_old, bm_cur[...])
    p_cur[...] = jnp.exp2(s_cur[...] - m_new).astype(BF16)
    al_cur[...] = jnp.exp2(m_old - m_new)
    m_ref[qc] = m_new
    _attn_scores(q_ref, k_ref, bias_ref, hh, item_next, s_nxt, bm_nxt, diag_next)


def _attn_item(t, n_lower):
    def lower(t):
        q = jnp.int32(1)
        for c in range(2, 64):
            if c * (c - 1) // 2 >= n_lower:
                break
            q = q + (t >= c * (c - 1) // 2).astype(jnp.int32)
        return q, t - lax.shift_right_logical(q * (q - 1), 1)
    ql, jl = lower(jnp.minimum(t, n_lower - 1))
    d = t - n_lower
    is_diag = t >= n_lower
    return jnp.where(is_diag, d, ql), jnp.where(is_diag, d, jl)


def _attn_kernel(q_ref, k_ref, vt_ref, bias_ref, o_ref, *bufs):
    n_q = q_ref.shape[2] // TQ
    n_lower = n_q * (n_q - 1) // 2
    n_items = n_lower + n_q
    nbuf = ATTN_BUFFERS
    heads = range(ATTN_HEADS)
    assert TQ == TK and ATTN_GROUP % nbuf == 0 and n_items % nbuf == 0
    per_head = len(bufs) // ATTN_HEADS
    accs = tuple(bufs[hh * per_head] for hh in heads)
    maxs = tuple(bufs[hh * per_head + 1] for hh in heads)
    rings = tuple(tuple(bufs[hh * per_head + 2 + 4 * r:hh * per_head + 6 + 4 * r]
                        for r in range(nbuf)) for hh in heads)

    def item(t):
        return _attn_item(jnp.clip(t, 0, n_items - 1), n_lower)

    def steps(t0, count, t0_static=None):
        for u in range(count):
            t = t0 + u
            diag = (False, False) if t0_static is None else (
                t0_static + u + ATTN_AHEAD >= n_lower, t0_static + u - 1 >= n_lower)
            items = (item(t + ATTN_AHEAD), item(t), item(t - 1))
            for hh in heads:
                ring = rings[hh]
                _attn_step(q_ref, k_ref, vt_ref, bias_ref, o_ref, accs[hh], maxs[hh], hh,
                           ring[u % nbuf], ring[(u + ATTN_AHEAD) % nbuf], ring[(u - 1) % nbuf],
                           items, diag)

    for hh in heads:
        accs[hh][...] = jnp.zeros_like(accs[hh])
        maxs[hh][...] = jnp.zeros_like(maxs[hh])
        _, p_last, _, al_last = rings[hh][nbuf - 1]
        p_last[...] = jnp.zeros_like(p_last)
        al_last[...] = jnp.ones_like(al_last)
        for a in range(ATTN_AHEAD):
            s_buf, _, bm_buf, _ = rings[hh][a]
            _attn_scores(q_ref, k_ref, bias_ref, hh, item(jnp.int32(a)), s_buf, bm_buf, False)

    n_plain = (n_lower - ATTN_AHEAD) // ATTN_GROUP
    lax.fori_loop(0, n_plain, lambda g, c: (steps(g * ATTN_GROUP, ATTN_GROUP), c)[1], 0)
    t_tail = n_plain * ATTN_GROUP
    steps(jnp.int32(t_tail), n_items - t_tail, t_tail)
    for hh in heads:
        _, p_last, _, al_last = rings[hh][(n_items - 1) % nbuf]
        _attn_accumulate(vt_ref, o_ref, accs[hh], hh, item(jnp.int32(n_items - 1)), p_last,
                         al_last, True)


def _attention(q, k, vt):
    b, h, s, _ = q.shape
    n_q = s // TQ
    row = lax.broadcasted_iota(jnp.int32, (TK, TQ), 0)
    col = lax.broadcasted_iota(jnp.int32, (TK, TQ), 1)
    bias = jnp.where(row <= col, 0.0, -jnp.inf).astype(F32)
    stat = pltpu.VMEM((1, TQ), F32)
    slot = [pltpu.VMEM((TK, TQ), F32), pltpu.VMEM((TK, TQ), BF16), stat, stat]
    head_scratch = ([pltpu.VMEM((n_q, V_PAD, TQ), F32), pltpu.VMEM((n_q, 1, TQ), F32)]
                    + slot * ATTN_BUFFERS)
    hs = ATTN_HEADS
    return pl.pallas_call(
        _attn_kernel,
        grid=(b, h // hs),
        in_specs=[
            pl.BlockSpec((1, hs, s, QK_PAD), lambda bi, hi: (bi, hi, 0, 0)),
            pl.BlockSpec((1, hs, s, QK_PAD), lambda bi, hi: (bi, hi, 0, 0)),
            pl.BlockSpec((1, s // TK, hs, V_PAD, TK), lambda bi, hi: (bi, 0, hi, 0, 0)),
            _const_spec(bias.shape),
        ],
        out_specs=pl.BlockSpec((1, s, hs * V_DIM), lambda bi, hi: (bi, 0, hi)),
        out_shape=jax.ShapeDtypeStruct((b, s, h * V_DIM), BF16),
        scratch_shapes=head_scratch * hs,
        compiler_params=pltpu.CompilerParams(
            dimension_semantics=("arbitrary", "arbitrary"),
            vmem_limit_bytes=VMEM_LIMIT),
        name="attn",
    )(q, k, vt, bias)


def _out_kernel(cv_ref, sa_ref, o_ref, sb_ref, ga_ref, gb_ref, x_ref, gate_ref, lw_ref, lb_ref,
                wco_ref, wao_ref, wo_ref, fw_ref, y_ref):
    for st in range(OUT_STREAMS):
        cv = cv_ref[st].astype(F32)
        mu = jnp.mean(cv, axis=-1, keepdims=True)
        xc = cv - mu
        var = jnp.mean(xc * xc, axis=-1, keepdims=True)
        ln = xc * lax.rsqrt(var + EPS) * lw_ref[...] + lb_ref[...]
        ug = (_silu(ln) * sa_ref[st].astype(F32)).astype(BF16)
        ya = jnp.dot(ug, wco_ref[...], preferred_element_type=F32)
        og = (o_ref[st].astype(F32) * sb_ref[st].astype(F32)).astype(BF16)
        yb = jnp.dot(og, wao_ref[...], preferred_element_type=F32)
        merged = ga_ref[st].astype(F32) * ya + gb_ref[st].astype(F32) * yb
        delta = jnp.dot(merged.astype(BF16), wo_ref[...], preferred_element_type=F32)
        y = x_ref[st] + gate_ref[st] * delta
        y_ref[st] = _rms(y, fw_ref[...])


def _out(cv, sa, o, sb, ga, gb, x, gate, ln_w, ln_b, w_co, w_ao, w_o, fw):
    b, s, d = x.shape
    tok = pl.BlockSpec((OUT_STREAMS, TO, d), lambda bi, i: (bi, i, 0))
    mod = pl.BlockSpec((OUT_STREAMS, 1, d), lambda bi, i: (bi, 0, 0))
    wspec = _const_spec((d, d))
    vec = _const_spec((1, d))
    return pl.pallas_call(
        _out_kernel,
        grid=(b // OUT_STREAMS, s // TO),
        in_specs=[tok, tok, tok, tok, tok, tok, tok, mod, vec, vec, wspec, wspec, wspec, vec],
        out_specs=tok,
        out_shape=jax.ShapeDtypeStruct((b, s, d), F32),
        compiler_params=pltpu.CompilerParams(
            dimension_semantics=("arbitrary", "arbitrary"), vmem_limit_bytes=VMEM_LIMIT),
        name="out",
    )(cv, sa, o, sb, ga, gb, x, gate, ln_w, ln_b, w_co, w_ao, w_o, fw)


def _rot_cols(w):
    half = w.shape[-1] // 2
    return jnp.concatenate([-w[..., half:], w[..., :half]], axis=-1)


def _layer(x, c_act_pad, table, w_ada, b_ada, norm_w, w_in, conv_w, conv_b, conv_ln_w,
           conv_ln_b, w_conv_out, q_norm_w, w_uq, kv_norm_w, w_ukv, w_attn_out, w_out, final_w):
    b, s, d = x.shape
    mod = _ada(c_act_pad, w_ada, b_ada[None, :])[:b]
    shift, scale, gate = (m[:, None, :] for m in jnp.split(mod, 3, axis=-1))

    assert w_in.shape[1] == IN_COLS
    w_in_t = w_in.T
    w_wide = _wprep(w_in_t)
    w_kpe_t = w_in_t[IN_COL_BGATE - QK_ROPE_DIM:IN_COL_BGATE]
    half = QK_ROPE_DIM // 2
    w_lat_t = jnp.concatenate([w_in_t[IN_COL_LATENT:IN_COL_BGATE], -w_kpe_t[half:], w_kpe_t[:half]],
                              axis=0)
    w_lat_t = lax.optimization_barrier(w_lat_t).astype(BF16)

    wq = w_uq.reshape(Q_LORA_RANK, N_HEADS, QK_NOPE_DIM + QK_ROPE_DIM)
    wq_rope = wq[..., QK_NOPE_DIM:]
    w_uq2 = jnp.concatenate([wq, _rot_cols(wq_rope)], axis=-1)
    w_uq2 = w_uq2.reshape(Q_LORA_RANK, N_HEADS * QK_PAD).astype(BF16)
    wkv = w_ukv.reshape(KV_LORA_RANK, N_HEADS, QK_NOPE_DIM + V_DIM)
    w_uk = wkv[..., :QK_NOPE_DIM].reshape(KV_LORA_RANK, N_HEADS * QK_NOPE_DIM).astype(BF16)
    w_vt = wkv[..., QK_NOPE_DIM:].reshape(KV_LORA_RANK, ATTN_WIDTH).T.astype(BF16)

    nw = norm_w[None, :]
    cw = conv_w.reshape(CONV_KERNEL, CONV_WIDTH // CONV_LANES, 1, CONV_LANES).transpose(1, 0, 2, 3)
    cw = jnp.broadcast_to(cw, (CONV_WIDTH // CONV_LANES, CONV_KERNEL, SUBLANES, CONV_LANES))
    cv, sa, sb, ga, gb = _gateconv(x, shift, scale, nw, w_wide, cw, conv_b[None, :])
    q, k, vt = _mla_proj(x, shift, scale, nw, table, w_lat_t, q_norm_w[None, :],
                         kv_norm_w[None, :], w_uq2, w_uk, w_vt)
    o = _attention(q, k, vt)
    return _out(cv, sa, o, sb, ga, gb, x, gate, conv_ln_w[None, :], conv_ln_b[None, :],
                w_conv_out.astype(BF16), w_attn_out.astype(BF16), w_out.astype(BF16),
                final_w[None, :])


def kernel(x, c, positions, w_ada, b_ada, norm_w, w_in, conv_w, conv_b, conv_ln_w, conv_ln_b,
           w_conv_out, q_norm_w, w_uq, kv_norm_w, w_ukv, w_attn_out, w_out, final_norm_w):
    b, s, d = x.shape
    depth = w_ada.shape[0]
    assert depth == 1, "final rmsnorm is fused into the single layer's output kernel"
    inv_freq = ROPE_THETA ** (-jnp.arange(0, QK_ROPE_DIM, 2, dtype=F32) / QK_ROPE_DIM)
    n_freq = inv_freq.shape[0]
    assert ROPE_PACK * n_freq == LANES
    invf = jnp.tile(inv_freq, ROPE_PACK)[None, :]
    pos_dense = jnp.repeat(positions.astype(F32).reshape(b * s // ROPE_PACK, ROPE_PACK), n_freq, axis=1)
    table = _rope_table(pos_dense, invf).reshape(b, s, LANES)
    c_pad = jnp.pad(c, ((0, 8 - b), (0, 0)))
    return _layer(x, c_pad, table, w_ada[0], b_ada[0], norm_w[0], w_in[0], conv_w[0],
                  conv_b[0], conv_ln_w[0], conv_ln_b[0], w_conv_out[0], q_norm_w[0], w_uq[0],
                  kv_norm_w[0], w_ukv[0], w_attn_out[0], w_out[0], final_norm_w)
```

```python
import functools
import math

import jax
import jax.numpy as jnp
from jax import lax
from jax.experimental import pallas as pl
from jax.experimental.pallas import tpu as pltpu

F32 = jnp.float32
BF16 = jnp.bfloat16

D_MODEL = 1024
CONV_WIDTH = 1024
CONV_KERNEL = 31
N_HEADS = 8
QK_NOPE_DIM = 128
QK_ROPE_DIM = 64
V_DIM = 128
Q_LORA_RANK = 256
KV_LORA_RANK = 256
ATTN_WIDTH = N_HEADS * V_DIM
ROPE_THETA = 10000.0
EPS = 1e-6

LANES = 128
QK_PAD = 2 * LANES
BF16_SUBLANES = 16
V_PAD = V_DIM + BF16_SUBLANES
ROPE_PACK = LANES // (QK_ROPE_DIM // 2)
HALO = 32
VMEM_LIMIT = 56 * 1024 * 1024

Q_SCALE = (QK_NOPE_DIM + QK_ROPE_DIM) ** -0.5 * math.log2(math.e)

IN_COL_LATENT = 3 * CONV_WIDTH
IN_COL_BGATE = IN_COL_LATENT + Q_LORA_RANK + KV_LORA_RANK + QK_ROPE_DIM
IN_COLS = IN_COL_BGATE + ATTN_WIDTH + 2 * D_MODEL
G_VAL, G_GLU, G_AGATE, G_BGATE, G_GA, G_GB = range(6)
N_WIDE_GROUPS = 6
assert CONV_WIDTH == ATTN_WIDTH == D_MODEL

TM = 1024
TC = 256
RC = 32
CONV_LANES = 512
PROJ_COLS = 1024
GC_STREAMS = 2
TO = 512
OUT_STREAMS = 1
SUBLANES = 8
SHIFT_ROWS = TC + HALO - SUBLANES
SHIFT_BLOCK = 40
assert SHIFT_ROWS % SHIFT_BLOCK == 0 and SHIFT_BLOCK % SUBLANES == 0
TQ = 512
TK = 512


def _sigmoid(x):
    return 1.0 / (1.0 + jnp.exp(-x))


def _silu(x):
    return x * _sigmoid(x)


def _rms(x, w):
    return x * lax.rsqrt(jnp.mean(x * x, axis=-1, keepdims=True) + EPS) * w


def _const_spec(shape):
    return pl.BlockSpec(shape, lambda *_: (0,) * len(shape), pipeline_mode=pl.Buffered(1))


def _ada_kernel(c_ref, w_ref, b_ref, o_ref):
    c = c_ref[...]
    o_ref[...] = jnp.dot(_silu(c), w_ref[...], preferred_element_type=F32) + b_ref[...]


def _ada(c_pad, w_ada, b_ada):
    rows, d = c_pad.shape
    n = w_ada.shape[1]
    tn = 512
    return pl.pallas_call(
        _ada_kernel,
        grid=(n // tn,),
        in_specs=[
            pl.BlockSpec((rows, d), lambda j: (0, 0)),
            pl.BlockSpec((d, tn), lambda j: (0, j)),
            pl.BlockSpec((1, tn), lambda j: (0, j)),
        ],
        out_specs=pl.BlockSpec((rows, tn), lambda j: (0, j)),
        out_shape=jax.ShapeDtypeStruct((rows, n), F32),
        compiler_params=pltpu.CompilerParams(dimension_semantics=("arbitrary",)),
        name="ada",
    )(c_pad, w_ada, b_ada)


def _modulated_norm(x_ref, shift_ref, scale_ref, nw_ref):
    x = x_ref[0]
    h = _rms(x, nw_ref[...]) * (1.0 + scale_ref[0]) + shift_ref[0]
    return h.astype(BF16)


def _wprep_kernel(wt_ref, out_ref):
    blk = wt_ref[...].astype(BF16)
    n = blk.shape[1]
    eye = (lax.broadcasted_iota(jnp.int32, (n, n), 0)
           == lax.broadcasted_iota(jnp.int32, (n, n), 1)).astype(BF16)
    out_ref[...] = lax.dot_general(eye, blk, (((1,), (1,)), ((), ())),
                                   preferred_element_type=F32).astype(BF16)


def _wprep(w_in_t):
    d = w_in_t.shape[1]

    def first_row(g):
        row = jnp.where(g < G_BGATE, g * d, IN_COL_BGATE + (g - G_BGATE) * d)
        return pl.multiple_of(row, math.gcd(d, IN_COL_BGATE))

    return pl.pallas_call(
        _wprep_kernel,
        grid=(N_WIDE_GROUPS,),
        in_specs=[pl.BlockSpec((pl.Element(d), pl.Element(d)), lambda g: (first_row(g), 0))],
        out_specs=pl.BlockSpec((d, d), lambda g: (0, g)),
        out_shape=jax.ShapeDtypeStruct((d, N_WIDE_GROUPS * d), BF16),
        compiler_params=pltpu.CompilerParams(dimension_semantics=("arbitrary",)),
        name="wprep",
    )(w_in_t)


def _gateconv_kernel(x_ref, shift_ref, scale_ref, nw_ref, w_ref, cw_ref, cb_ref,
                     cv_ref, sa_ref, sb_ref, ga_ref, gb_ref, sh_ref, hb_ref):
    k = pl.program_id(1)
    n_lc = CONV_WIDTH // CONV_LANES
    first = HALO - (CONV_KERNEL - 1)
    groups = RC // SUBLANES
    streams = range(GC_STREAMS)

    @pl.when(k == 0)
    def _():
        for st in streams:
            for lc in range(n_lc):
                sh_ref[st, lc, 0, 0:HALO, :] = jnp.zeros((HALO, CONV_LANES), F32)

    @pl.when(k > 0)
    def _():
        for st in streams:
            for lc in range(n_lc):
                sh_ref[st, lc, 0, 0:HALO, :] = sh_ref[st, lc, 0, TC:TC + HALO, :]

    def proj(st, group, piece):
        col = group * D_MODEL + piece * PROJ_COLS
        return jnp.dot(hb_ref[st], w_ref[:, col:col + PROJ_COLS], preferred_element_type=F32)

    for st in streams:
        h = _rms(x_ref[st], nw_ref[...]) * (1.0 + scale_ref[st]) + shift_ref[st]
        hb_ref[st] = h.astype(BF16)
        for piece in range(D_MODEL // PROJ_COLS):
            u = proj(st, G_VAL, piece) * _sigmoid(proj(st, G_GLU, piece))
            for j in range(PROJ_COLS // CONV_LANES):
                lc = piece * (PROJ_COLS // CONV_LANES) + j
                sh_ref[st, lc, 0, HALO:, :] = u[:, j * CONV_LANES:(j + 1) * CONV_LANES]

    def gate_piece(st, group, out_ref, act, piece):
        cols = slice(piece * PROJ_COLS, (piece + 1) * PROJ_COLS)
        out_ref[st, :, cols] = act(proj(st, group, piece)).astype(BF16)

    def shift_chunk(st, lc):
        for r in range(1, SUBLANES):
            for rb in range(0, SHIFT_ROWS, SHIFT_BLOCK):
                sh_ref[st, lc, r, rb:rb + SHIFT_BLOCK, :] = (
                    sh_ref[st, lc, 0, rb + r:rb + r + SHIFT_BLOCK, :])

    def conv_chunk(st, lc, c):
        row0 = c * RC
        lanes = slice(lc * CONV_LANES, (lc + 1) * CONV_LANES)
        acc = jnp.zeros((groups, SUBLANES, CONV_LANES), F32)
        for r in range(SUBLANES):
            offs = [o for o in range(first, first + CONV_KERNEL) if o % SUBLANES == r]
            g0, g1 = offs[0] // SUBLANES, offs[-1] // SUBLANES
            span = groups + g1 - g0
            start = row0 + g0 * SUBLANES
            xs = sh_ref[st, lc, r, start:start + span * SUBLANES, :]
            xs = xs.reshape(span, SUBLANES, CONV_LANES)
            for o in offs:
                g = o // SUBLANES - g0
                acc = acc + cw_ref[lc, o - first][None] * xs[g:g + groups]
        cv = acc.reshape(RC, CONV_LANES) + cb_ref[:, lanes]
        cv_ref[st, row0:row0 + RC, lanes] = cv.astype(BF16)

    def stream_work(st):
        mxu_work, vpu_work, text = [], [], []
        for group, out_ref, act in ((G_AGATE, sa_ref, _silu), (G_BGATE, sb_ref, _silu),
                                    (G_GA, ga_ref, _sigmoid), (G_GB, gb_ref, _sigmoid)):
            mxu_work += [functools.partial(gate_piece, st, group, out_ref, act, p)
                         for p in range(D_MODEL // PROJ_COLS)]
        for lc in range(n_lc):
            vpu_work.append(functools.partial(shift_chunk, st, lc))
            vpu_work += [functools.partial(conv_chunk, st, lc, c) for c in range(TC // RC)]
        issued = 0
        for i, mxu_fn in enumerate(mxu_work):
            target = (len(vpu_work) * (i + 1)) // len(mxu_work)
            text.append([mxu_fn] + vpu_work[issued:target])
            issued = target
        return text

    @pl.when(k >= 0)
    def _():
        for parts in zip(*[stream_work(st) for st in streams]):
            for part in parts:
                for fn in part:
                    fn()


def _gateconv(x, shift, scale, norm_w, w_wide, conv_w, conv_b):
    b, s, d = x.shape
    n_lc = d // CONV_LANES
    gs = GC_STREAMS
    tok = pl.BlockSpec((gs, TC, d), lambda bi, k: (bi, k, 0))
    mod = pl.BlockSpec((gs, 1, d), lambda bi, k: (bi, 0, 0))
    out = jax.ShapeDtypeStruct((b, s, d), BF16)
    return pl.pallas_call(
        _gateconv_kernel,
        grid=(b // gs, s // TC),
        in_specs=[tok, mod, mod, _const_spec((1, d)), _const_spec(w_wide.shape),
                  _const_spec(conv_w.shape), _const_spec((1, d))],
        out_specs=[tok] * 5,
        out_shape=[out] * 5,
        scratch_shapes=[pltpu.VMEM((gs, n_lc, SUBLANES, HALO + TC, CONV_LANES), F32),
                        pltpu.VMEM((gs, TC, d), BF16)],
        compiler_params=pltpu.CompilerParams(
            dimension_semantics=("arbitrary", "arbitrary"), vmem_limit_bytes=VMEM_LIMIT),
        name="gateconv",
    )(x, shift, scale, norm_w, w_wide, conv_w, conv_b)


def _rope_kernel(pos_ref, invf_ref, table_ref):
    ang = pos_ref[...] * invf_ref[...]
    cos_d, sin_d = jnp.cos(ang), jnp.sin(ang)
    n_freq = LANES // ROPE_PACK
    quarter = lax.broadcasted_iota(jnp.int32, ang.shape, 1) // n_freq
    for p in range(ROPE_PACK):
        def placed(src, q):
            return pltpu.roll(src, ((q - p) * n_freq) % LANES, 1)
        row = jnp.where(quarter == 0, placed(cos_d, 0),
                        jnp.where(quarter == 1, placed(cos_d, 1),
                                  jnp.where(quarter == 2, placed(sin_d, 2), placed(sin_d, 3))))
        table_ref[pl.ds(p, ang.shape[0], stride=ROPE_PACK), :] = row


def _rope_table(pos_dense, invf):
    rows = pos_dense.shape[0]
    tr = 512
    return pl.pallas_call(
        _rope_kernel,
        grid=(rows // tr,),
        in_specs=[pl.BlockSpec((tr, LANES), lambda i: (i, 0)), _const_spec(invf.shape)],
        out_specs=pl.BlockSpec((ROPE_PACK * tr, LANES), lambda i: (i, 0)),
        out_shape=jax.ShapeDtypeStruct((ROPE_PACK * rows, LANES), F32),
        compiler_params=pltpu.CompilerParams(dimension_semantics=("arbitrary",)),
        name="rope",
    )(pos_dense, invf)


def _mla_kernel(x_ref, shift_ref, scale_ref, nw_ref, table_ref, w_ref,
                qn_ref, kvn_ref, wuq_ref, wuk_ref, wvt_ref,
                q_ref, k_ref, vt_ref):
    hb = _modulated_norm(x_ref, shift_ref, scale_ref, nw_ref)
    lat = lax.dot_general(hb, w_ref[...], (((1,), (1,)), ((), ())),
                          preferred_element_type=F32)
    cq = lat[:, :Q_LORA_RANK]
    ckv = lat[:, Q_LORA_RANK:Q_LORA_RANK + KV_LORA_RANK]
    kp = lat[:, Q_LORA_RANK + KV_LORA_RANK:]

    table = table_ref[0]
    lane = lax.broadcasted_iota(jnp.int32, table.shape, 1)
    first_half = lane < QK_ROPE_DIM

    def rope(t):
        return jnp.where(first_half, t + pltpu.roll(t, QK_ROPE_DIM, 1), 0.0)

    cqn = _rms(cq, qn_ref[...]).astype(BF16)
    ckvn = _rms(ckv, kvn_ref[...]).astype(BF16)

    qa = jnp.dot(cqn, wuq_ref[...], preferred_element_type=F32)
    kn = jnp.dot(ckvn, wuk_ref[...], preferred_element_type=F32)
    vt = lax.dot_general(wvt_ref[...], ckvn, (((1,), (1,)), ((), ())),
                         preferred_element_type=F32)

    k_hi = rope(kp * table).astype(BF16)
    q_table = table * Q_SCALE
    for h in range(N_HEADS):
        blk = qa[:, h * QK_PAD:(h + 1) * QK_PAD]
        q_ref[0, h, :, :LANES] = (blk[:, :LANES] * Q_SCALE).astype(BF16)
        q_ref[0, h, :, LANES:] = rope(blk[:, LANES:] * q_table).astype(BF16)
        k_ref[0, h, :, :LANES] = kn[:, h * LANES:(h + 1) * LANES].astype(BF16)
        k_ref[0, h, :, LANES:] = k_hi
    ones = jnp.ones((V_PAD - V_DIM, TK), BF16)
    for j in range(TM // TK):
        for h in range(N_HEADS):
            vt_ref[0, j, h, :V_DIM, :] = vt[h * V_DIM:(h + 1) * V_DIM, j * TK:(j + 1) * TK].astype(BF16)
            vt_ref[0, j, h, V_DIM:, :] = ones


def _mla_proj(x, shift, scale, norm_w, table, w_lat_t, qn_w, kvn_w, w_uq2, w_uk, w_vt):
    b, s, d = x.shape
    tok = pl.BlockSpec((1, TM, d), lambda bi, i: (bi, i, 0))
    mod = pl.BlockSpec((1, 1, d), lambda bi, i: (bi, 0, 0))
    qk_spec = pl.BlockSpec((1, N_HEADS, TM, QK_PAD), lambda bi, i: (bi, 0, i, 0))
    return pl.pallas_call(
        _mla_kernel,
        grid=(b, s // TM),
        in_specs=[
            tok, mod, mod, _const_spec((1, d)),
            pl.BlockSpec((1, TM, LANES), lambda bi, i: (bi, i, 0)),
            _const_spec(w_lat_t.shape),
            _const_spec(qn_w.shape), _const_spec(kvn_w.shape),
            _const_spec(w_uq2.shape), _const_spec(w_uk.shape), _const_spec(w_vt.shape),
        ],
        out_specs=[
            qk_spec, qk_spec,
            pl.BlockSpec((1, TM // TK, N_HEADS, V_PAD, TK), lambda bi, i: (bi, i, 0, 0, 0)),
        ],
        out_shape=[
            jax.ShapeDtypeStruct((b, N_HEADS, s, QK_PAD), BF16),
            jax.ShapeDtypeStruct((b, N_HEADS, s, QK_PAD), BF16),
            jax.ShapeDtypeStruct((b, s // TK, N_HEADS, V_PAD, TK), BF16),
        ],
        compiler_params=pltpu.CompilerParams(
            dimension_semantics=("arbitrary", "arbitrary"), vmem_limit_bytes=VMEM_LIMIT),
        name="mla_proj",
    )(x, shift, scale, norm_w, table, w_lat_t, qn_w, kvn_w, w_uq2, w_uk, w_vt)


ATTN_BUFFERS = 3
ATTN_AHEAD = 2
ATTN_GROUP = 3
ATTN_HEADS = 2


def _attn_scores(q_ref, k_ref, bias_ref, hh, item, s_out, bm_out, diagonal):
    qn, jn = item
    k = k_ref[0, hh, pl.ds(pl.multiple_of(jn * TK, TK), TK), :]
    q = q_ref[0, hh, pl.ds(pl.multiple_of(qn * TQ, TQ), TQ), :]
    s = lax.dot_general(k, q, (((1,), (1,)), ((), ())), preferred_element_type=F32)
    if diagonal:
        s = s + bias_ref[...]
    s_out[...] = s
    bm_out[...] = jnp.max(s, axis=0, keepdims=True)


def _attn_accumulate(vt_ref, o_ref, acc_ref, hh, item, p_in, al_in, diagonal):
    qp, jp = item
    pv = jnp.dot(vt_ref[0, jp, hh], p_in[...], preferred_element_type=F32)
    acc = al_in[...] * acc_ref[qp] + pv
    acc_ref[qp] = acc
    if diagonal:
        rows = pl.ds(pl.multiple_of(qp * TQ, TQ), TQ)
        out = acc[:V_DIM] / acc[V_DIM:V_DIM + 1]
        o_ref[0, rows, hh * V_DIM:(hh + 1) * V_DIM] = jnp.transpose(out).astype(BF16)


def _attn_step(q_ref, k_ref, vt_ref, bias_ref, o_ref, acc_ref, m_ref, hh, cur, nxt, prv, items,
               diag):
    s_cur, p_cur, bm_cur, al_cur = cur
    s_nxt, _, bm_nxt, _ = nxt
    _, p_prv, _, al_prv = prv
    item_next, (qc, jc), item_prev = items
    diag_next, diag_prev = diag
    _attn_accumulate(vt_ref, o_ref, acc_ref, hh, item_prev, p_prv, al_prv, diag_prev)
    m_old = jnp.where(jc == 0, -jnp.inf, m_ref[qc])
    m_new = jnp.maximum(m_old, bm_cur[...])
    p_cur[...] = jnp.exp2(s_cur[...] - m_new).astype(BF16)
    al_cur[...] = jnp.exp2(m_old - m_new)
    m_ref[qc] = m_new
    _attn_scores(q_ref, k_ref, bias_ref, hh, item_next, s_nxt, bm_nxt, diag_next)


def _attn_item(t, n_lower):
    def lower(t):
        q = jnp.int32(1)
        for c in range(2, 64):
            if c * (c - 1) // 2 >= n_lower:
                break
            q = q + (t >= c * (c - 1) // 2).astype(jnp.int32)
        return q, t - lax.shift_right_logical(q * (q - 1), 1)
    ql, jl = lower(jnp.minimum(t, n_lower - 1))
    d = t - n_lower
    is_diag = t >= n_lower
    return jnp.where(is_diag, d, ql), jnp.where(is_diag, d, jl)


def _attn_kernel(q_ref, k_ref, vt_ref, bias_ref, o_ref, *bufs):
    n_q = q_ref.shape[2] // TQ
    n_lower = n_q * (n_q - 1) // 2
    n_items = n_lower + n_q
    nbuf = ATTN_BUFFERS
    heads = range(ATTN_HEADS)
    assert TQ == TK and ATTN_GROUP % nbuf == 0 and n_items % nbuf == 0
    per_head = len(bufs) // ATTN_HEADS
    accs = tuple(bufs[hh * per_head] for hh in heads)
    maxs = tuple(bufs[hh * per_head + 1] for hh in heads)
    rings = tuple(tuple(bufs[hh * per_head + 2 + 4 * r:hh * per_head + 6 + 4 * r]
                        for r in range(nbuf)) for hh in heads)

    def item(t):
        return _attn_item(jnp.clip(t, 0, n_items - 1), n_lower)

    def steps(t0, count, t0_static=None):
        for u in range(count):
            t = t0 + u
            diag = (False, False) if t0_static is None else (
                t0_static + u + ATTN_AHEAD >= n_lower, t0_static + u - 1 >= n_lower)
            items = (item(t + ATTN_AHEAD), item(t), item(t - 1))
            for hh in heads:
                ring = rings[hh]
                _attn_step(q_ref, k_ref, vt_ref, bias_ref, o_ref, accs[hh], maxs[hh], hh,
                           ring[u % nbuf], ring[(u + ATTN_AHEAD) % nbuf], ring[(u - 1) % nbuf],
                           items, diag)

    for hh in heads:
        accs[hh][...] = jnp.zeros_like(accs[hh])
        maxs[hh][...] = jnp.zeros_like(maxs[hh])
        _, p_last, _, al_last = rings[hh][nbuf - 1]
        p_last[...] = jnp.zeros_like(p_last)
        al_last[...] = jnp.ones_like(al_last)
        for a in range(ATTN_AHEAD):
            s_buf, _, bm_buf, _ = rings[hh][a]
            _attn_scores(q_ref, k_ref, bias_ref, hh, item(jnp.int32(a)), s_buf, bm_buf, False)

    n_plain = (n_lower - ATTN_AHEAD) // ATTN_GROUP
    lax.fori_loop(0, n_plain, lambda g, c: (steps(g * ATTN_GROUP, ATTN_GROUP), c)[1], 0)
    t_tail = n_plain * ATTN_GROUP
    steps(jnp.int32(t_tail), n_items - t_tail, t_tail)
    for hh in heads:
        _, p_last, _, al_last = rings[hh][(n_items - 1) % nbuf]
        _attn_accumulate(vt_ref, o_ref, accs[hh], hh, item(jnp.int32(n_items - 1)), p_last,
                         al_last, True)


def _attention(q, k, vt):
    b, h, s, _ = q.shape
    n_q = s // TQ
    row = lax.broadcasted_iota(jnp.int32, (TK, TQ), 0)
    col = lax.broadcasted_iota(jnp.int32, (TK, TQ), 1)
    bias = jnp.where(row <= col, 0.0, -jnp.inf).astype(F32)
    stat = pltpu.VMEM((1, TQ), F32)
    slot = [pltpu.VMEM((TK, TQ), F32), pltpu.VMEM((TK, TQ), BF16), stat, stat]
    head_scratch = ([pltpu.VMEM((n_q, V_PAD, TQ), F32), pltpu.VMEM((n_q, 1, TQ), F32)]
                    + slot * ATTN_BUFFERS)
    hs = ATTN_HEADS
    return pl.pallas_call(
        _attn_kernel,
        grid=(b, h // hs),
        in_specs=[
            pl.BlockSpec((1, hs, s, QK_PAD), lambda bi, hi: (bi, hi, 0, 0)),
            pl.BlockSpec((1, hs, s, QK_PAD), lambda bi, hi: (bi, hi, 0, 0)),
            pl.BlockSpec((1, s // TK, hs, V_PAD, TK), lambda bi, hi: (bi, 0, hi, 0, 0)),
            _const_spec(bias.shape),
        ],
        out_specs=pl.BlockSpec((1, s, hs * V_DIM), lambda bi, hi: (bi, 0, hi)),
        out_shape=jax.ShapeDtypeStruct((b, s, h * V_DIM), BF16),
        scratch_shapes=head_scratch * hs,
        compiler_params=pltpu.CompilerParams(
            dimension_semantics=("arbitrary", "arbitrary"),
            vmem_limit_bytes=VMEM_LIMIT),
        name="attn",
    )(q, k, vt, bias)


def _out_kernel(cv_ref, sa_ref, o_ref, sb_ref, ga_ref, gb_ref, x_ref, gate_ref, lw_ref, lb_ref,
                wco_ref, wao_ref, wo_ref, fw_ref, y_ref):
    for st in range(OUT_STREAMS):
        cv = cv_ref[st].astype(F32)
        mu = jnp.mean(cv, axis=-1, keepdims=True)
        xc = cv - mu
        var = jnp.mean(xc * xc, axis=-1, keepdims=True)
        ln = xc * lax.rsqrt(var + EPS) * lw_ref[...] + lb_ref[...]
        ug = (_silu(ln) * sa_ref[st].astype(F32)).astype(BF16)
        ya = jnp.dot(ug, wco_ref[...], preferred_element_type=F32)
        og = (o_ref[st].astype(F32) * sb_ref[st].astype(F32)).astype(BF16)
        yb = jnp.dot(og, wao_ref[...], preferred_element_type=F32)
        merged = ga_ref[st].astype(F32) * ya + gb_ref[st].astype(F32) * yb
        delta = jnp.dot(merged.astype(BF16), wo_ref[...], preferred_element_type=F32)
        y = x_ref[st] + gate_ref[st] * delta
        y_ref[st] = _rms(y, fw_ref[...])


def _out(cv, sa, o, sb, ga, gb, x, gate, ln_w, ln_b, w_co, w_ao, w_o, fw):
    b, s, d = x.shape
    tok = pl.BlockSpec((OUT_STREAMS, TO, d), lambda bi, i: (bi, i, 0))
    mod = pl.BlockSpec((OUT_STREAMS, 1, d), lambda bi, i: (bi, 0, 0))
    wspec = _const_spec((d, d))
    vec = _const_spec((1, d))
    return pl.pallas_call(
        _out_kernel,
        grid=(b // OUT_STREAMS, s // TO),
        in_specs=[tok, tok, tok, tok, tok, tok, tok, mod, vec, vec, wspec, wspec, wspec, vec],
        out_specs=tok,
        out_shape=jax.ShapeDtypeStruct((b, s, d), F32),
        compiler_params=pltpu.CompilerParams(
            dimension_semantics=("arbitrary", "arbitrary"), vmem_limit_bytes=VMEM_LIMIT),
        name="out",
    )(cv, sa, o, sb, ga, gb, x, gate, ln_w, ln_b, w_co, w_ao, w_o, fw)


def _rot_cols(w):
    half = w.shape[-1] // 2
    return jnp.concatenate([-w[..., half:], w[..., :half]], axis=-1)


def _layer(x, c_act_pad, table, w_ada, b_ada, norm_w, w_in, conv_w, conv_b, conv_ln_w,
           conv_ln_b, w_conv_out, q_norm_w, w_uq, kv_norm_w, w_ukv, w_attn_out, w_out, final_w):
    b, s, d = x.shape
    mod = _ada(c_act_pad, w_ada, b_ada[None, :])[:b]
    shift, scale, gate = (m[:, None, :] for m in jnp.split(mod, 3, axis=-1))

    assert w_in.shape[1] == IN_COLS
    w_in_t = w_in.T
    w_wide = _wprep(w_in_t)
    w_kpe_t = w_in_t[IN_COL_BGATE - QK_ROPE_DIM:IN_COL_BGATE]
    half = QK_ROPE_DIM // 2
    w_lat_t = jnp.concatenate([w_in_t[IN_COL_LATENT:IN_COL_BGATE], -w_kpe_t[half:], w_kpe_t[:half]],
                              axis=0)
    w_lat_t = lax.optimization_barrier(w_lat_t).astype(BF16)

    wq = w_uq.reshape(Q_LORA_RANK, N_HEADS, QK_NOPE_DIM + QK_ROPE_DIM)
    wq_rope = wq[..., QK_NOPE_DIM:]
    w_uq2 = jnp.concatenate([wq, _rot_cols(wq_rope)], axis=-1)
    w_uq2 = w_uq2.reshape(Q_LORA_RANK, N_HEADS * QK_PAD).astype(BF16)
    wkv = w_ukv.reshape(KV_LORA_RANK, N_HEADS, QK_NOPE_DIM + V_DIM)
    w_uk = wkv[..., :QK_NOPE_DIM].reshape(KV_LORA_RANK, N_HEADS * QK_NOPE_DIM).astype(BF16)
    w_vt = wkv[..., QK_NOPE_DIM:].reshape(KV_LORA_RANK, ATTN_WIDTH).T.astype(BF16)

    nw = norm_w[None, :]
    cw = conv_w.reshape(CONV_KERNEL, CONV_WIDTH // CONV_LANES, 1, CONV_LANES).transpose(1, 0, 2, 3)
    cw = jnp.broadcast_to(cw, (CONV_WIDTH // CONV_LANES, CONV_KERNEL, SUBLANES, CONV_LANES))
    cv, sa, sb, ga, gb = _gateconv(x, shift, scale, nw, w_wide, cw, conv_b[None, :])
    q, k, vt = _mla_proj(x, shift, scale, nw, table, w_lat_t, q_norm_w[None, :],
                         kv_norm_w[None, :], w_uq2, w_uk, w_vt)
    o = _attention(q, k, vt)
    return _out(cv, sa, o, sb, ga, gb, x, gate, conv_ln_w[None, :], conv_ln_b[None, :],
                w_conv_out.astype(BF16), w_attn_out.astype(BF16), w_out.astype(BF16),
                final_w[None, :])


def kernel(x, c, positions, w_ada, b_ada, norm_w, w_in, conv_w, conv_b, conv_ln_w, conv_ln_b,
           w_conv_out, q_norm_w, w_uq, kv_norm_w, w_ukv, w_attn_out, w_out, final_norm_w):
    b, s, d = x.shape
    depth = w_ada.shape[0]
    assert depth == 1, "final rmsnorm is fused into the single layer's output kernel"
    inv_freq = ROPE_THETA ** (-jnp.arange(0, QK_ROPE_DIM, 2, dtype=F32) / QK_ROPE_DIM)
    n_freq = inv_freq.shape[0]
    assert ROPE_PACK * n_freq == LANES
    invf = jnp.tile(inv_freq, ROPE_PACK)[None, :]
    pos_dense = jnp.repeat(positions.astype(F32).reshape(b * s // ROPE_PACK, ROPE_PACK), n_freq, axis=1)
    table = _rope_table(pos_dense, invf).reshape(b, s, LANES)
    c_pad = jnp.pad(c, ((0, 8 - b), (0, 0)))
    return _layer(x, c_pad, table, w_ada[0], b_ada[0], norm_w[0], w_in[0], conv_w[0],
                  conv_b[0], conv_ln_w[0], conv_ln_b[0], w_conv_out[0], q_norm_w[0], w_uq[0],
                  kv_norm_w[0], w_ukv[0], w_attn_out[0], w_out[0], final_norm_w)
```

```python
import functools
import math

import jax
import jax.numpy as jnp
from jax import lax
from jax.experimental import pallas as pl
from jax.experimental.pallas import tpu as pltpu

F32 = jnp.float32
BF16 = jnp.bfloat16

D_MODEL = 1024
CONV_WIDTH = 1024
CONV_KERNEL = 31
N_HEADS = 8
QK_NOPE_DIM = 128
QK_ROPE_DIM = 64
V_DIM = 128
Q_LORA_RANK = 256
KV_LORA_RANK = 256
ATTN_WIDTH = N_HEADS * V_DIM
ROPE_THETA = 10000.0
EPS = 1e-6

LANES = 128
QK_PAD = 2 * LANES
BF16_SUBLANES = 16
V_PAD = V_DIM + BF16_SUBLANES
ROPE_PACK = LANES // (QK_ROPE_DIM // 2)
HALO = 32
VMEM_LIMIT = 56 * 1024 * 1024

Q_SCALE = (QK_NOPE_DIM + QK_ROPE_DIM) ** -0.5 * math.log2(math.e)

IN_COL_LATENT = 3 * CONV_WIDTH
IN_COL_BGATE = IN_COL_LATENT + Q_LORA_RANK + KV_LORA_RANK + QK_ROPE_DIM
IN_COLS = IN_COL_BGATE + ATTN_WIDTH + 2 * D_MODEL
G_VAL, G_GLU, G_AGATE, G_BGATE, G_GA, G_GB = range(6)
N_WIDE_GROUPS = 6
assert CONV_WIDTH == ATTN_WIDTH == D_MODEL

TM = 1024
TC = 256
RC = 16
CONV_LANES = 1024
PROJ_COLS = 1024
GC_STREAMS = 2
TO = 512
OUT_STREAMS = 1
SUBLANES = 8
SHIFT_ROWS = TC + HALO - SUBLANES
SHIFT_BLOCK = 40
assert SHIFT_ROWS % SHIFT_BLOCK == 0 and SHIFT_BLOCK % SUBLANES == 0
TQ = 512
TK = 512


def _sigmoid(x):
    return 1.0 / (1.0 + jnp.exp(-x))


def _silu(x):
    return x * _sigmoid(x)


def _rms(x, w):
    return x * lax.rsqrt(jnp.mean(x * x, axis=-1, keepdims=True) + EPS) * w


def _const_spec(shape):
    return pl.BlockSpec(shape, lambda *_: (0,) * len(shape), pipeline_mode=pl.Buffered(1))


def _ada_kernel(c_ref, w_ref, b_ref, o_ref):
    c = c_ref[...]
    o_ref[...] = jnp.dot(_silu(c), w_ref[...], preferred_element_type=F32) + b_ref[...]


def _ada(c_pad, w_ada, b_ada):
    rows, d = c_pad.shape
    n = w_ada.shape[1]
    tn = 512
    return pl.pallas_call(
        _ada_kernel,
        grid=(n // tn,),
        in_specs=[
            pl.BlockSpec((rows, d), lambda j: (0, 0)),
            pl.BlockSpec((d, tn), lambda j: (0, j)),
            pl.BlockSpec((1, tn), lambda j: (0, j)),
        ],
        out_specs=pl.BlockSpec((rows, tn), lambda j: (0, j)),
        out_shape=jax.ShapeDtypeStruct((rows, n), F32),
        compiler_params=pltpu.CompilerParams(dimension_semantics=("arbitrary",)),
        name="ada",
    )(c_pad, w_ada, b_ada)


def _modulated_norm(x_ref, shift_ref, scale_ref, nw_ref):
    x = x_ref[0]
    h = _rms(x, nw_ref[...]) * (1.0 + scale_ref[0]) + shift_ref[0]
    return h.astype(BF16)


def _wprep_kernel(wt_ref, out_ref):
    blk = wt_ref[...].astype(BF16)
    n = blk.shape[1]
    eye = (lax.broadcasted_iota(jnp.int32, (n, n), 0)
           == lax.broadcasted_iota(jnp.int32, (n, n), 1)).astype(BF16)
    out_ref[...] = lax.dot_general(eye, blk, (((1,), (1,)), ((), ())),
                                   preferred_element_type=F32).astype(BF16)


def _wprep(w_in_t):
    d = w_in_t.shape[1]

    def first_row(g):
        row = jnp.where(g < G_BGATE, g * d, IN_COL_BGATE + (g - G_BGATE) * d)
        return pl.multiple_of(row, math.gcd(d, IN_COL_BGATE))

    return pl.pallas_call(
        _wprep_kernel,
        grid=(N_WIDE_GROUPS,),
        in_specs=[pl.BlockSpec((pl.Element(d), pl.Element(d)), lambda g: (first_row(g), 0))],
        out_specs=pl.BlockSpec((d, d), lambda g: (0, g)),
        out_shape=jax.ShapeDtypeStruct((d, N_WIDE_GROUPS * d), BF16),
        compiler_params=pltpu.CompilerParams(dimension_semantics=("arbitrary",)),
        name="wprep",
    )(w_in_t)


def _gateconv_kernel(x_ref, shift_ref, scale_ref, nw_ref, w_ref, cw_ref, cb_ref,
                     cv_ref, sa_ref, sb_ref, ga_ref, gb_ref, sh_ref, hb_ref):
    k = pl.program_id(1)
    n_lc = CONV_WIDTH // CONV_LANES
    first = HALO - (CONV_KERNEL - 1)
    groups = RC // SUBLANES
    streams = range(GC_STREAMS)

    @pl.when(k == 0)
    def _():
        for st in streams:
            for lc in range(n_lc):
                sh_ref[st, lc, 0, 0:HALO, :] = jnp.zeros((HALO, CONV_LANES), F32)

    @pl.when(k > 0)
    def _():
        for st in streams:
            for lc in range(n_lc):
                sh_ref[st, lc, 0, 0:HALO, :] = sh_ref[st, lc, 0, TC:TC + HALO, :]

    def proj(st, group, piece):
        col = group * D_MODEL + piece * PROJ_COLS
        return jnp.dot(hb_ref[st], w_ref[:, col:col + PROJ_COLS], preferred_element_type=F32)

    for st in streams:
        h = _rms(x_ref[st], nw_ref[...]) * (1.0 + scale_ref[st]) + shift_ref[st]
        hb_ref[st] = h.astype(BF16)
        for piece in range(D_MODEL // PROJ_COLS):
            u = proj(st, G_VAL, piece) * _sigmoid(proj(st, G_GLU, piece))
            for j in range(PROJ_COLS // CONV_LANES):
                lc = piece * (PROJ_COLS // CONV_LANES) + j
                sh_ref[st, lc, 0, HALO:, :] = u[:, j * CONV_LANES:(j + 1) * CONV_LANES]

    def gate_piece(st, group, out_ref, act, piece):
        cols = slice(piece * PROJ_COLS, (piece + 1) * PROJ_COLS)
        out_ref[st, :, cols] = act(proj(st, group, piece)).astype(BF16)

    def shift_chunk(st, lc):
        for r in range(1, SUBLANES):
            for rb in range(0, SHIFT_ROWS, SHIFT_BLOCK):
                sh_ref[st, lc, r, rb:rb + SHIFT_BLOCK, :] = (
                    sh_ref[st, lc, 0, rb + r:rb + r + SHIFT_BLOCK, :])

    def conv_chunk(st, lc, c):
        row0 = c * RC
        lanes = slice(lc * CONV_LANES, (lc + 1) * CONV_LANES)
        acc = jnp.zeros((groups, SUBLANES, CONV_LANES), F32)
        for r in range(SUBLANES):
            offs = [o for o in range(first, first + CONV_KERNEL) if o % SUBLANES == r]
            g0, g1 = offs[0] // SUBLANES, offs[-1] // SUBLANES
            span = groups + g1 - g0
            start = row0 + g0 * SUBLANES
            xs = sh_ref[st, lc, r, start:start + span * SUBLANES, :]
            xs = xs.reshape(span, SUBLANES, CONV_LANES)
            for o in offs:
                g = o // SUBLANES - g0
                acc = acc + cw_ref[lc, o - first][None] * xs[g:g + groups]
        cv = acc.reshape(RC, CONV_LANES) + cb_ref[:, lanes]
        cv_ref[st, row0:row0 + RC, lanes] = cv.astype(BF16)

    def stream_work(st):
        mxu_work, vpu_work, text = [], [], []
        for group, out_ref, act in ((G_AGATE, sa_ref, _silu), (G_BGATE, sb_ref, _silu),
                                    (G_GA, ga_ref, _sigmoid), (G_GB, gb_ref, _sigmoid)):
            mxu_work += [functools.partial(gate_piece, st, group, out_ref, act, p)
                         for p in range(D_MODEL // PROJ_COLS)]
        for lc in range(n_lc):
            vpu_work.append(functools.partial(shift_chunk, st, lc))
            vpu_work += [functools.partial(conv_chunk, st, lc, c) for c in range(TC // RC)]
        issued = 0
        for i, mxu_fn in enumerate(mxu_work):
            target = (len(vpu_work) * (i + 1)) // len(mxu_work)
            text.append([mxu_fn] + vpu_work[issued:target])
            issued = target
        return text

    @pl.when(k >= 0)
    def _():
        for parts in zip(*[stream_work(st) for st in streams]):
            for part in parts:
                for fn in part:
                    fn()


def _gateconv(x, shift, scale, norm_w, w_wide, conv_w, conv_b):
    b, s, d = x.shape
    n_lc = d // CONV_LANES
    gs = GC_STREAMS
    tok = pl.BlockSpec((gs, TC, d), lambda bi, k: (bi, k, 0))
    mod = pl.BlockSpec((gs, 1, d), lambda bi, k: (bi, 0, 0))
    out = jax.ShapeDtypeStruct((b, s, d), BF16)
    return pl.pallas_call(
        _gateconv_kernel,
        grid=(b // gs, s // TC),
        in_specs=[tok, mod, mod, _const_spec((1, d)), _const_spec(w_wide.shape),
                  _const_spec(conv_w.shape), _const_spec((1, d))],
        out_specs=[tok] * 5,
        out_shape=[out] * 5,
        scratch_shapes=[pltpu.VMEM((gs, n_lc, SUBLANES, HALO + TC, CONV_LANES), F32),
                        pltpu.VMEM((gs, TC, d), BF16)],
        compiler_params=pltpu.CompilerParams(
            dimension_semantics=("arbitrary", "arbitrary"), vmem_limit_bytes=VMEM_LIMIT),
        name="gateconv",
    )(x, shift, scale, norm_w, w_wide, conv_w, conv_b)


def _rope_kernel(pos_ref, invf_ref, table_ref):
    ang = pos_ref[...] * invf_ref[...]
    cos_d, sin_d = jnp.cos(ang), jnp.sin(ang)
    n_freq = LANES // ROPE_PACK
    quarter = lax.broadcasted_iota(jnp.int32, ang.shape, 1) // n_freq
    for p in range(ROPE_PACK):
        def placed(src, q):
            return pltpu.roll(src, ((q - p) * n_freq) % LANES, 1)
        row = jnp.where(quarter == 0, placed(cos_d, 0),
                        jnp.where(quarter == 1, placed(cos_d, 1),
                                  jnp.where(quarter == 2, placed(sin_d, 2), placed(sin_d, 3))))
        table_ref[pl.ds(p, ang.shape[0], stride=ROPE_PACK), :] = row


def _rope_table(pos_dense, invf):
    rows = pos_dense.shape[0]
    tr = 512
    return pl.pallas_call(
        _rope_kernel,
        grid=(rows // tr,),
        in_specs=[pl.BlockSpec((tr, LANES), lambda i: (i, 0)), _const_spec(invf.shape)],
        out_specs=pl.BlockSpec((ROPE_PACK * tr, LANES), lambda i: (i, 0)),
        out_shape=jax.ShapeDtypeStruct((ROPE_PACK * rows, LANES), F32),
        compiler_params=pltpu.CompilerParams(dimension_semantics=("arbitrary",)),
        name="rope",
    )(pos_dense, invf)


def _mla_kernel(x_ref, shift_ref, scale_ref, nw_ref, table_ref, w_ref,
                qn_ref, kvn_ref, wuq_ref, wuk_ref, wvt_ref,
                q_ref, k_ref, vt_ref):
    hb = _modulated_norm(x_ref, shift_ref, scale_ref, nw_ref)
    lat = lax.dot_general(hb, w_ref[...], (((1,), (1,)), ((), ())),
                          preferred_element_type=F32)
    cq = lat[:, :Q_LORA_RANK]
    ckv = lat[:, Q_LORA_RANK:Q_LORA_RANK + KV_LORA_RANK]
    kp = lat[:, Q_LORA_RANK + KV_LORA_RANK:]

    table = table_ref[0]
    lane = lax.broadcasted_iota(jnp.int32, table.shape, 1)
    first_half = lane < QK_ROPE_DIM

    def rope(t):
        return jnp.where(first_half, t + pltpu.roll(t, QK_ROPE_DIM, 1), 0.0)

    cqn = _rms(cq, qn_ref[...]).astype(BF16)
    ckvn = _rms(ckv, kvn_ref[...]).astype(BF16)

    qa = jnp.dot(cqn, wuq_ref[...], preferred_element_type=F32)
    kn = jnp.dot(ckvn, wuk_ref[...], preferred_element_type=F32)
    vt = lax.dot_general(wvt_ref[...], ckvn, (((1,), (1,)), ((), ())),
                         preferred_element_type=F32)

    k_hi = rope(kp * table).astype(BF16)
    q_table = table * Q_SCALE
    for h in range(N_HEADS):
        blk = qa[:, h * QK_PAD:(h + 1) * QK_PAD]
        q_ref[0, h, :, :LANES] = (blk[:, :LANES] * Q_SCALE).astype(BF16)
        q_ref[0, h, :, LANES:] = rope(blk[:, LANES:] * q_table).astype(BF16)
        k_ref[0, h, :, :LANES] = kn[:, h * LANES:(h + 1) * LANES].astype(BF16)
        k_ref[0, h, :, LANES:] = k_hi
    ones = jnp.ones((V_PAD - V_DIM, TK), BF16)
    for j in range(TM // TK):
        for h in range(N_HEADS):
            vt_ref[0, j, h, :V_DIM, :] = vt[h * V_DIM:(h + 1) * V_DIM, j * TK:(j + 1) * TK].astype(BF16)
            vt_ref[0, j, h, V_DIM:, :] = ones


def _mla_proj(x, shift, scale, norm_w, table, w_lat_t, qn_w, kvn_w, w_uq2, w_uk, w_vt):
    b, s, d = x.shape
    tok = pl.BlockSpec((1, TM, d), lambda bi, i: (bi, i, 0))
    mod = pl.BlockSpec((1, 1, d), lambda bi, i: (bi, 0, 0))
    qk_spec = pl.BlockSpec((1, N_HEADS, TM, QK_PAD), lambda bi, i: (bi, 0, i, 0))
    return pl.pallas_call(
        _mla_kernel,
        grid=(b, s // TM),
        in_specs=[
            tok, mod, mod, _const_spec((1, d)),
            pl.BlockSpec((1, TM, LANES), lambda bi, i: (bi, i, 0)),
            _const_spec(w_lat_t.shape),
            _const_spec(qn_w.shape), _const_spec(kvn_w.shape),
            _const_spec(w_uq2.shape), _const_spec(w_uk.shape), _const_spec(w_vt.shape),
        ],
        out_specs=[
            qk_spec, qk_spec,
            pl.BlockSpec((1, TM // TK, N_HEADS, V_PAD, TK), lambda bi, i: (bi, i, 0, 0, 0)),
        ],
        out_shape=[
            jax.ShapeDtypeStruct((b, N_HEADS, s, QK_PAD), BF16),
            jax.ShapeDtypeStruct((b, N_HEADS, s, QK_PAD), BF16),
            jax.ShapeDtypeStruct((b, s // TK, N_HEADS, V_PAD, TK), BF16),
        ],
        compiler_params=pltpu.CompilerParams(
            dimension_semantics=("arbitrary", "arbitrary"), vmem_limit_bytes=VMEM_LIMIT),
        name="mla_proj",
    )(x, shift, scale, norm_w, table, w_lat_t, qn_w, kvn_w, w_uq2, w_uk, w_vt)


ATTN_BUFFERS = 3
ATTN_AHEAD = 2
ATTN_GROUP = 3
ATTN_HEADS = 2


def _attn_scores(q_ref, k_ref, bias_ref, hh, item, s_out, bm_out, diagonal):
    qn, jn = item
    k = k_ref[0, hh, pl.ds(pl.multiple_of(jn * TK, TK), TK), :]
    q = q_ref[0, hh, pl.ds(pl.multiple_of(qn * TQ, TQ), TQ), :]
    s = lax.dot_general(k, q, (((1,), (1,)), ((), ())), preferred_element_type=F32)
    if diagonal:
        s = s + bias_ref[...]
    s_out[...] = s
    bm_out[...] = jnp.max(s, axis=0, keepdims=True)


def _attn_accumulate(vt_ref, o_ref, acc_ref, hh, item, p_in, al_in, diagonal):
    qp, jp = item
    pv = jnp.dot(vt_ref[0, jp, hh], p_in[...], preferred_element_type=F32)
    acc = al_in[...] * acc_ref[qp] + pv
    acc_ref[qp] = acc
    if diagonal:
        rows = pl.ds(pl.multiple_of(qp * TQ, TQ), TQ)
        out = acc[:V_DIM] / acc[V_DIM:V_DIM + 1]
        o_ref[0, rows, hh * V_DIM:(hh + 1) * V_DIM] = jnp.transpose(out).astype(BF16)


def _attn_step(q_ref, k_ref, vt_ref, bias_ref, o_ref, acc_ref, m_ref, hh, cur, nxt, prv, items,
               diag):
    s_cur, p_cur, bm_cur, al_cur = cur
    s_nxt, _, bm_nxt, _ = nxt
    _, p_prv, _, al_prv = prv
    item_next, (qc, jc), item_prev = items
    diag_next, diag_prev = diag
    _attn_accumulate(vt_ref, o_ref, acc_ref, hh, item_prev, p_prv, al_prv, diag_prev)
    m_old = jnp.where(jc == 0, -jnp.inf, m_ref[qc])
    m_new = jnp.maximum(m_old, bm_cur[...])
    p_cur[...] = jnp.exp2(s_cur[...] - m_new).astype(BF16)
    al_cur[...] = jnp.exp2(m_old - m_new)
    m_ref[qc] = m_new
    _attn_scores(q_ref, k_ref, bias_ref, hh, item_next, s_nxt, bm_nxt, diag_next)


def _attn_item(t, n_lower):
    def lower(t):
        q = jnp.int32(1)
        for c in range(2, 64):
            if c * (c - 1) // 2 >= n_lower:
                break
            q = q + (t >= c * (c - 1) // 2).astype(jnp.int32)
        return q, t - lax.shift_right_logical(q * (q - 1), 1)
    ql, jl = lower(jnp.minimum(t, n_lower - 1))
    d = t - n_lower
    is_diag = t >= n_lower
    return jnp.where(is_diag, d, ql), jnp.where(is_diag, d, jl)


def _attn_kernel(q_ref, k_ref, vt_ref, bias_ref, o_ref, *bufs):
    n_q = q_ref.shape[2] // TQ
    n_lower = n_q * (n_q - 1) // 2
    n_items = n_lower + n_q
    nbuf = ATTN_BUFFERS
    heads = range(ATTN_HEADS)
    assert TQ == TK and ATTN_GROUP % nbuf == 0 and n_items % nbuf == 0
    per_head = len(bufs) // ATTN_HEADS
    accs = tuple(bufs[hh * per_head] for hh in heads)
    maxs = tuple(bufs[hh * per_head + 1] for hh in heads)
    rings = tuple(tuple(bufs[hh * per_head + 2 + 4 * r:hh * per_head + 6 + 4 * r]
                        for r in range(nbuf)) for hh in heads)

    def item(t):
        return _attn_item(jnp.clip(t, 0, n_items - 1), n_lower)

    def steps(t0, count, t0_static=None):
        for u in range(count):
            t = t0 + u
            diag = (False, False) if t0_static is None else (
                t0_static + u + ATTN_AHEAD >= n_lower, t0_static + u - 1 >= n_lower)
            items = (item(t + ATTN_AHEAD), item(t), item(t - 1))
            for hh in heads:
                ring = rings[hh]
                _attn_step(q_ref, k_ref, vt_ref, bias_ref, o_ref, accs[hh], maxs[hh], hh,
                           ring[u % nbuf], ring[(u + ATTN_AHEAD) % nbuf], ring[(u - 1) % nbuf],
                           items, diag)

    for hh in heads:
        accs[hh][...] = jnp.zeros_like(accs[hh])
        maxs[hh][...] = jnp.zeros_like(maxs[hh])
        _, p_last, _, al_last = rings[hh][nbuf - 1]
        p_last[...] = jnp.zeros_like(p_last)
        al_last[...] = jnp.ones_like(al_last)
        for a in range(ATTN_AHEAD):
            s_buf, _, bm_buf, _ = rings[hh][a]
            _attn_scores(q_ref, k_ref, bias_ref, hh, item(jnp.int32(a)), s_buf, bm_buf, False)

    n_plain = (n_lower - ATTN_AHEAD) // ATTN_GROUP
    lax.fori_loop(0, n_plain, lambda g, c: (steps(g * ATTN_GROUP, ATTN_GROUP), c)[1], 0)
    t_tail = n_plain * ATTN_GROUP
    steps(jnp.int32(t_tail), n_items - t_tail, t_tail)
    for hh in heads:
        _, p_last, _, al_last = rings[hh][(n_items - 1) % nbuf]
        _attn_accumulate(vt_ref, o_ref, accs[hh], hh, item(jnp.int32(n_items - 1)), p_last,
                         al_last, True)


def _attention(q, k, vt):
    b, h, s, _ = q.shape
    n_q = s // TQ
    row = lax.broadcasted_iota(jnp.int32, (TK, TQ), 0)
    col = lax.broadcasted_iota(jnp.int32, (TK, TQ), 1)
    bias = jnp.where(row <= col, 0.0, -jnp.inf).astype(F32)
    stat = pltpu.VMEM((1, TQ), F32)
    slot = [pltpu.VMEM((TK, TQ), F32), pltpu.VMEM((TK, TQ), BF16), stat, stat]
    head_scratch = ([pltpu.VMEM((n_q, V_PAD, TQ), F32), pltpu.VMEM((n_q, 1, TQ), F32)]
                    + slot * ATTN_BUFFERS)
    hs = ATTN_HEADS
    return pl.pallas_call(
        _attn_kernel,
        grid=(b, h // hs),
        in_specs=[
            pl.BlockSpec((1, hs, s, QK_PAD), lambda bi, hi: (bi, hi, 0, 0)),
            pl.BlockSpec((1, hs, s, QK_PAD), lambda bi, hi: (bi, hi, 0, 0)),
            pl.BlockSpec((1, s // TK, hs, V_PAD, TK), lambda bi, hi: (bi, 0, hi, 0, 0)),
            _const_spec(bias.shape),
        ],
        out_specs=pl.BlockSpec((1, s, hs * V_DIM), lambda bi, hi: (bi, 0, hi)),
        out_shape=jax.ShapeDtypeStruct((b, s, h * V_DIM), BF16),
        scratch_shapes=head_scratch * hs,
        compiler_params=pltpu.CompilerParams(
            dimension_semantics=("arbitrary", "arbitrary"),
            vmem_limit_bytes=VMEM_LIMIT),
        name="attn",
    )(q, k, vt, bias)


def _out_kernel(cv_ref, sa_ref, o_ref, sb_ref, ga_ref, gb_ref, x_ref, gate_ref, lw_ref, lb_ref,
                wco_ref, wao_ref, wo_ref, fw_ref, y_ref):
    for st in range(OUT_STREAMS):
        cv = cv_ref[st].astype(F32)
        mu = jnp.mean(cv, axis=-1, keepdims=True)
        xc = cv - mu
        var = jnp.mean(xc * xc, axis=-1, keepdims=True)
        ln = xc * lax.rsqrt(var + EPS) * lw_ref[...] + lb_ref[...]
        ug = (_silu(ln) * sa_ref[st].astype(F32)).astype(BF16)
        ya = jnp.dot(ug, wco_ref[...], preferred_element_type=F32)
        og = (o_ref[st].astype(F32) * sb_ref[st].astype(F32)).astype(BF16)
        yb = jnp.dot(og, wao_ref[...], preferred_element_type=F32)
        merged = ga_ref[st].astype(F32) * ya + gb_ref[st].astype(F32) * yb
        delta = jnp.dot(merged.astype(BF16), wo_ref[...], preferred_element_type=F32)
        y = x_ref[st] + gate_ref[st] * delta
        y_ref[st] = _rms(y, fw_ref[...])


def _out(cv, sa, o, sb, ga, gb, x, gate, ln_w, ln_b, w_co, w_ao, w_o, fw):
    b, s, d = x.shape
    tok = pl.BlockSpec((OUT_STREAMS, TO, d), lambda bi, i: (bi, i, 0))
    mod = pl.BlockSpec((OUT_STREAMS, 1, d), lambda bi, i: (bi, 0, 0))
    wspec = _const_spec((d, d))
    vec = _const_spec((1, d))
    return pl.pallas_call(
        _out_kernel,
        grid=(b // OUT_STREAMS, s // TO),
        in_specs=[tok, tok, tok, tok, tok, tok, tok, mod, vec, vec, wspec, wspec, wspec, vec],
        out_specs=tok,
        out_shape=jax.ShapeDtypeStruct((b, s, d), F32),
        compiler_params=pltpu.CompilerParams(
            dimension_semantics=("arbitrary", "arbitrary"), vmem_limit_bytes=VMEM_LIMIT),
        name="out",
    )(cv, sa, o, sb, ga, gb, x, gate, ln_w, ln_b, w_co, w_ao, w_o, fw)


def _rot_cols(w):
    half = w.shape[-1] // 2
    return jnp.concatenate([-w[..., half:], w[..., :half]], axis=-1)


def _layer(x, c_act_pad, table, w_ada, b_ada, norm_w, w_in, conv_w, conv_b, conv_ln_w,
           conv_ln_b, w_conv_out, q_norm_w, w_uq, kv_norm_w, w_ukv, w_attn_out, w_out, final_w):
    b, s, d = x.shape
    mod = _ada(c_act_pad, w_ada, b_ada[None, :])[:b]
    shift, scale, gate = (m[:, None, :] for m in jnp.split(mod, 3, axis=-1))

    assert w_in.shape[1] == IN_COLS
    w_in_t = w_in.T
    w_wide = _wprep(w_in_t)
    w_kpe_t = w_in_t[IN_COL_BGATE - QK_ROPE_DIM:IN_COL_BGATE]
    half = QK_ROPE_DIM // 2
    w_lat_t = jnp.concatenate([w_in_t[IN_COL_LATENT:IN_COL_BGATE], -w_kpe_t[half:], w_kpe_t[:half]],
                              axis=0)
    w_lat_t = lax.optimization_barrier(w_lat_t).astype(BF16)

    wq = w_uq.reshape(Q_LORA_RANK, N_HEADS, QK_NOPE_DIM + QK_ROPE_DIM)
    wq_rope = wq[..., QK_NOPE_DIM:]
    w_uq2 = jnp.concatenate([wq, _rot_cols(wq_rope)], axis=-1)
    w_uq2 = w_uq2.reshape(Q_LORA_RANK, N_HEADS * QK_PAD).astype(BF16)
    wkv = w_ukv.reshape(KV_LORA_RANK, N_HEADS, QK_NOPE_DIM + V_DIM)
    w_uk = wkv[..., :QK_NOPE_DIM].reshape(KV_LORA_RANK, N_HEADS * QK_NOPE_DIM).astype(BF16)
    w_vt = wkv[..., QK_NOPE_DIM:].reshape(KV_LORA_RANK, ATTN_WIDTH).T.astype(BF16)

    nw = norm_w[None, :]
    cw = conv_w.reshape(CONV_KERNEL, CONV_WIDTH // CONV_LANES, 1, CONV_LANES).transpose(1, 0, 2, 3)
    cw = jnp.broadcast_to(cw, (CONV_WIDTH // CONV_LANES, CONV_KERNEL, SUBLANES, CONV_LANES))
    cv, sa, sb, ga, gb = _gateconv(x, shift, scale, nw, w_wide, cw, conv_b[None, :])
    q, k, vt = _mla_proj(x, shift, scale, nw, table, w_lat_t, q_norm_w[None, :],
                         kv_norm_w[None, :], w_uq2, w_uk, w_vt)
    o = _attention(q, k, vt)
    return _out(cv, sa, o, sb, ga, gb, x, gate, conv_ln_w[None, :], conv_ln_b[None, :],
                w_conv_out.astype(BF16), w_attn_out.astype(BF16), w_out.astype(BF16),
                final_w[None, :])


def kernel(x, c, positions, w_ada, b_ada, norm_w, w_in, conv_w, conv_b, conv_ln_w, conv_ln_b,
           w_conv_out, q_norm_w, w_uq, kv_norm_w, w_ukv, w_attn_out, w_out, final_norm_w):
    b, s, d = x.shape
    depth = w_ada.shape[0]
    assert depth == 1, "final rmsnorm is fused into the single layer's output kernel"
    inv_freq = ROPE_THETA ** (-jnp.arange(0, QK_ROPE_DIM, 2, dtype=F32) / QK_ROPE_DIM)
    n_freq = inv_freq.shape[0]
    assert ROPE_PACK * n_freq == LANES
    invf = jnp.tile(inv_freq, ROPE_PACK)[None, :]
    pos_dense = jnp.repeat(positions.astype(F32).reshape(b * s // ROPE_PACK, ROPE_PACK), n_freq, axis=1)
    table = _rope_table(pos_dense, invf).reshape(b, s, LANES)
    c_pad = jnp.pad(c, ((0, 8 - b), (0, 0)))
    return _layer(x, c_pad, table, w_ada[0], b_ada[0], norm_w[0], w_in[0], conv_w[0],
                  conv_b[0], conv_ln_w[0], conv_ln_b[0], w_conv_out[0], q_norm_w[0], w_uq[0],
                  kv_norm_w[0], w_ukv[0], w_attn_out[0], w_out[0], final_norm_w)
```

```python
import functools
import math

import jax
import jax.numpy as jnp
from jax import lax
from jax.experimental import pallas as pl
from jax.experimental.pallas import tpu as pltpu

F32 = jnp.float32
BF16 = jnp.bfloat16

D_MODEL = 1024
CONV_WIDTH = 1024
CONV_KERNEL = 31
N_HEADS = 8
QK_NOPE_DIM = 128
QK_ROPE_DIM = 64
V_DIM = 128
Q_LORA_RANK = 256
KV_LORA_RANK = 256
ATTN_WIDTH = N_HEADS * V_DIM
ROPE_THETA = 10000.0
EPS = 1e-6

LANES = 128
QK_PAD = 2 * LANES
BF16_SUBLANES = 16
V_PAD = V_DIM + BF16_SUBLANES
ROPE_PACK = LANES // (QK_ROPE_DIM // 2)
HALO = 32
VMEM_LIMIT = 56 * 1024 * 1024

Q_SCALE = (QK_NOPE_DIM + QK_ROPE_DIM) ** -0.5 * math.log2(math.e)

IN_COL_LATENT = 3 * CONV_WIDTH
IN_COL_BGATE = IN_COL_LATENT + Q_LORA_RANK + KV_LORA_RANK + QK_ROPE_DIM
IN_COLS = IN_COL_BGATE + ATTN_WIDTH + 2 * D_MODEL
G_VAL, G_GLU, G_AGATE, G_BGATE, G_GA, G_GB = range(6)
N_WIDE_GROUPS = 6
assert CONV_WIDTH == ATTN_WIDTH == D_MODEL

TM = 1024
TC = 256
RC = 16
CONV_LANES = 1024
PROJ_COLS = 1024
GC_STREAMS = 2
TO = 512
OUT_STREAMS = 1
SUBLANES = 8
SHIFT_ROWS = TC + HALO - SUBLANES
SHIFT_BLOCK = 40
assert SHIFT_ROWS % SHIFT_BLOCK == 0 and SHIFT_BLOCK % SUBLANES == 0
assert PROJ_COLS % CONV_LANES == 0 and D_MODEL % PROJ_COLS == 0 and TC % RC == 0
TQ = 512
TK = 512


def _sigmoid(x):
    return 1.0 / (1.0 + jnp.exp(-x))


def _silu(x):
    return x * _sigmoid(x)


def _rms(x, w):
    return x * lax.rsqrt(jnp.mean(x * x, axis=-1, keepdims=True) + EPS) * w


def _const_spec(shape):
    return pl.BlockSpec(shape, lambda *_: (0,) * len(shape), pipeline_mode=pl.Buffered(1))


def _ada_kernel(c_ref, w_ref, b_ref, o_ref):
    c = c_ref[...]
    o_ref[...] = jnp.dot(_silu(c), w_ref[...], preferred_element_type=F32) + b_ref[...]


def _ada(c_pad, w_ada, b_ada):
    rows, d = c_pad.shape
    n = w_ada.shape[1]
    tn = 512
    return pl.pallas_call(
        _ada_kernel,
        grid=(n // tn,),
        in_specs=[
            pl.BlockSpec((rows, d), lambda j: (0, 0)),
            pl.BlockSpec((d, tn), lambda j: (0, j)),
            pl.BlockSpec((1, tn), lambda j: (0, j)),
        ],
        out_specs=pl.BlockSpec((rows, tn), lambda j: (0, j)),
        out_shape=jax.ShapeDtypeStruct((rows, n), F32),
        compiler_params=pltpu.CompilerParams(dimension_semantics=("arbitrary",)),
        name="ada",
    )(c_pad, w_ada, b_ada)


def _modulated_norm(x_ref, shift_ref, scale_ref, nw_ref):
    x = x_ref[0]
    h = _rms(x, nw_ref[...]) * (1.0 + scale_ref[0]) + shift_ref[0]
    return h.astype(BF16)


def _wprep_kernel(wt_ref, out_ref):
    blk = wt_ref[...].astype(BF16)
    n = blk.shape[1]
    eye = (lax.broadcasted_iota(jnp.int32, (n, n), 0)
           == lax.broadcasted_iota(jnp.int32, (n, n), 1)).astype(BF16)
    out_ref[...] = lax.dot_general(eye, blk, (((1,), (1,)), ((), ())),
                                   preferred_element_type=F32).astype(BF16)


def _wprep(w_in_t):
    d = w_in_t.shape[1]

    def first_row(g):
        row = jnp.where(g < G_BGATE, g * d, IN_COL_BGATE + (g - G_BGATE) * d)
        return pl.multiple_of(row, math.gcd(d, IN_COL_BGATE))

    return pl.pallas_call(
        _wprep_kernel,
        grid=(N_WIDE_GROUPS,),
        in_specs=[pl.BlockSpec((pl.Element(d), pl.Element(d)), lambda g: (first_row(g), 0))],
        out_specs=pl.BlockSpec((d, d), lambda g: (0, g)),
        out_shape=jax.ShapeDtypeStruct((d, N_WIDE_GROUPS * d), BF16),
        compiler_params=pltpu.CompilerParams(dimension_semantics=("arbitrary",)),
        name="wprep",
    )(w_in_t)


def _gateconv_kernel(x_ref, shift_ref, scale_ref, nw_ref, w_ref, cw_ref, cb_ref,
                     cv_ref, sa_ref, sb_ref, ga_ref, gb_ref, sh_ref, hb_ref):
    k = pl.program_id(1)
    n_lc = CONV_WIDTH // CONV_LANES
    first = HALO - (CONV_KERNEL - 1)
    groups = RC // SUBLANES
    streams = range(GC_STREAMS)

    @pl.when(k == 0)
    def _():
        for st in streams:
            for lc in range(n_lc):
                sh_ref[st, lc, 0, 0:HALO, :] = jnp.zeros((HALO, CONV_LANES), F32)

    @pl.when(k > 0)
    def _():
        for st in streams:
            for lc in range(n_lc):
                sh_ref[st, lc, 0, 0:HALO, :] = sh_ref[st, lc, 0, TC:TC + HALO, :]

    def proj(st, group, piece):
        col = group * D_MODEL + piece * PROJ_COLS
        return jnp.dot(hb_ref[st], w_ref[:, col:col + PROJ_COLS], preferred_element_type=F32)

    for st in streams:
        h = _rms(x_ref[st], nw_ref[...]) * (1.0 + scale_ref[st]) + shift_ref[st]
        hb_ref[st] = h.astype(BF16)
        for piece in range(D_MODEL // PROJ_COLS):
            u = proj(st, G_VAL, piece) * _sigmoid(proj(st, G_GLU, piece))
            for j in range(PROJ_COLS // CONV_LANES):
                lc = piece * (PROJ_COLS // CONV_LANES) + j
                sh_ref[st, lc, 0, HALO:, :] = u[:, j * CONV_LANES:(j + 1) * CONV_LANES]

    def gate_piece(st, group, out_ref, act, piece):
        cols = slice(piece * PROJ_COLS, (piece + 1) * PROJ_COLS)
        out_ref[st, :, cols] = act(proj(st, group, piece)).astype(BF16)

    def shift_chunk(st, lc):
        for r in range(1, SUBLANES):
            for rb in range(0, SHIFT_ROWS, SHIFT_BLOCK):
                sh_ref[st, lc, r, rb:rb + SHIFT_BLOCK, :] = (
                    sh_ref[st, lc, 0, rb + r:rb + r + SHIFT_BLOCK, :])

    def conv_chunk(st, lc, c):
        row0 = c * RC
        lanes = slice(lc * CONV_LANES, (lc + 1) * CONV_LANES)
        acc = jnp.zeros((groups, SUBLANES, CONV_LANES), F32)
        for r in range(SUBLANES):
            offs = [o for o in range(first, first + CONV_KERNEL) if o % SUBLANES == r]
            g0, g1 = offs[0] // SUBLANES, offs[-1] // SUBLANES
            span = groups + g1 - g0
            start = row0 + g0 * SUBLANES
            xs = sh_ref[st, lc, r, start:start + span * SUBLANES, :]
            xs = xs.reshape(span, SUBLANES, CONV_LANES)
            for o in offs:
                g = o // SUBLANES - g0
                acc = acc + cw_ref[lc, o - first][None] * xs[g:g + groups]
        cv = acc.reshape(RC, CONV_LANES) + cb_ref[:, lanes]
        cv_ref[st, row0:row0 + RC, lanes] = cv.astype(BF16)

    def stream_work(st):
        mxu_work, vpu_work, text = [], [], []
        for group, out_ref, act in ((G_AGATE, sa_ref, _silu), (G_BGATE, sb_ref, _silu),
                                    (G_GA, ga_ref, _sigmoid), (G_GB, gb_ref, _sigmoid)):
            mxu_work += [functools.partial(gate_piece, st, group, out_ref, act, p)
                         for p in range(D_MODEL // PROJ_COLS)]
        for lc in range(n_lc):
            vpu_work.append(functools.partial(shift_chunk, st, lc))
            vpu_work += [functools.partial(conv_chunk, st, lc, c) for c in range(TC // RC)]
        issued = 0
        for i, mxu_fn in enumerate(mxu_work):
            target = (len(vpu_work) * (i + 1)) // len(mxu_work)
            text.append([mxu_fn] + vpu_work[issued:target])
            issued = target
        return text

    @pl.when(k >= 0)
    def _():
        for parts in zip(*[stream_work(st) for st in streams]):
            for part in parts:
                for fn in part:
                    fn()


def _gateconv(x, shift, scale, norm_w, w_wide, conv_w, conv_b):
    b, s, d = x.shape
    n_lc = d // CONV_LANES
    gs = GC_STREAMS
    tok = pl.BlockSpec((gs, TC, d), lambda bi, k: (bi, k, 0))
    mod = pl.BlockSpec((gs, 1, d), lambda bi, k: (bi, 0, 0))
    out = jax.ShapeDtypeStruct((b, s, d), BF16)
    return pl.pallas_call(
        _gateconv_kernel,
        grid=(b // gs, s // TC),
        in_specs=[tok, mod, mod, _const_spec((1, d)), _const_spec(w_wide.shape),
                  _const_spec(conv_w.shape), _const_spec((1, d))],
        out_specs=[tok] * 5,
        out_shape=[out] * 5,
        scratch_shapes=[pltpu.VMEM((gs, n_lc, SUBLANES, HALO + TC, CONV_LANES), F32),
                        pltpu.VMEM((gs, TC, d), BF16)],
        compiler_params=pltpu.CompilerParams(
            dimension_semantics=("arbitrary", "arbitrary"), vmem_limit_bytes=VMEM_LIMIT),
        name="gateconv",
    )(x, shift, scale, norm_w, w_wide, conv_w, conv_b)


def _rope_kernel(pos_ref, invf_ref, table_ref):
    ang = pos_ref[...] * invf_ref[...]
    cos_d, sin_d = jnp.cos(ang), jnp.sin(ang)
    n_freq = LANES // ROPE_PACK
    quarter = lax.broadcasted_iota(jnp.int32, ang.shape, 1) // n_freq
    for p in range(ROPE_PACK):
        def placed(src, q):
            return pltpu.roll(src, ((q - p) * n_freq) % LANES, 1)
        row = jnp.where(quarter == 0, placed(cos_d, 0),
                        jnp.where(quarter == 1, placed(cos_d, 1),
                                  jnp.where(quarter == 2, placed(sin_d, 2), placed(sin_d, 3))))
        table_ref[pl.ds(p, ang.shape[0], stride=ROPE_PACK), :] = row


def _rope_table(pos_dense, invf):
    rows = pos_dense.shape[0]
    tr = 512
    return pl.pallas_call(
        _rope_kernel,
        grid=(rows // tr,),
        in_specs=[pl.BlockSpec((tr, LANES), lambda i: (i, 0)), _const_spec(invf.shape)],
        out_specs=pl.BlockSpec((ROPE_PACK * tr, LANES), lambda i: (i, 0)),
        out_shape=jax.ShapeDtypeStruct((ROPE_PACK * rows, LANES), F32),
        compiler_params=pltpu.CompilerParams(dimension_semantics=("arbitrary",)),
        name="rope",
    )(pos_dense, invf)


def _mla_kernel(x_ref, shift_ref, scale_ref, nw_ref, table_ref, w_ref,
                qn_ref, kvn_ref, wuq_ref, wuk_ref, wvt_ref,
                q_ref, k_ref, vt_ref):
    hb = _modulated_norm(x_ref, shift_ref, scale_ref, nw_ref)
    lat = lax.dot_general(hb, w_ref[...], (((1,), (1,)), ((), ())),
                          preferred_element_type=F32)
    cq = lat[:, :Q_LORA_RANK]
    ckv = lat[:, Q_LORA_RANK:Q_LORA_RANK + KV_LORA_RANK]
    kp = lat[:, Q_LORA_RANK + KV_LORA_RANK:]

    table = table_ref[0]
    lane = lax.broadcasted_iota(jnp.int32, table.shape, 1)
    first_half = lane < QK_ROPE_DIM

    def rope(t):
        return jnp.where(first_half, t + pltpu.roll(t, QK_ROPE_DIM, 1), 0.0)

    cqn = _rms(cq, qn_ref[...]).astype(BF16)
    ckvn = _rms(ckv, kvn_ref[...]).astype(BF16)

    qa = jnp.dot(cqn, wuq_ref[...], preferred_element_type=F32)
    kn = jnp.dot(ckvn, wuk_ref[...], preferred_element_type=F32)
    vt = lax.dot_general(wvt_ref[...], ckvn, (((1,), (1,)), ((), ())),
                         preferred_element_type=F32)

    k_hi = rope(kp * table).astype(BF16)
    q_table = table * Q_SCALE
    for h in range(N_HEADS):
        blk = qa[:, h * QK_PAD:(h + 1) * QK_PAD]
        q_ref[0, h, :, :LANES] = (blk[:, :LANES] * Q_SCALE).astype(BF16)
        q_ref[0, h, :, LANES:] = rope(blk[:, LANES:] * q_table).astype(BF16)
        k_ref[0, h, :, :LANES] = kn[:, h * LANES:(h + 1) * LANES].astype(BF16)
        k_ref[0, h, :, LANES:] = k_hi
    ones = jnp.ones((V_PAD - V_DIM, TK), BF16)
    for j in range(TM // TK):
        for h in range(N_HEADS):
            vt_ref[0, j, h, :V_DIM, :] = vt[h * V_DIM:(h + 1) * V_DIM, j * TK:(j + 1) * TK].astype(BF16)
            vt_ref[0, j, h, V_DIM:, :] = ones


def _mla_proj(x, shift, scale, norm_w, table, w_lat_t, qn_w, kvn_w, w_uq2, w_uk, w_vt):
    b, s, d = x.shape
    tok = pl.BlockSpec((1, TM, d), lambda bi, i: (bi, i, 0))
    mod = pl.BlockSpec((1, 1, d), lambda bi, i: (bi, 0, 0))
    qk_spec = pl.BlockSpec((1, N_HEADS, TM, QK_PAD), lambda bi, i: (bi, 0, i, 0))
    return pl.pallas_call(
        _mla_kernel,
        grid=(b, s // TM),
        in_specs=[
            tok, mod, mod, _const_spec((1, d)),
            pl.BlockSpec((1, TM, LANES), lambda bi, i: (bi, i, 0)),
            _const_spec(w_lat_t.shape),
            _const_spec(qn_w.shape), _const_spec(kvn_w.shape),
            _const_spec(w_uq2.shape), _const_spec(w_uk.shape), _const_spec(w_vt.shape),
        ],
        out_specs=[
            qk_spec, qk_spec,
            pl.BlockSpec((1, TM // TK, N_HEADS, V_PAD, TK), lambda bi, i: (bi, i, 0, 0, 0)),
        ],
        out_shape=[
            jax.ShapeDtypeStruct((b, N_HEADS, s, QK_PAD), BF16),
            jax.ShapeDtypeStruct((b, N_HEADS, s, QK_PAD), BF16),
            jax.ShapeDtypeStruct((b, s // TK, N_HEADS, V_PAD, TK), BF16),
        ],
        compiler_params=pltpu.CompilerParams(
            dimension_semantics=("arbitrary", "arbitrary"), vmem_limit_bytes=VMEM_LIMIT),
        name="mla_proj",
    )(x, shift, scale, norm_w, table, w_lat_t, qn_w, kvn_w, w_uq2, w_uk, w_vt)


ATTN_BUFFERS = 3
ATTN_AHEAD = 2
ATTN_GROUP = 3
ATTN_HEADS = 2


def _attn_scores(q_ref, k_ref, bias_ref, hh, item, s_out, bm_out, diagonal):
    qn, jn = item
    k = k_ref[0, hh, pl.ds(pl.multiple_of(jn * TK, TK), TK), :]
    q = q_ref[0, hh, pl.ds(pl.multiple_of(qn * TQ, TQ), TQ), :]
    s = lax.dot_general(k, q, (((1,), (1,)), ((), ())), preferred_element_type=F32)
    if diagonal:
        s = s + bias_ref[...]
    s_out[...] = s
    bm_out[...] = jnp.max(s, axis=0, keepdims=True)


def _attn_accumulate(vt_ref, o_ref, acc_ref, hh, item, p_in, al_in, diagonal):
    qp, jp = item
    pv = jnp.dot(vt_ref[0, jp, hh], p_in[...], preferred_element_type=F32)
    acc = al_in[...] * acc_ref[qp] + pv
    acc_ref[qp] = acc
    if diagonal:
        rows = pl.ds(pl.multiple_of(qp * TQ, TQ), TQ)
        out = acc[:V_DIM] / acc[V_DIM:V_DIM + 1]
        o_ref[0, rows, hh * V_DIM:(hh + 1) * V_DIM] = jnp.transpose(out).astype(BF16)


def _attn_step(q_ref, k_ref, vt_ref, bias_ref, o_ref, acc_ref, m_ref, hh, cur, nxt, prv, items,
               diag):
    s_cur, p_cur, bm_cur, al_cur = cur
    s_nxt, _, bm_nxt, _ = nxt
    _, p_prv, _, al_prv = prv
    item_next, (qc, jc), item_prev = items
    diag_next, diag_prev = diag
    _attn_accumulate(vt_ref, o_ref, acc_ref, hh, item_prev, p_prv, al_prv, diag_prev)
    m_old = jnp.where(jc == 0, -jnp.inf, m_ref[qc])
    m_new = jnp.maximum(m_old, bm_cur[...])
    p_cur[...] = jnp.exp2(s_cur[...] - m_new).astype(BF16)
    al_cur[...] = jnp.exp2(m_old - m_new)
    m_ref[qc] = m_new
    _attn_scores(q_ref, k_ref, bias_ref, hh, item_next, s_nxt, bm_nxt, diag_next)


def _attn_item(t, n_lower):
    def lower(t):
        q = jnp.int32(1)
        for c in range(2, 64):
            if c * (c - 1) // 2 >= n_lower:
                break
            q = q + (t >= c * (c - 1) // 2).astype(jnp.int32)
        return q, t - lax.shift_right_logical(q * (q - 1), 1)
    ql, jl = lower(jnp.minimum(t, n_lower - 1))
    d = t - n_lower
    is_diag = t >= n_lower
    return jnp.where(is_diag, d, ql), jnp.where(is_diag, d, jl)


def _attn_kernel(q_ref, k_ref, vt_ref, bias_ref, o_ref, *bufs):
    n_q = q_ref.shape[2] // TQ
    n_lower = n_q * (n_q - 1) // 2
    n_items = n_lower + n_q
    nbuf = ATTN_BUFFERS
    heads = range(ATTN_HEADS)
    assert TQ == TK and ATTN_GROUP % nbuf == 0 and n_items % nbuf == 0
    per_head = len(bufs) // ATTN_HEADS
    accs = tuple(bufs[hh * per_head] for hh in heads)
    maxs = tuple(bufs[hh * per_head + 1] for hh in heads)
    rings = tuple(tuple(bufs[hh * per_head + 2 + 4 * r:hh * per_head + 6 + 4 * r]
                        for r in range(nbuf)) for hh in heads)

    def item(t):
        return _attn_item(jnp.clip(t, 0, n_items - 1), n_lower)

    def steps(t0, count, t0_static=None):
        for u in range(count):
            t = t0 + u
            diag = (False, False) if t0_static is None else (
                t0_static + u + ATTN_AHEAD >= n_lower, t0_static + u - 1 >= n_lower)
            items = (item(t + ATTN_AHEAD), item(t), item(t - 1))
            for hh in heads:
                ring = rings[hh]
                _attn_step(q_ref, k_ref, vt_ref, bias_ref, o_ref, accs[hh], maxs[hh], hh,
                           ring[u % nbuf], ring[(u + ATTN_AHEAD) % nbuf], ring[(u - 1) % nbuf],
                           items, diag)

    for hh in heads:
        accs[hh][...] = jnp.zeros_like(accs[hh])
        maxs[hh][...] = jnp.zeros_like(maxs[hh])
        _, p_last, _, al_last = rings[hh][nbuf - 1]
        p_last[...] = jnp.zeros_like(p_last)
        al_last[...] = jnp.ones_like(al_last)
        for a in range(ATTN_AHEAD):
            s_buf, _, bm_buf, _ = rings[hh][a]
            _attn_scores(q_ref, k_ref, bias_ref, hh, item(jnp.int32(a)), s_buf, bm_buf, False)

    n_plain = (n_lower - ATTN_AHEAD) // ATTN_GROUP
    lax.fori_loop(0, n_plain, lambda g, c: (steps(g * ATTN_GROUP, ATTN_GROUP), c)[1], 0)
    t_tail = n_plain * ATTN_GROUP
    steps(jnp.int32(t_tail), n_items - t_tail, t_tail)
    for hh in heads:
        _, p_last, _, al_last = rings[hh][(n_items - 1) % nbuf]
        _attn_accumulate(vt_ref, o_ref, accs[hh], hh, item(jnp.int32(n_items - 1)), p_last,
                         al_last, True)


def _attention(q, k, vt):
    b, h, s, _ = q.shape
    n_q = s // TQ
    row = lax.broadcasted_iota(jnp.int32, (TK, TQ), 0)
    col = lax.broadcasted_iota(jnp.int32, (TK, TQ), 1)
    bias = jnp.where(row <= col, 0.0, -jnp.inf).astype(F32)
    stat = pltpu.VMEM((1, TQ), F32)
    slot = [pltpu.VMEM((TK, TQ), F32), pltpu.VMEM((TK, TQ), BF16), stat, stat]
    head_scratch = ([pltpu.VMEM((n_q, V_PAD, TQ), F32), pltpu.VMEM((n_q, 1, TQ), F32)]
                    + slot * ATTN_BUFFERS)
    hs = ATTN_HEADS
    return pl.pallas_call(
        _attn_kernel,
        grid=(b, h // hs),
        in_specs=[
            pl.BlockSpec((1, hs, s, QK_PAD), lambda bi, hi: (bi, hi, 0, 0)),
            pl.BlockSpec((1, hs, s, QK_PAD), lambda bi, hi: (bi, hi, 0, 0)),
            pl.BlockSpec((1, s // TK, hs, V_PAD, TK), lambda bi, hi: (bi, 0, hi, 0, 0)),
            _const_spec(bias.shape),
        ],
        out_specs=pl.BlockSpec((1, s, hs * V_DIM), lambda bi, hi: (bi, 0, hi)),
        out_shape=jax.ShapeDtypeStruct((b, s, h * V_DIM), BF16),
        scratch_shapes=head_scratch * hs,
        compiler_params=pltpu.CompilerParams(
            dimension_semantics=("arbitrary", "arbitrary"),
            vmem_limit_bytes=VMEM_LIMIT),
        name="attn",
    )(q, k, vt, bias)


def _out_kernel(cv_ref, sa_ref, o_ref, sb_ref, ga_ref, gb_ref, x_ref, gate_ref, lw_ref, lb_ref,
                wco_ref, wao_ref, wo_ref, fw_ref, y_ref):
    for st in range(OUT_STREAMS):
        cv = cv_ref[st].astype(F32)
        mu = jnp.mean(cv, axis=-1, keepdims=True)
        xc = cv - mu
        var = jnp.mean(xc * xc, axis=-1, keepdims=True)
        ln = xc * lax.rsqrt(var + EPS) * lw_ref[...] + lb_ref[...]
        ug = (_silu(ln) * sa_ref[st].astype(F32)).astype(BF16)
        ya = jnp.dot(ug, wco_ref[...], preferred_element_type=F32)
        og = (o_ref[st].astype(F32) * sb_ref[st].astype(F32)).astype(BF16)
        yb = jnp.dot(og, wao_ref[...], preferred_element_type=F32)
        merged = ga_ref[st].astype(F32) * ya + gb_ref[st].astype(F32) * yb
        delta = jnp.dot(merged.astype(BF16), wo_ref[...], preferred_element_type=F32)
        y = x_ref[st] + gate_ref[st] * delta
        y_ref[st] = _rms(y, fw_ref[...])


def _out(cv, sa, o, sb, ga, gb, x, gate, ln_w, ln_b, w_co, w_ao, w_o, fw):
    b, s, d = x.shape
    tok = pl.BlockSpec((OUT_STREAMS, TO, d), lambda bi, i: (bi, i, 0))
    mod = pl.BlockSpec((OUT_STREAMS, 1, d), lambda bi, i: (bi, 0, 0))
    wspec = _const_spec((d, d))
    vec = _const_spec((1, d))
    return pl.pallas_call(
        _out_kernel,
        grid=(b // OUT_STREAMS, s // TO),
        in_specs=[tok, tok, tok, tok, tok, tok, tok, mod, vec, vec, wspec, wspec, wspec, vec],
        out_specs=tok,
        out_shape=jax.ShapeDtypeStruct((b, s, d), F32),
        compiler_params=pltpu.CompilerParams(
            dimension_semantics=("arbitrary", "arbitrary"), vmem_limit_bytes=VMEM_LIMIT),
        name="out",
    )(cv, sa, o, sb, ga, gb, x, gate, ln_w, ln_b, w_co, w_ao, w_o, fw)


def _rot_cols(w):
    half = w.shape[-1] // 2
    return jnp.concatenate([-w[..., half:], w[..., :half]], axis=-1)


def _layer(x, c_act_pad, table, w_ada, b_ada, norm_w, w_in, conv_w, conv_b, conv_ln_w,
           conv_ln_b, w_conv_out, q_norm_w, w_uq, kv_norm_w, w_ukv, w_attn_out, w_out, final_w):
    b, s, d = x.shape
    mod = _ada(c_act_pad, w_ada, b_ada[None, :])[:b]
    shift, scale, gate = (m[:, None, :] for m in jnp.split(mod, 3, axis=-1))

    assert w_in.shape[1] == IN_COLS
    w_in_t = w_in.T
    w_wide = _wprep(w_in_t)
    w_kpe_t = w_in_t[IN_COL_BGATE - QK_ROPE_DIM:IN_COL_BGATE]
    half = QK_ROPE_DIM // 2
    w_lat_t = jnp.concatenate([w_in_t[IN_COL_LATENT:IN_COL_BGATE], -w_kpe_t[half:], w_kpe_t[:half]],
                              axis=0)
    w_lat_t = lax.optimization_barrier(w_lat_t).astype(BF16)

    wq = w_uq.reshape(Q_LORA_RANK, N_HEADS, QK_NOPE_DIM + QK_ROPE_DIM)
    wq_rope = wq[..., QK_NOPE_DIM:]
    w_uq2 = jnp.concatenate([wq, _rot_cols(wq_rope)], axis=-1)
    w_uq2 = w_uq2.reshape(Q_LORA_RANK, N_HEADS * QK_PAD).astype(BF16)
    wkv = w_ukv.reshape(KV_LORA_RANK, N_HEADS, QK_NOPE_DIM + V_DIM)
    w_uk = wkv[..., :QK_NOPE_DIM].reshape(KV_LORA_RANK, N_HEADS * QK_NOPE_DIM).astype(BF16)
    w_vt = wkv[..., QK_NOPE_DIM:].reshape(KV_LORA_RANK, ATTN_WIDTH).T.astype(BF16)

    nw = norm_w[None, :]
    cw = conv_w.reshape(CONV_KERNEL, CONV_WIDTH // CONV_LANES, 1, CONV_LANES).transpose(1, 0, 2, 3)
    cw = jnp.broadcast_to(cw, (CONV_WIDTH // CONV_LANES, CONV_KERNEL, SUBLANES, CONV_LANES))
    cv, sa, sb, ga, gb = _gateconv(x, shift, scale, nw, w_wide, cw, conv_b[None, :])
    q, k, vt = _mla_proj(x, shift, scale, nw, table, w_lat_t, q_norm_w[None, :],
                         kv_norm_w[None, :], w_uq2, w_uk, w_vt)
    o = _attention(q, k, vt)
    return _out(cv, sa, o, sb, ga, gb, x, gate, conv_ln_w[None, :], conv_ln_b[None, :],
                w_conv_out.astype(BF16), w_attn_out.astype(BF16), w_out.astype(BF16),
                final_w[None, :])


def kernel(x, c, positions, w_ada, b_ada, norm_w, w_in, conv_w, conv_b, conv_ln_w, conv_ln_b,
           w_conv_out, q_norm_w, w_uq, kv_norm_w, w_ukv, w_attn_out, w_out, final_norm_w):
    b, s, d = x.shape
    depth = w_ada.shape[0]
    assert depth == 1, "final rmsnorm is fused into the single layer's output kernel"
    inv_freq = ROPE_THETA ** (-jnp.arange(0, QK_ROPE_DIM, 2, dtype=F32) / QK_ROPE_DIM)
    n_freq = inv_freq.shape[0]
    assert ROPE_PACK * n_freq == LANES
    invf = jnp.tile(inv_freq, ROPE_PACK)[None, :]
    pos_dense = jnp.repeat(positions.astype(F32).reshape(b * s // ROPE_PACK, ROPE_PACK), n_freq, axis=1)
    table = _rope_table(pos_dense, invf).reshape(b, s, LANES)
    c_pad = jnp.pad(c, ((0, 8 - b), (0, 0)))
    return _layer(x, c_pad, table, w_ada[0], b_ada[0], norm_w[0], w_in[0], conv_w[0],
                  conv_b[0], conv_ln_w[0], conv_ln_b[0], w_conv_out[0], q_norm_w[0], w_uq[0],
                  kv_norm_w[0], w_ukv[0], w_attn_out[0], w_out[0], final_norm_w)
```

```python
import functools
import math

import jax
import jax.numpy as jnp
from jax import lax
from jax.experimental import pallas as pl
from jax.experimental.pallas import tpu as pltpu

F32 = jnp.float32
BF16 = jnp.bfloat16

D_MODEL = 1024
CONV_WIDTH = 1024
CONV_KERNEL = 31
N_HEADS = 8
QK_NOPE_DIM = 128
QK_ROPE_DIM = 64
V_DIM = 128
Q_LORA_RANK = 256
KV_LORA_RANK = 256
ATTN_WIDTH = N_HEADS * V_DIM
ROPE_THETA = 10000.0
EPS = 1e-6

LANES = 128
QK_PAD = 2 * LANES
BF16_SUBLANES = 16
V_PAD = V_DIM + BF16_SUBLANES
ROPE_PACK = LANES // (QK_ROPE_DIM // 2)
HALO = 32
VMEM_LIMIT = 56 * 1024 * 1024

Q_SCALE = (QK_NOPE_DIM + QK_ROPE_DIM) ** -0.5 * math.log2(math.e)

IN_COL_LATENT = 3 * CONV_WIDTH
IN_COL_BGATE = IN_COL_LATENT + Q_LORA_RANK + KV_LORA_RANK + QK_ROPE_DIM
IN_COLS = IN_COL_BGATE + ATTN_WIDTH + 2 * D_MODEL
G_VAL, G_GLU, G_AGATE, G_BGATE, G_GA, G_GB = range(6)
N_WIDE_GROUPS = 6
assert CONV_WIDTH == ATTN_WIDTH == D_MODEL

TM = 1024
TC = 256
RC = 16
CONV_LANES = 1024
PROJ_COLS = 1024
GC_STREAMS = 2
TO = 512
OUT_STREAMS = 1
SUBLANES = 8
SHIFT_ROWS = TC + HALO - SUBLANES
SHIFT_BLOCK = 40
assert SHIFT_ROWS % SHIFT_BLOCK == 0 and SHIFT_BLOCK % SUBLANES == 0
assert PROJ_COLS % CONV_LANES == 0 and D_MODEL % PROJ_COLS == 0 and TC % RC == 0
TQ = 512
TK = 512


def _sigmoid(x):
    return 1.0 / (1.0 + jnp.exp(-x))


def _silu(x):
    return x * _sigmoid(x)


def _rms(x, w):
    return x * lax.rsqrt(jnp.mean(x * x, axis=-1, keepdims=True) + EPS) * w


def _const_spec(shape):
    return pl.BlockSpec(shape, lambda *_: (0,) * len(shape), pipeline_mode=pl.Buffered(1))


def _ada_kernel(c_ref, w_ref, b_ref, o_ref):
    c = c_ref[...]
    o_ref[...] = jnp.dot(_silu(c), w_ref[...], preferred_element_type=F32) + b_ref[...]


def _ada(c_pad, w_ada, b_ada):
    rows, d = c_pad.shape
    n = w_ada.shape[1]
    tn = 512
    return pl.pallas_call(
        _ada_kernel,
        grid=(n // tn,),
        in_specs=[
            pl.BlockSpec((rows, d), lambda j: (0, 0)),
            pl.BlockSpec((d, tn), lambda j: (0, j)),
            pl.BlockSpec((1, tn), lambda j: (0, j)),
        ],
        out_specs=pl.BlockSpec((rows, tn), lambda j: (0, j)),
        out_shape=jax.ShapeDtypeStruct((rows, n), F32),
        compiler_params=pltpu.CompilerParams(dimension_semantics=("arbitrary",)),
        name="ada",
    )(c_pad, w_ada, b_ada)


def _modulated_norm(x_ref, shift_ref, scale_ref, nw_ref):
    x = x_ref[0]
    h = _rms(x, nw_ref[...]) * (1.0 + scale_ref[0]) + shift_ref[0]
    return h.astype(BF16)


def _wprep_kernel(wt_ref, out_ref):
    blk = wt_ref[...].astype(BF16)
    n = blk.shape[1]
    eye = (lax.broadcasted_iota(jnp.int32, (n, n), 0)
           == lax.broadcasted_iota(jnp.int32, (n, n), 1)).astype(BF16)
    out_ref[...] = lax.dot_general(eye, blk, (((1,), (1,)), ((), ())),
                                   preferred_element_type=F32).astype(BF16)


def _wprep(w_in_t):
    d = w_in_t.shape[1]

    def first_row(g):
        row = jnp.where(g < G_BGATE, g * d, IN_COL_BGATE + (g - G_BGATE) * d)
        return pl.multiple_of(row, math.gcd(d, IN_COL_BGATE))

    return pl.pallas_call(
        _wprep_kernel,
        grid=(N_WIDE_GROUPS,),
        in_specs=[pl.BlockSpec((pl.Element(d), pl.Element(d)), lambda g: (first_row(g), 0))],
        out_specs=pl.BlockSpec((d, d), lambda g: (0, g)),
        out_shape=jax.ShapeDtypeStruct((d, N_WIDE_GROUPS * d), BF16),
        compiler_params=pltpu.CompilerParams(dimension_semantics=("arbitrary",)),
        name="wprep",
    )(w_in_t)


def _gateconv_kernel(x_ref, shift_ref, scale_ref, nw_ref, w_ref, cw_ref, cb_ref, z_ref,
                     cv_ref, sa_ref, sb_ref, ga_ref, gb_ref, sh_ref, hb_ref, dep_ref):
    k = pl.program_id(1)
    n_lc = CONV_WIDTH // CONV_LANES
    first = HALO - (CONV_KERNEL - 1)
    groups = RC // SUBLANES
    streams = range(GC_STREAMS)

    @pl.when(k == 0)
    def _():
        for st in streams:
            for lc in range(n_lc):
                sh_ref[st, lc, 0, 0:HALO, :] = jnp.zeros((HALO, CONV_LANES), F32)

    @pl.when(k > 0)
    def _():
        for st in streams:
            for lc in range(n_lc):
                sh_ref[st, lc, 0, 0:HALO, :] = sh_ref[st, lc, 0, TC:TC + HALO, :]

    def proj(st, group, piece):
        col = group * D_MODEL + piece * PROJ_COLS
        return jnp.dot(hb_ref[st], w_ref[:, col:col + PROJ_COLS], preferred_element_type=F32)

    for st in streams:
        h = _rms(x_ref[st], nw_ref[...]) * (1.0 + scale_ref[st]) + shift_ref[st]
        hb_ref[st] = h.astype(BF16)
        for piece in range(D_MODEL // PROJ_COLS):
            u = proj(st, G_VAL, piece) * _sigmoid(proj(st, G_GLU, piece))
            for j in range(PROJ_COLS // CONV_LANES):
                lc = piece * (PROJ_COLS // CONV_LANES) + j
                sh_ref[st, lc, 0, HALO:, :] = u[:, j * CONV_LANES:(j + 1) * CONV_LANES]

    def gate_piece(st, group, out_ref, act, piece, after_conv):
        if after_conv:
            zero = (pltpu.bitcast(dep_ref[st], jnp.int32) & z_ref[...]).astype(F32)
            tile = hb_ref[st, 0:BF16_SUBLANES, 0:LANES].astype(F32)
            hb_ref[st, 0:BF16_SUBLANES, 0:LANES] = (
                tile + jnp.concatenate([zero, zero], axis=0)).astype(BF16)
        cols = slice(piece * PROJ_COLS, (piece + 1) * PROJ_COLS)
        out_ref[st, :, cols] = act(proj(st, group, piece)).astype(BF16)

    def shift_chunk(st, lc):
        for r in range(1, SUBLANES):
            for rb in range(0, SHIFT_ROWS, SHIFT_BLOCK):
                sh_ref[st, lc, r, rb:rb + SHIFT_BLOCK, :] = (
                    sh_ref[st, lc, 0, rb + r:rb + r + SHIFT_BLOCK, :])

    def conv_chunk(st, lc, c):
        row0 = c * RC
        lanes = slice(lc * CONV_LANES, (lc + 1) * CONV_LANES)
        acc = jnp.zeros((groups, SUBLANES, CONV_LANES), F32)
        for r in range(SUBLANES):
            offs = [o for o in range(first, first + CONV_KERNEL) if o % SUBLANES == r]
            g0, g1 = offs[0] // SUBLANES, offs[-1] // SUBLANES
            span = groups + g1 - g0
            start = row0 + g0 * SUBLANES
            xs = sh_ref[st, lc, r, start:start + span * SUBLANES, :]
            xs = xs.reshape(span, SUBLANES, CONV_LANES)
            for o in offs:
                g = o // SUBLANES - g0
                acc = acc + cw_ref[lc, o - first][None] * xs[g:g + groups]
        cv = acc.reshape(RC, CONV_LANES) + cb_ref[:, lanes]
        cv_ref[st, row0:row0 + RC, lanes] = cv.astype(BF16)
        dep_ref[st] = cv[0:SUBLANES, 0:LANES]

    def stream_work(st):
        mxu_work, vpu_work, text = [], [], []
        for group, out_ref, act in ((G_AGATE, sa_ref, _silu), (G_BGATE, sb_ref, _silu),
                                    (G_GA, ga_ref, _sigmoid), (G_GB, gb_ref, _sigmoid)):
            mxu_work += [functools.partial(gate_piece, st, group, out_ref, act, p,
                                           len(mxu_work) + p > 0)
                         for p in range(D_MODEL // PROJ_COLS)]
        for lc in range(n_lc):
            vpu_work.append(functools.partial(shift_chunk, st, lc))
            vpu_work += [functools.partial(conv_chunk, st, lc, c) for c in range(TC // RC)]
        issued = 0
        for i, mxu_fn in enumerate(mxu_work):
            target = (len(vpu_work) * (i + 1)) // len(mxu_work)
            text.append([mxu_fn] + vpu_work[issued:target])
            issued = target
        return text

    @pl.when(k >= 0)
    def _():
        for parts in zip(*[stream_work(st) for st in streams]):
            for part in parts:
                for fn in part:
                    fn()


def _gateconv(x, shift, scale, norm_w, w_wide, conv_w, conv_b):
    b, s, d = x.shape
    n_lc = d // CONV_LANES
    gs = GC_STREAMS
    tok = pl.BlockSpec((gs, TC, d), lambda bi, k: (bi, k, 0))
    mod = pl.BlockSpec((gs, 1, d), lambda bi, k: (bi, 0, 0))
    out = jax.ShapeDtypeStruct((b, s, d), BF16)
    run_time_zero = jnp.zeros((SUBLANES, LANES), jnp.int32)
    return pl.pallas_call(
        _gateconv_kernel,
        grid=(b // gs, s // TC),
        in_specs=[tok, mod, mod, _const_spec((1, d)), _const_spec(w_wide.shape),
                  _const_spec(conv_w.shape), _const_spec((1, d)),
                  _const_spec((SUBLANES, LANES))],
        out_specs=[tok] * 5,
        out_shape=[out] * 5,
        scratch_shapes=[pltpu.VMEM((gs, n_lc, SUBLANES, HALO + TC, CONV_LANES), F32),
                        pltpu.VMEM((gs, TC, d), BF16), pltpu.VMEM((gs, SUBLANES, LANES), F32)],
        compiler_params=pltpu.CompilerParams(
            dimension_semantics=("arbitrary", "arbitrary"), vmem_limit_bytes=VMEM_LIMIT),
        name="gateconv",
    )(x, shift, scale, norm_w, w_wide, conv_w, conv_b, run_time_zero)


def _rope_kernel(pos_ref, invf_ref, table_ref):
    ang = pos_ref[...] * invf_ref[...]
    cos_d, sin_d = jnp.cos(ang), jnp.sin(ang)
    n_freq = LANES // ROPE_PACK
    quarter = lax.broadcasted_iota(jnp.int32, ang.shape, 1) // n_freq
    for p in range(ROPE_PACK):
        def placed(src, q):
            return pltpu.roll(src, ((q - p) * n_freq) % LANES, 1)
        row = jnp.where(quarter == 0, placed(cos_d, 0),
                        jnp.where(quarter == 1, placed(cos_d, 1),
                                  jnp.where(quarter == 2, placed(sin_d, 2), placed(sin_d, 3))))
        table_ref[pl.ds(p, ang.shape[0], stride=ROPE_PACK), :] = row


def _rope_table(pos_dense, invf):
    rows = pos_dense.shape[0]
    tr = 512
    return pl.pallas_call(
        _rope_kernel,
        grid=(rows // tr,),
        in_specs=[pl.BlockSpec((tr, LANES), lambda i: (i, 0)), _const_spec(invf.shape)],
        out_specs=pl.BlockSpec((ROPE_PACK * tr, LANES), lambda i: (i, 0)),
        out_shape=jax.ShapeDtypeStruct((ROPE_PACK * rows, LANES), F32),
        compiler_params=pltpu.CompilerParams(dimension_semantics=("arbitrary",)),
        name="rope",
    )(pos_dense, invf)


def _mla_kernel(x_ref, shift_ref, scale_ref, nw_ref, table_ref, w_ref,
                qn_ref, kvn_ref, wuq_ref, wuk_ref, wvt_ref,
                q_ref, k_ref, vt_ref):
    hb = _modulated_norm(x_ref, shift_ref, scale_ref, nw_ref)
    lat = lax.dot_general(hb, w_ref[...], (((1,), (1,)), ((), ())),
                          preferred_element_type=F32)
    cq = lat[:, :Q_LORA_RANK]
    ckv = lat[:, Q_LORA_RANK:Q_LORA_RANK + KV_LORA_RANK]
    kp = lat[:, Q_LORA_RANK + KV_LORA_RANK:]

    table = table_ref[0]
    lane = lax.broadcasted_iota(jnp.int32, table.shape, 1)
    first_half = lane < QK_ROPE_DIM

    def rope(t):
        return jnp.where(first_half, t + pltpu.roll(t, QK_ROPE_DIM, 1), 0.0)

    cqn = _rms(cq, qn_ref[...]).astype(BF16)
    ckvn = _rms(ckv, kvn_ref[...]).astype(BF16)

    qa = jnp.dot(cqn, wuq_ref[...], preferred_element_type=F32)
    kn = jnp.dot(ckvn, wuk_ref[...], preferred_element_type=F32)
    vt = lax.dot_general(wvt_ref[...], ckvn, (((1,), (1,)), ((), ())),
                         preferred_element_type=F32)

    k_hi = rope(kp * table).astype(BF16)
    q_table = table * Q_SCALE
    for h in range(N_HEADS):
        blk = qa[:, h * QK_PAD:(h + 1) * QK_PAD]
        q_ref[0, h, :, :LANES] = (blk[:, :LANES] * Q_SCALE).astype(BF16)
        q_ref[0, h, :, LANES:] = rope(blk[:, LANES:] * q_table).astype(BF16)
        k_ref[0, h, :, :LANES] = kn[:, h * LANES:(h + 1) * LANES].astype(BF16)
        k_ref[0, h, :, LANES:] = k_hi
    ones = jnp.ones((V_PAD - V_DIM, TK), BF16)
    for j in range(TM // TK):
        for h in range(N_HEADS):
            vt_ref[0, j, h, :V_DIM, :] = vt[h * V_DIM:(h + 1) * V_DIM, j * TK:(j + 1) * TK].astype(BF16)
            vt_ref[0, j, h, V_DIM:, :] = ones


def _mla_proj(x, shift, scale, norm_w, table, w_lat_t, qn_w, kvn_w, w_uq2, w_uk, w_vt):
    b, s, d = x.shape
    tok = pl.BlockSpec((1, TM, d), lambda bi, i: (bi, i, 0))
    mod = pl.BlockSpec((1, 1, d), lambda bi, i: (bi, 0, 0))
    qk_spec = pl.BlockSpec((1, N_HEADS, TM, QK_PAD), lambda bi, i: (bi, 0, i, 0))
    return pl.pallas_call(
        _mla_kernel,
        grid=(b, s // TM),
        in_specs=[
            tok, mod, mod, _const_spec((1, d)),
            pl.BlockSpec((1, TM, LANES), lambda bi, i: (bi, i, 0)),
            _const_spec(w_lat_t.shape),
            _const_spec(qn_w.shape), _const_spec(kvn_w.shape),
            _const_spec(w_uq2.shape), _const_spec(w_uk.shape), _const_spec(w_vt.shape),
        ],
        out_specs=[
            qk_spec, qk_spec,
            pl.BlockSpec((1, TM // TK, N_HEADS, V_PAD, TK), lambda bi, i: (bi, i, 0, 0, 0)),
        ],
        out_shape=[
            jax.ShapeDtypeStruct((b, N_HEADS, s, QK_PAD), BF16),
            jax.ShapeDtypeStruct((b, N_HEADS, s, QK_PAD), BF16),
            jax.ShapeDtypeStruct((b, s // TK, N_HEADS, V_PAD, TK), BF16),
        ],
        compiler_params=pltpu.CompilerParams(
            dimension_semantics=("arbitrary", "arbitrary"), vmem_limit_bytes=VMEM_LIMIT),
        name="mla_proj",
    )(x, shift, scale, norm_w, table, w_lat_t, qn_w, kvn_w, w_uq2, w_uk, w_vt)


ATTN_BUFFERS = 3
ATTN_AHEAD = 2
ATTN_GROUP = 3
ATTN_HEADS = 2


def _attn_scores(q_ref, k_ref, bias_ref, hh, item, s_out, bm_out, diagonal):
    qn, jn = item
    k = k_ref[0, hh, pl.ds(pl.multiple_of(jn * TK, TK), TK), :]
    q = q_ref[0, hh, pl.ds(pl.multiple_of(qn * TQ, TQ), TQ), :]
    s = lax.dot_general(k, q, (((1,), (1,)), ((), ())), preferred_element_type=F32)
    if diagonal:
        s = s + bias_ref[...]
    s_out[...] = s
    bm_out[...] = jnp.max(s, axis=0, keepdims=True)


def _attn_accumulate(vt_ref, o_ref, acc_ref, hh, item, p_in, al_in, diagonal):
    qp, jp = item
    pv = jnp.dot(vt_ref[0, jp, hh], p_in[...], preferred_element_type=F32)
    acc = al_in[...] * acc_ref[qp] + pv
    acc_ref[qp] = acc
    if diagonal:
        rows = pl.ds(pl.multiple_of(qp * TQ, TQ), TQ)
        out = acc[:V_DIM] / acc[V_DIM:V_DIM + 1]
        o_ref[0, rows, hh * V_DIM:(hh + 1) * V_DIM] = jnp.transpose(out).astype(BF16)


def _attn_step(q_ref, k_ref, vt_ref, bias_ref, o_ref, acc_ref, m_ref, hh, cur, nxt, prv, items,
               diag):
    s_cur, p_cur, bm_cur, al_cur = cur
    s_nxt, _, bm_nxt, _ = nxt
    _, p_prv, _, al_prv = prv
    item_next, (qc, jc), item_prev = items
    diag_next, diag_prev = diag
    _attn_accumulate(vt_ref, o_ref, acc_ref, hh, item_prev, p_prv, al_prv, diag_prev)
    m_old = jnp.where(jc == 0, -jnp.inf, m_ref[qc])
    m_new = jnp.maximum(m_old, bm_cur[...])
    p_cur[...] = jnp.exp2(s_cur[...] - m_new).astype(BF16)
    al_cur[...] = jnp.exp2(m_old - m_new)
    m_ref[qc] = m_new
    _attn_scores(q_ref, k_ref, bias_ref, hh, item_next, s_nxt, bm_nxt, diag_next)


def _attn_item(t, n_lower):
    def lower(t):
        q = jnp.int32(1)
        for c in range(2, 64):
            if c * (c - 1) // 2 >= n_lower:
                break
            q = q + (t >= c * (c - 1) // 2).astype(jnp.int32)
        return q, t - lax.shift_right_logical(q * (q - 1), 1)
    ql, jl = lower(jnp.minimum(t, n_lower - 1))
    d = t - n_lower
    is_diag = t >= n_lower
    return jnp.where(is_diag, d, ql), jnp.where(is_diag, d, jl)


def _attn_kernel(q_ref, k_ref, vt_ref, bias_ref, o_ref, *bufs):
    n_q = q_ref.shape[2] // TQ
    n_lower = n_q * (n_q - 1) // 2
    n_items = n_lower + n_q
    nbuf = ATTN_BUFFERS
    heads = range(ATTN_HEADS)
    assert TQ == TK and ATTN_GROUP % nbuf == 0 and n_items % nbuf == 0
    per_head = len(bufs) // ATTN_HEADS
    accs = tuple(bufs[hh * per_head] for hh in heads)
    maxs = tuple(bufs[hh * per_head + 1] for hh in heads)
    rings = tuple(tuple(bufs[hh * per_head + 2 + 4 * r:hh * per_head + 6 + 4 * r]
                        for r in range(nbuf)) for hh in heads)

    def item(t):
        return _attn_item(jnp.clip(t, 0, n_items - 1), n_lower)

    def steps(t0, count, t0_static=None):
        for u in range(count):
            t = t0 + u
            diag = (False, False) if t0_static is None else (
                t0_static + u + ATTN_AHEAD >= n_lower, t0_static + u - 1 >= n_lower)
            items = (item(t + ATTN_AHEAD), item(t), item(t - 1))
            for hh in heads:
                ring = rings[hh]
                _attn_step(q_ref, k_ref, vt_ref, bias_ref, o_ref, accs[hh], maxs[hh], hh,
                           ring[u % nbuf], ring[(u + ATTN_AHEAD) % nbuf], ring[(u - 1) % nbuf],
                           items, diag)

    for hh in heads:
        accs[hh][...] = jnp.zeros_like(accs[hh])
        maxs[hh][...] = jnp.zeros_like(maxs[hh])
        _, p_last, _, al_last = rings[hh][nbuf - 1]
        p_last[...] = jnp.zeros_like(p_last)
        al_last[...] = jnp.ones_like(al_last)
        for a in range(ATTN_AHEAD):
            s_buf, _, bm_buf, _ = rings[hh][a]
            _attn_scores(q_ref, k_ref, bias_ref, hh, item(jnp.int32(a)), s_buf, bm_buf, False)

    n_plain = (n_lower - ATTN_AHEAD) // ATTN_GROUP
    lax.fori_loop(0, n_plain, lambda g, c: (steps(g * ATTN_GROUP, ATTN_GROUP), c)[1], 0)
    t_tail = n_plain * ATTN_GROUP
    steps(jnp.int32(t_tail), n_items - t_tail, t_tail)
    for hh in heads:
        _, p_last, _, al_last = rings[hh][(n_items - 1) % nbuf]
        _attn_accumulate(vt_ref, o_ref, accs[hh], hh, item(jnp.int32(n_items - 1)), p_last,
                         al_last, True)


def _attention(q, k, vt):
    b, h, s, _ = q.shape
    n_q = s // TQ
    row = lax.broadcasted_iota(jnp.int32, (TK, TQ), 0)
    col = lax.broadcasted_iota(jnp.int32, (TK, TQ), 1)
    bias = jnp.where(row <= col, 0.0, -jnp.inf).astype(F32)
    stat = pltpu.VMEM((1, TQ), F32)
    slot = [pltpu.VMEM((TK, TQ), F32), pltpu.VMEM((TK, TQ), BF16), stat, stat]
    head_scratch = ([pltpu.VMEM((n_q, V_PAD, TQ), F32), pltpu.VMEM((n_q, 1, TQ), F32)]
                    + slot * ATTN_BUFFERS)
    hs = ATTN_HEADS
    return pl.pallas_call(
        _attn_kernel,
        grid=(b, h // hs),
        in_specs=[
            pl.BlockSpec((1, hs, s, QK_PAD), lambda bi, hi: (bi, hi, 0, 0)),
            pl.BlockSpec((1, hs, s, QK_PAD), lambda bi, hi: (bi, hi, 0, 0)),
            pl.BlockSpec((1, s // TK, hs, V_PAD, TK), lambda bi, hi: (bi, 0, hi, 0, 0)),
            _const_spec(bias.shape),
        ],
        out_specs=pl.BlockSpec((1, s, hs * V_DIM), lambda bi, hi: (bi, 0, hi)),
        out_shape=jax.ShapeDtypeStruct((b, s, h * V_DIM), BF16),
        scratch_shapes=head_scratch * hs,
        compiler_params=pltpu.CompilerParams(
            dimension_semantics=("arbitrary", "arbitrary"),
            vmem_limit_bytes=VMEM_LIMIT),
        name="attn",
    )(q, k, vt, bias)


def _out_kernel(cv_ref, sa_ref, o_ref, sb_ref, ga_ref, gb_ref, x_ref, gate_ref, lw_ref, lb_ref,
                wco_ref, wao_ref, wo_ref, fw_ref, y_ref):
    for st in range(OUT_STREAMS):
        cv = cv_ref[st].astype(F32)
        mu = jnp.mean(cv, axis=-1, keepdims=True)
        xc = cv - mu
        var = jnp.mean(xc * xc, axis=-1, keepdims=True)
        ln = xc * lax.rsqrt(var + EPS) * lw_ref[...] + lb_ref[...]
        ug = (_silu(ln) * sa_ref[st].astype(F32)).astype(BF16)
        ya = jnp.dot(ug, wco_ref[...], preferred_element_type=F32)
        og = (o_ref[st].astype(F32) * sb_ref[st].astype(F32)).astype(BF16)
        yb = jnp.dot(og, wao_ref[...], preferred_element_type=F32)
        merged = ga_ref[st].astype(F32) * ya + gb_ref[st].astype(F32) * yb
        delta = jnp.dot(merged.astype(BF16), wo_ref[...], preferred_element_type=F32)
        y = x_ref[st] + gate_ref[st] * delta
        y_ref[st] = _rms(y, fw_ref[...])


def _out(cv, sa, o, sb, ga, gb, x, gate, ln_w, ln_b, w_co, w_ao, w_o, fw):
    b, s, d = x.shape
    tok = pl.BlockSpec((OUT_STREAMS, TO, d), lambda bi, i: (bi, i, 0))
    mod = pl.BlockSpec((OUT_STREAMS, 1, d), lambda bi, i: (bi, 0, 0))
    wspec = _const_spec((d, d))
    vec = _const_spec((1, d))
    return pl.pallas_call(
        _out_kernel,
        grid=(b // OUT_STREAMS, s // TO),
        in_specs=[tok, tok, tok, tok, tok, tok, tok, mod, vec, vec, wspec, wspec, wspec, vec],
        out_specs=tok,
        out_shape=jax.ShapeDtypeStruct((b, s, d), F32),
        compiler_params=pltpu.CompilerParams(
            dimension_semantics=("arbitrary", "arbitrary"), vmem_limit_bytes=VMEM_LIMIT),
        name="out",
    )(cv, sa, o, sb, ga, gb, x, gate, ln_w, ln_b, w_co, w_ao, w_o, fw)


def _rot_cols(w):
    half = w.shape[-1] // 2
    return jnp.concatenate([-w[..., half:], w[..., :half]], axis=-1)


def _layer(x, c_act_pad, table, w_ada, b_ada, norm_w, w_in, conv_w, conv_b, conv_ln_w,
           conv_ln_b, w_conv_out, q_norm_w, w_uq, kv_norm_w, w_ukv, w_attn_out, w_out, final_w):
    b, s, d = x.shape
    mod = _ada(c_act_pad, w_ada, b_ada[None, :])[:b]
    shift, scale, gate = (m[:, None, :] for m in jnp.split(mod, 3, axis=-1))

    assert w_in.shape[1] == IN_COLS
    w_in_t = w_in.T
    w_wide = _wprep(w_in_t)
    w_kpe_t = w_in_t[IN_COL_BGATE - QK_ROPE_DIM:IN_COL_BGATE]
    half = QK_ROPE_DIM // 2
    w_lat_t = jnp.concatenate([w_in_t[IN_COL_LATENT:IN_COL_BGATE], -w_kpe_t[half:], w_kpe_t[:half]],
                              axis=0)
    w_lat_t = lax.optimization_barrier(w_lat_t).astype(BF16)

    wq = w_uq.reshape(Q_LORA_RANK, N_HEADS, QK_NOPE_DIM + QK_ROPE_DIM)
    wq_rope = wq[..., QK_NOPE_DIM:]
    w_uq2 = jnp.concatenate([wq, _rot_cols(wq_rope)], axis=-1)
    w_uq2 = w_uq2.reshape(Q_LORA_RANK, N_HEADS * QK_PAD).astype(BF16)
    wkv = w_ukv.reshape(KV_LORA_RANK, N_HEADS, QK_NOPE_DIM + V_DIM)
    w_uk = wkv[..., :QK_NOPE_DIM].reshape(KV_LORA_RANK, N_HEADS * QK_NOPE_DIM).astype(BF16)
    w_vt = wkv[..., QK_NOPE_DIM:].reshape(KV_LORA_RANK, ATTN_WIDTH).T.astype(BF16)

    nw = norm_w[None, :]
    cw = conv_w.reshape(CONV_KERNEL, CONV_WIDTH // CONV_LANES, 1, CONV_LANES).transpose(1, 0, 2, 3)
    cw = jnp.broadcast_to(cw, (CONV_WIDTH // CONV_LANES, CONV_KERNEL, SUBLANES, CONV_LANES))
    cv, sa, sb, ga, gb = _gateconv(x, shift, scale, nw, w_wide, cw, conv_b[None, :])
    q, k, vt = _mla_proj(x, shift, scale, nw, table, w_lat_t, q_norm_w[None, :],
                         kv_norm_w[None, :], w_uq2, w_uk, w_vt)
    o = _attention(q, k, vt)
    return _out(cv, sa, o, sb, ga, gb, x, gate, conv_ln_w[None, :], conv_ln_b[None, :],
                w_conv_out.astype(BF16), w_attn_out.astype(BF16), w_out.astype(BF16),
                final_w[None, :])


def kernel(x, c, positions, w_ada, b_ada, norm_w, w_in, conv_w, conv_b, conv_ln_w, conv_ln_b,
           w_conv_out, q_norm_w, w_uq, kv_norm_w, w_ukv, w_attn_out, w_out, final_norm_w):
    b, s, d = x.shape
    depth = w_ada.shape[0]
    assert depth == 1, "final rmsnorm is fused into the single layer's output kernel"
    inv_freq = ROPE_THETA ** (-jnp.arange(0, QK_ROPE_DIM, 2, dtype=F32) / QK_ROPE_DIM)
    n_freq = inv_freq.shape[0]
    assert ROPE_PACK * n_freq == LANES
    invf = jnp.tile(inv_freq, ROPE_PACK)[None, :]
    pos_dense = jnp.repeat(positions.astype(F32).reshape(b * s // ROPE_PACK, ROPE_PACK), n_freq, axis=1)
    table = _rope_table(pos_dense, invf).reshape(b, s, LANES)
    c_pad = jnp.pad(c, ((0, 8 - b), (0, 0)))
    return _layer(x, c_pad, table, w_ada[0], b_ada[0], norm_w[0], w_in[0], conv_w[0],
                  conv_b[0], conv_ln_w[0], conv_ln_b[0], w_conv_out[0], q_norm_w[0], w_uq[0],
                  kv_norm_w[0], w_ukv[0], w_attn_out[0], w_out[0], final_norm_w)
```
